```python
import numpy as np
import jax
import jax.numpy as jnp
from jax import lax

D_MODEL = 1024
BATCH = 8
SEQ = 2048
DEPTH = 1
DEC_BATCH = 16
DEC_SEQ = 4096
PAST_LEN = 128

RW = D_MODEL // 2
RN = 64
RH = RW // RN
DECAY_LORA = 64
AAA_LORA = 64
GATE_LORA = 160
GN_EPS = 64e-5
GW = D_MODEL - RW
GH = 4
GV = GW // GH
GK = GV // 2
GK_LORA = 16
GATE_NORM = 16.0
CHUNK = 64
GLA_EPS = 1e-5
R_COLS = 3 * RW + 2 * DECAY_LORA + 2 * AAA_LORA + GATE_LORA
G_COLS = 2 * GH * GK + GH * GV + GK_LORA + GH * GV
N_IN = R_COLS + G_COLS
N_MEM = 256
X_HEADS = 4
X_HD = D_MODEL // X_HEADS
D_FF = 4 * D_MODEL
NORM_EPS = 1e-6

kernel_name = 'hymba_rwkv7_gla_sandwich_encoder'


def _split(t, sizes):
    return jnp.split(t, np.cumsum(sizes)[:-1].tolist(), axis=-1)


def _rev(t):
    return jnp.flip(t, axis=1)


def rms_norm(x, g, eps=NORM_EPS):
    xf = x.astype(jnp.float32)
    y = xf * lax.rsqrt(jnp.mean(xf * xf, axis=-1, keepdims=True) + eps)
    return (y * g.astype(jnp.float32)).astype(x.dtype)


def centred_shift(p, mu_prev, mu_next):
    prev = jnp.pad(p[:, :-1], ((0, 0), (1, 0), (0, 0)))
    nxt = jnp.pad(p[:, 1:], ((0, 0), (0, 1), (0, 0)))
    return p + mu_prev * (prev - p) + mu_next * (nxt - p)


def rwkv7_scan(r, w, k, v, kk, a):
    B, T, H, N = r.shape

    def step(S, inp):
        r_t, w_t, k_t, v_t, kk_t, a_t = inp
        sa = jnp.einsum('bhij,bhj->bhi', S, -kk_t)
        S = (S * w_t[:, :, None, :] + sa[..., None] * (kk_t * a_t)[:, :, None, :]
             + v_t[..., None] * k_t[:, :, None, :])
        return S, jnp.einsum('bhij,bhj->bhi', S, r_t)

    xs = (jnp.moveaxis(r, 1, 0), jnp.moveaxis(w, 1, 0), jnp.moveaxis(k, 1, 0),
          jnp.moveaxis(v, 1, 0), jnp.moveaxis(kk, 1, 0), jnp.moveaxis(a, 1, 0))
    _, y = lax.scan(step, jnp.zeros((B, H, N, N), r.dtype), xs)
    return jnp.moveaxis(y, 0, 1)


def _rwkv_direction(rh, k, vh, kk, wd, ad, w0, w2, a0, a2, k_a, r_k, reverse):
    B, T, _ = k.shape
    hd = lambda t: t.reshape(B, T, RH, RN)
    w = -jax.nn.softplus(-(w0 + jnp.tanh(wd) @ w2)) - 0.5
    decay = jnp.exp(-jnp.exp(w))
    a = jax.nn.sigmoid(a0 + ad @ a2)
    kd = hd(k * (1.0 + (a - 1.0) * k_a))
    args = (rh, hd(decay), kd, vh, kk, hd(a))
    if reverse:
        y = _rev(rwkv7_scan(*[_rev(t) for t in args]))
    else:
        y = rwkv7_scan(*args)
    bonus = jnp.sum(rh * kd * r_k, axis=-1, keepdims=True) * vh
    return y, bonus


def rwkv7_group(rw, p):
    B, T, _ = rw.shape
    r, k, v, wd_f, wd_b, ad_f, ad_b, gd = _split(
        rw, (RW, RW, RW, DECAY_LORA, DECAY_LORA, AAA_LORA, AAA_LORA, GATE_LORA))
    hd = lambda t: t.reshape(B, T, RH, RN)
    kk = hd(k * p['k_k'])
    kk = kk * lax.rsqrt(jnp.sum(kk * kk, axis=-1, keepdims=True) + 1e-12)
    rh, vh = hd(r), hd(v)
    y_f, b_f = _rwkv_direction(rh, k, vh, kk, wd_f, ad_f, p['w0_f'], p['w2_f'], p['a0_f'], p['a2_f'],
                               p['k_a'], p['r_k'], False)
    y_b, b_b = _rwkv_direction(rh, k, vh, kk, wd_b, ad_b, p['w0_b'], p['w2_b'], p['a0_b'], p['a2_b'],
                               p['k_a'], p['r_k'], True)
    y = y_f + y_b
    mean = jnp.mean(y, axis=-1, keepdims=True)
    var = jnp.mean(jnp.square(y - mean), axis=-1, keepdims=True)
    gn = ((y - mean) * lax.rsqrt(var + GN_EPS)).reshape(B, T, RW) * p['lnx_w'] + p['lnx_b']
    g = jax.nn.sigmoid(gd) @ p['g2']
    return (gn + (b_f + b_b).reshape(B, T, RW)) * g


def gla_chunked(q, k, v, lg):
    B, T, H, dk = q.shape
    dv = v.shape[-1]
    n = T // CHUNK
    c = lambda t: t.reshape(B, n, CHUNK, H, t.shape[-1])
    q, k, v, lg = c(q), c(k), c(v), c(lg)
    b = jnp.cumsum(lg, axis=2)
    b_last = b[:, :, -1]
    q_in = q * jnp.exp(b)
    k_in = k * jnp.exp(-b)
    mask = jnp.tril(jnp.ones((CHUNK, CHUNK), dtype=bool))
    A = jnp.where(mask, jnp.einsum('bnthd,bnshd->bnhts', q_in, k_in), 0.0)
    o_intra = jnp.einsum('bnhts,bnshe->bnthe', A, v)
    dS = jnp.einsum('bnshd,bnshe->bnhde', k * jnp.exp(b_last[:, :, None] - b), v)

    def step(S, inp):
        dec, ds = inp
        return dec[..., None] * S + ds, S

    _, S_prev = lax.scan(step, jnp.zeros((B, H, dk, dv), q.dtype),
                         (jnp.moveaxis(jnp.exp(b_last), 1, 0), jnp.moveaxis(dS, 1, 0)))
    o_inter = jnp.einsum('bnthd,bnhde->bnthe', q_in, jnp.moveaxis(S_prev, 0, 1))
    return (o_intra + o_inter).reshape(B, T, H, dv)


def gla_group(gl, p):
    B, T, _ = gl.shape
    q, k, v, gkd, gg = _split(gl, (GH * GK, GH * GK, GH * GV, GK_LORA, GH * GV))
    q = q.reshape(B, T, GH, GK) * (GK ** -0.5)
    k = k.reshape(B, T, GH, GK)
    v = v.reshape(B, T, GH, GV)
    lg = lambda w2, bias: (jax.nn.log_sigmoid(gkd @ w2 + bias) / GATE_NORM).reshape(B, T, GH, GK)
    o = (gla_chunked(q, k, v, lg(p['gk2_f'], p['gkb_f']))
         + _rev(gla_chunked(_rev(q), _rev(k), _rev(v), _rev(lg(p['gk2_b'], p['gkb_b'])))))
    o = o * lax.rsqrt(jnp.mean(o * o, axis=-1, keepdims=True) + GLA_EPS) * p['gla_norm_w']
    return (o * jax.nn.silu(gg.reshape(B, T, GH, GV))).reshape(B, T, GH * GV)


def hybrid_mixer(h, p):
    proj = h @ p['w_in']
    rw = centred_shift(proj[..., :R_COLS], p['mu_prev'], p['mu_next']).astype(jnp.float32)
    gl = proj[..., R_COLS:].astype(jnp.float32)
    mixed = jnp.concatenate([rwkv7_group(rw, p), gla_group(gl, p)], axis=-1).astype(h.dtype)
    return mixed @ p['w_out']


def cross_attention(h, m, p):
    B, S, _ = h.shape
    M = m.shape[1]
    q = (h @ p['wq_x']).reshape(B, S, X_HEADS, X_HD)
    kv = (m @ p['wkv_x']).reshape(B, M, 2, X_HEADS, X_HD)
    k, v = kv[:, :, 0], kv[:, :, 1]
    s = jnp.einsum('bqhd,bkhd->bhqk', q.astype(jnp.float32), k.astype(jnp.float32)) * (X_HD ** -0.5)
    pr = jax.nn.softmax(s, axis=-1).astype(v.dtype)
    o = jnp.einsum('bhqk,bkhd->bqhd', pr, v).reshape(B, S, D_MODEL)
    return o @ p['wo_x']


def encoder_trunk(x, mem, params):
    for l in range(DEPTH):
        p = {name: w[l] for name, w in params.items()}
        x = x + rms_norm(hybrid_mixer(rms_norm(x, p['g_mix_pre']), p), p['g_mix_post'])
        m = rms_norm(mem, p['g_mem'])
        x = x + rms_norm(cross_attention(rms_norm(x, p['g_x_pre']), m, p), p['g_x_post'])
        h = rms_norm(x, p['g_ffn_pre'])
        x = x + rms_norm(jnp.square(jax.nn.relu(h @ p['w_ff1'])) @ p['w_ff2'], p['g_ffn_post'])
    return x


def setup_inputs(seed: int = 0) -> dict:
    key = jax.random.key(seed)
    ks = jax.random.split(key, 48)
    cnt = [0]

    def nk():
        cnt[0] += 1
        return ks[cnt[0] - 1]

    f32 = jnp.float32
    L = DEPTH

    def nrm(shape, scale):
        return scale * jax.random.normal(nk(), shape, f32)

    def gain(n):
        return 1.0 + nrm((L, n), 0.05)

    def unif(shape, lo, hi):
        return jax.random.uniform(nk(), shape, f32, lo, hi)

    return {
        'x_prompt': nrm((BATCH, SEQ, D_MODEL), 1.0),
        'x_sample': nrm((DEC_BATCH, DEC_SEQ, D_MODEL), 1.0),
        'mem_prompt': nrm((BATCH, N_MEM, D_MODEL), 1.0),
        'mem_sample': nrm((DEC_BATCH, N_MEM, D_MODEL), 1.0),
        'g_mix_pre': gain(D_MODEL),
        'w_in': nrm((L, D_MODEL, N_IN), D_MODEL ** -0.5),
        'mu_prev': unif((L, R_COLS), 0.0, 0.5),
        'mu_next': unif((L, R_COLS), 0.0, 0.5),
        'w0_f': unif((L, RW), -6.0, 0.0),
        'w2_f': nrm((L, DECAY_LORA, RW), 0.1 * DECAY_LORA ** -0.5),
        'w0_b': unif((L, RW), -6.0, 0.0),
        'w2_b': nrm((L, DECAY_LORA, RW), 0.1 * DECAY_LORA ** -0.5),
        'a0_f': nrm((L, RW), 0.1),
        'a2_f': nrm((L, AAA_LORA, RW), 0.1 * AAA_LORA ** -0.5),
        'a0_b': nrm((L, RW), 0.1),
        'a2_b': nrm((L, AAA_LORA, RW), 0.1 * AAA_LORA ** -0.5),
        'g2': nrm((L, GATE_LORA, RW), GATE_LORA ** -0.5),
        'k_k': 0.85 + nrm((L, RW), 0.05),
        'k_a': 1.0 + nrm((L, RW), 0.05),
        'r_k': nrm((L, RH, RN), 0.1),
        'lnx_w': gain(RW),
        'lnx_b': nrm((L, RW), 0.02),
        'gk2_f': nrm((L, GK_LORA, GH * GK), GK_LORA ** -0.5),
        'gkb_f': nrm((L, GH * GK), 0.1),
        'gk2_b': nrm((L, GK_LORA, GH * GK), GK_LORA ** -0.5),
        'gkb_b': nrm((L, GH * GK), 0.1),
        'gla_norm_w': gain(GV),
        'w_out': nrm((L, D_MODEL, D_MODEL), D_MODEL ** -0.5),
        'g_mix_post': gain(D_MODEL),
        'g_x_pre': gain(D_MODEL),
        'g_mem': gain(D_MODEL),
        'wq_x': nrm((L, D_MODEL, D_MODEL), D_MODEL ** -0.5),
        'wkv_x': nrm((L, D_MODEL, 2 * D_MODEL), D_MODEL ** -0.5),
        'wo_x': nrm((L, D_MODEL, D_MODEL), D_MODEL ** -0.5),
        'g_x_post': gain(D_MODEL),
        'g_ffn_pre': gain(D_MODEL),
        'w_ff1': nrm((L, D_MODEL, D_FF), D_MODEL ** -0.5),
        'w_ff2': nrm((L, D_FF, D_MODEL), D_FF ** -0.5),
        'g_ffn_post': gain(D_MODEL),
    }


def reference(x_prompt, x_sample, mem_prompt, mem_sample, g_mix_pre, w_in, mu_prev, mu_next,
              w0_f, w2_f, w0_b, w2_b, a0_f, a2_f, a0_b, a2_b, g2, k_k, k_a, r_k, lnx_w, lnx_b,
              gk2_f, gkb_f, gk2_b, gkb_b, gla_norm_w, w_out, g_mix_post, g_x_pre, g_mem,
              wq_x, wkv_x, wo_x, g_x_post, g_ffn_pre, w_ff1, w_ff2, g_ffn_post):
    params = {
        'g_mix_pre': g_mix_pre, 'w_in': w_in, 'mu_prev': mu_prev, 'mu_next': mu_next,
        'w0_f': w0_f, 'w2_f': w2_f, 'w0_b': w0_b, 'w2_b': w2_b,
        'a0_f': a0_f, 'a2_f': a2_f, 'a0_b': a0_b, 'a2_b': a2_b,
        'g2': g2, 'k_k': k_k, 'k_a': k_a, 'r_k': r_k, 'lnx_w': lnx_w, 'lnx_b': lnx_b,
        'gk2_f': gk2_f, 'gkb_f': gkb_f, 'gk2_b': gk2_b, 'gkb_b': gkb_b, 'gla_norm_w': gla_norm_w,
        'w_out': w_out, 'g_mix_post': g_mix_post, 'g_x_pre': g_x_pre, 'g_mem': g_mem,
        'wq_x': wq_x, 'wkv_x': wkv_x, 'wo_x': wo_x, 'g_x_post': g_x_post,
        'g_ffn_pre': g_ffn_pre, 'w_ff1': w_ff1, 'w_ff2': w_ff2, 'g_ffn_post': g_ffn_post,
    }
    y_prompt = encoder_trunk(x_prompt, mem_prompt, params)
    y_sample = encoder_trunk(x_sample, mem_sample, params)
    return (y_prompt, y_sample)
```

```python
import functools
import math

import jax
import jax.numpy as jnp
from jax import lax
from jax.experimental import pallas as pl
from jax.experimental.pallas import tpu as pltpu

F32 = jnp.float32
BF16 = jnp.bfloat16

D_MODEL = 1024
RW = 512
RN = 64
DECAY_LORA = 64
AAA_LORA = 64
GATE_LORA = 160
GN_EPS = 64e-5
GH = 4
GV = 128
GK = 64
GK_LORA = 16
GATE_NORM = 16.0
CHUNK = 64
GLA_EPS = 1e-5
R_COLS = 3 * RW + 2 * DECAY_LORA + 2 * AAA_LORA + GATE_LORA
N_MEM = 256
X_HEADS = 4
X_HD = D_MODEL // X_HEADS
D_FF = 4 * D_MODEL
NORM_EPS = 1e-6

LANES = 128
LORA_W = 512
GKD_W = 128
C_RKV, C_LORA, C_QK, C_GV, C_GG, C_GKD = 0, 1536, 2048, 2560, 3072, 3584
N_PROJ = 3712
VMEM_LIMIT = 56 * 1024 * 1024


def _cparams(sem):
    return pltpu.CompilerParams(dimension_semantics=sem, vmem_limit_bytes=VMEM_LIMIT)


def _mm(a, b):
    return jnp.dot(a.astype(BF16), b.astype(BF16), preferred_element_type=F32)


def _mm_nt(a, b):
    return lax.dot_general(a.astype(BF16), b.astype(BF16), (((1,), (1,)), ((), ())),
                           preferred_element_type=F32)


def _mm_tn(a, b):
    return lax.dot_general(a.astype(BF16), b.astype(BF16), (((0,), (0,)), ((), ())),
                           preferred_element_type=F32)


def _split2(x):
    hi = x.astype(BF16)
    lo = (x - hi.astype(F32)).astype(BF16)
    return hi, lo


def _mm_hp(a, b):
    ah, al = _split2(a)
    bh, bl = _split2(b)
    d = lambda x, y: jnp.dot(x, y, preferred_element_type=F32)
    return d(ah, bh) + (d(ah, bl) + d(al, bh))


def _mm_lhs01(m01, x):
    x1 = x.astype(BF16)
    r1 = x - x1.astype(F32)
    x2 = r1.astype(BF16)
    x3 = (r1 - x2.astype(F32)).astype(BF16)
    d = lambda y: jnp.dot(m01, y, preferred_element_type=F32)
    return d(x1) + (d(x2) + d(x3))


def _mm_rhs01(x, m01):
    x1 = x.astype(BF16)
    r1 = x - x1.astype(F32)
    x2 = r1.astype(BF16)
    x3 = (r1 - x2.astype(F32)).astype(BF16)
    d = lambda y: jnp.dot(y, m01, preferred_element_type=F32)
    return d(x1) + (d(x2) + d(x3))


def _rms(x, g, eps):
    return x * lax.rsqrt(jnp.mean(x * x, axis=-1, keepdims=True) + eps) * g


def _sigmoid(x):
    return 1.0 / (1.0 + jnp.exp(-x))


def _iota2(shape, dim):
    return lax.broadcasted_iota(jnp.int32, shape, dim)


def _shifted_rows(ref, t0, nrows, seq_len, mu_p, mu_n):
    cur = ref[0, pl.ds(t0, nrows), :]
    pb = ref[0, pl.ds(pl.multiple_of(jnp.maximum(t0 - 8, 0), 8), 8), :]
    nb = ref[0, pl.ds(pl.multiple_of(jnp.minimum(t0 + nrows, seq_len - 8), 8), 8), :]
    prow = jnp.where(t0 > 0, pb[7:8, :], 0.0)
    nrow = jnp.where(t0 + nrows < seq_len, nb[0:1, :], 0.0)
    rid = _iota2(cur.shape, 0)
    prev = jnp.where(rid == 0, prow, pltpu.roll(cur, 1, 0))
    nxt = jnp.where(rid == nrows - 1, nrow, pltpu.roll(cur, nrows - 1, 0))
    return cur + mu_p * (prev - cur) + mu_n * (nxt - cur)


def _chunk_tri(n, reverse):
    t = _iota2((n, n), 0)
    s = _iota2((n, n), 1)
    same = (t >> 6) == (s >> 6)
    tri = (s >= t) if reverse else (s <= t)
    return jnp.where(same & tri, 1.0, 0.0).astype(BF16)


def _inproj_body(x_ref, g_ref, w_ref, rkv_ref, lora_ref, qk_ref, gv_ref, gg_ref, gkd_ref):
    h = _rms(x_ref[...], g_ref[...], NORM_EPS).astype(BF16)

    def proj(lo, hi):
        return jnp.dot(h, w_ref[:, lo:hi], preferred_element_type=F32)

    rkv_ref[...] = proj(C_RKV, C_LORA)
    lora_ref[...] = proj(C_LORA, C_QK)
    qk_ref[...] = proj(C_QK, C_GV)
    gv_ref[...] = proj(C_GV, C_GG)
    gg_ref[...] = proj(C_GG, C_GKD)
    gkd_ref[...] = proj(C_GKD, N_PROJ)


def _inproj(x2d, g, w_perm, tm):
    n = x2d.shape[0]
    widths = (C_LORA - C_RKV, C_QK - C_LORA, C_GV - C_QK, C_GG - C_GV, C_GKD - C_GG, N_PROJ - C_GKD)
    return pl.pallas_call(
        _inproj_body,
        grid=(n // tm,),
        in_specs=[pl.BlockSpec((tm, D_MODEL), lambda i: (i, 0)),
                  pl.BlockSpec((1, D_MODEL), lambda i: (0, 0)),
                  pl.BlockSpec((D_MODEL, N_PROJ), lambda i: (0, 0))],
        out_specs=[pl.BlockSpec((tm, w), lambda i: (i, 0)) for w in widths],
        out_shape=[jax.ShapeDtypeStruct((n, w), F32) for w in widths],
        compiler_params=_cparams(("parallel",)),
        name="inproj",
    )(x2d, g, w_perm)


(PV_MUP_R, PV_MUN_R, PV_MUP_K, PV_MUN_K, PV_MUP_V, PV_MUN_V, PV_W0_F, PV_W0_B, PV_A0_F, PV_A0_B,
 PV_KK, PV_KA, PV_RK, PV_LNW, PV_LNB) = range(15)
PV_ROWS = 16


def _rwkv_body(r_ref, k_ref, v_ref, lo_ref, pv_ref, mul_ref, ww_ref, wa_ref, wg_ref, out_ref,
               y_scr, bon_scr, s_scr, *, seq_len, blk):
    C = CHUNK
    nc = blk // C
    nblk = seq_len // blk
    pv = pv_ref[...]
    prow = lambda i: pv[i:i + 1, :]
    mul = mul_ref[...]

    lane = _iota2((1, LANES), 1)
    head_mask = [lane < RN, lane >= RN]
    li = _iota2((LANES, LANES), 0)
    lj = _iota2((LANES, LANES), 1)
    same_head = (li >> 6) == (lj >> 6)
    seg01 = jnp.where(same_head, 1.0, 0.0).astype(BF16)
    eye128 = li == lj
    ct = _iota2((C, C), 0)
    cs = _iota2((C, C), 1)
    eye64 = jnp.where(ct == cs, 1.0, 0.0)

    def seg_sum(x):
        return _mm_rhs01(x, seg01)

    def block_inputs(t0, reverse):
        d = 1 if reverse else 0
        r = _shifted_rows(r_ref, t0, blk, seq_len, prow(PV_MUP_R), prow(PV_MUN_R))
        k = _shifted_rows(k_ref, t0, blk, seq_len, prow(PV_MUP_K), prow(PV_MUN_K))
        v = _shifted_rows(v_ref, t0, blk, seq_len, prow(PV_MUP_V), prow(PV_MUN_V))
        lo = _shifted_rows(lo_ref, t0, blk, seq_len, mul[0:1, :], mul[1:2, :])
        zw = _mm(jnp.tanh(lo[:, 0:LANES]), ww_ref[0, d])
        za = _mm(lo[:, LANES:2 * LANES], wa_ref[0, d])
        lw = -_sigmoid(prow(PV_W0_F + d) + zw) * math.exp(-0.5)
        a = _sigmoid(prow(PV_A0_F + d) + za)
        kd = k * (1.0 + (a - 1.0) * prow(PV_KA))
        kk = k * prow(PV_KK)
        kk = kk * lax.rsqrt(seg_sum(kk * kk) + 1e-12)
        bonus = seg_sum(r * kd * prow(PV_RK)) * v
        g = _mm_lhs01(_chunk_tri(blk, reverse), lw)
        return dict(r=r, v=v, kd=kd, kk=kk, a=a, lw=lw, g=g, bonus=bonus, lo=lo)

    def chunk_terms(q, c, reverse):
        sl = slice(c * C, (c + 1) * C)
        g = q["g"][sl]
        lw = q["lw"][sl]
        kk = q["kk"][sl]
        r, v, kd, a = q["r"][sl], q["v"][sl], q["kd"][sl], q["a"][sl]
        g_tot = g[0:1, :] if reverse else g[C - 1:C, :]
        eg = jnp.exp(g)
        eng = jnp.exp(-g)
        rt = r * eg
        at = -kk * jnp.exp(g - lw)
        beta = kk * a
        bt = beta * eng
        kt = kd * eng
        dec = jnp.exp(g_tot - g)
        bh = beta * dec
        kh = kd * dec
        gam = jnp.exp(g_tot)
        strict = (cs > ct) if reverse else (cs < ct)
        incl = (cs >= ct) if reverse else (cs <= ct)
        x2 = jnp.concatenate([at, rt], axis=0)
        ws, u0s, qs, y0s = [], [], [], []
        for h in range(2):
            x2h = jnp.where(head_mask[h], x2, 0.0)
            gb = _mm_nt(x2h, bt)
            gk = _mm_nt(x2h, kt)
            a_ab = jnp.where(strict, gb[:C], 0.0)
            a_rb = jnp.where(incl, gb[C:], 0.0)
            a_ak = jnp.where(strict, gk[:C], 0.0)
            a_rk = jnp.where(incl, gk[C:], 0.0)
            tm = eye64 + jnp.where((ct >> 1) == (cs >> 1), a_ab, 0.0)
            for lvl in range(1, 6):
                same = (ct >> (lvl + 1)) == (cs >> (lvl + 1))
                if reverse:
                    off = (((ct >> lvl) & 1) == 0) & (((cs >> lvl) & 1) == 1)
                else:
                    off = (((ct >> lvl) & 1) == 1) & (((cs >> lvl) & 1) == 0)
                tm = tm + _mm_hp(_mm_hp(tm, jnp.where(same & off, a_ab, 0.0)), tm)
            akv = _mm(a_ak, v)
            wu = _mm(tm, jnp.concatenate([at, akv], axis=1))
            qy = _mm(a_rb, wu)
            ws.append(wu[:, :LANES])
            u0s.append(wu[:, LANES:])
            qs.append(rt + qy[:, :LANES])
            y0s.append(qy[:, LANES:] + _mm(a_rk, v))
        sel = lambda xs: jnp.where(head_mask[0], xs[0], xs[1])
        w, u0, qh, y0 = sel(ws), sel(u0s), sel(qs), sel(y0s)
        p = jnp.where(eye128, gam, 0.0) + jnp.where(same_head, _mm_tn(w, bh), 0.0)
        n0t = jnp.where(same_head,
                        _mm_tn(jnp.concatenate([u0, v], axis=0), jnp.concatenate([bh, kh], axis=0)),
                        0.0)
        return qh, y0, p, n0t

    def run_block(t0, reverse):
        q = block_inputs(t0, reverse)
        order = range(nc - 1, -1, -1) if reverse else range(nc)
        terms = {c: chunk_terms(q, c, reverse) for c in order}
        s = s_scr[...]
        ys = [None] * nc
        for c in order:
            qh, y0, p, n0t = terms[c]
            ys[c] = _mm_nt(qh, s) + y0
            s = _mm_hp(s, p) + n0t
        s_scr[...] = s
        return q, jnp.concatenate(ys, axis=0)

    s_scr[...] = jnp.zeros(s_scr.shape, F32)

    def fwd_step(i, carry):
        t0 = pl.multiple_of(i * blk, blk)
        q, y = run_block(t0, False)
        y_scr[pl.ds(t0, blk), :] = y
        bon_scr[pl.ds(t0, blk), :] = q["bonus"]
        return carry

    lax.fori_loop(0, nblk, fwd_step, 0)

    s_scr[...] = jnp.zeros(s_scr.shape, F32)

    def bwd_step(i, carry):
        t0 = pl.multiple_of((nblk - 1 - i) * blk, blk)
        q, y = run_block(t0, True)
        y = y + y_scr[pl.ds(t0, blk), :]
        mean = seg_sum(y) * (1.0 / RN)
        dy = y - mean
        var = seg_sum(dy * dy) * (1.0 / RN)
        gn = dy * lax.rsqrt(var + GN_EPS) * prow(PV_LNW) + prow(PV_LNB)
        gate = _mm(_sigmoid(q["lo"][:, 2 * LANES:]), wg_ref[0])
        res = (gn + (bon_scr[pl.ds(t0, blk), :] + q["bonus"])) * gate
        out_ref[0, pl.ds(t0, blk), :] = res.astype(out_ref.dtype)
        return carry

    lax.fori_loop(0, nblk, bwd_step, 0)


def _rwkv(rkv, lora, pvec, mul, ww, wa, wg, blk):
    b, t, _ = rkv.shape
    nhp = RW // LANES
    col = lambda off: pl.BlockSpec((1, t, LANES), lambda i, j, off=off: (i, 0, off + j))
    body = functools.partial(_rwkv_body, seq_len=t, blk=blk)
    return pl.pallas_call(
        body,
        grid=(b, nhp),
        in_specs=[col(0), col(nhp), col(2 * nhp),
                  pl.BlockSpec((1, t, LORA_W), lambda i, j: (i, 0, 0)),
                  pl.BlockSpec((PV_ROWS, LANES), lambda i, j: (0, j)),
                  pl.BlockSpec((2, LORA_W), lambda i, j: (0, 0)),
                  pl.BlockSpec((1, 2, LANES, LANES), lambda i, j: (j, 0, 0, 0)),
                  pl.BlockSpec((1, 2, LANES, LANES), lambda i, j: (j, 0, 0, 0)),
                  pl.BlockSpec((1, 2 * LANES, LANES), lambda i, j: (j, 0, 0))],
        out_specs=pl.BlockSpec((1, t, LANES), lambda i, j: (i, 0, j)),
        out_shape=jax.ShapeDtypeStruct((b, t, RW), BF16),
        scratch_shapes=[pltpu.VMEM((t, LANES), F32), pltpu.VMEM((t, LANES), F32),
                        pltpu.VMEM((LANES, LANES), F32)],
        compiler_params=_cparams(("parallel", "parallel")),
        name="rwkv",
    )(rkv, rkv, rkv, lora, pvec, mul, ww, wa, wg)


def _gla_body(q_ref, k_ref, v_ref, gg_ref, gkd_ref, wgk_ref, gkb_ref, nw_ref, out_ref,
              o_scr, s_scr, *, seq_len, blk):
    C = CHUNK
    nc = blk // C
    nblk = seq_len // blk
    lane = _iota2((1, LANES), 1)
    head_mask = [lane < GK, lane >= GK]
    si = _iota2((2 * GV, LANES), 0)
    sj = _iota2((2 * GV, LANES), 1)
    same_head = (si >> 7) == (sj >> 6)
    ct = _iota2((C, C), 0)
    cs = _iota2((C, C), 1)

    def run_block(t0, reverse):
        d = 1 if reverse else 0
        rows = pl.ds(t0, blk)
        q = q_ref[0, rows, :] * (GK ** -0.5)
        k = k_ref[0, rows, :]
        v = v_ref[0, rows, :]
        z = _mm(gkd_ref[0, rows, :], wgk_ref[0, d]) + gkb_ref[0, d]
        lg = (jnp.minimum(z, 0.0) - jnp.log1p(jnp.exp(-jnp.abs(z)))) * (1.0 / GATE_NORM)
        bcum = _mm_lhs01(_chunk_tri(blk, reverse), lg)
        incl = (cs >= ct) if reverse else (cs <= ct)
        s = s_scr[...]
        outs = [None] * nc
        for c in (range(nc - 1, -1, -1) if reverse else range(nc)):
            sl = slice(c * C, (c + 1) * C)
            bc = bcum[sl]
            b_tot = bc[0:1, :] if reverse else bc[C - 1:C, :]
            q_in = q[sl] * jnp.exp(bc)
            k_in = k[sl] * jnp.exp(-bc)
            k_dec = k[sl] * jnp.exp(b_tot - bc)
            vc = v[sl]
            o = _mm_nt(q_in, s)
            intra = []
            for h in range(2):
                att = jnp.where(incl, _mm_nt(jnp.where(head_mask[h], q_in, 0.0), k_in), 0.0)
                intra.append(_mm(att, vc[:, h * GV:(h + 1) * GV]))
            outs[c] = o + jnp.concatenate(intra, axis=1)
            s = s * jnp.exp(b_tot) + jnp.where(same_head, _mm_tn(vc, k_dec), 0.0)
        s_scr[...] = s
        return jnp.concatenate(outs, axis=0)

    s_scr[...] = jnp.zeros(s_scr.shape, F32)

    def fwd_step(i, carry):
        t0 = pl.multiple_of(i * blk, blk)
        o_scr[pl.ds(t0, blk), :] = run_block(t0, False)
        return carry

    lax.fori_loop(0, nblk, fwd_step, 0)

    s_scr[...] = jnp.zeros(s_scr.shape, F32)

    def bwd_step(i, carry):
        t0 = pl.multiple_of((nblk - 1 - i) * blk, blk)
        o = run_block(t0, True) + o_scr[pl.ds(t0, blk), :]
        gg = gg_ref[0, pl.ds(t0, blk), :]
        res = []
        for h in range(2):
            oh = o[:, h * GV:(h + 1) * GV]
            oh = oh * lax.rsqrt(jnp.mean(oh * oh, axis=-1, keepdims=True) + GLA_EPS) * nw_ref[...]
            gh = gg[:, h * GV:(h + 1) * GV]
            res.append(oh * (gh * _sigmoid(gh)))
        out_ref[0, pl.ds(t0, blk), :] = jnp.concatenate(res, axis=1).astype(out_ref.dtype)
        return carry

    lax.fori_loop(0, nblk, bwd_step, 0)


def _gla(qk, gv, gg, gkd, wgk, gkb, nw, blk):
    b, t, _ = qk.shape
    npair = GH // 2
    body = functools.partial(_gla_body, seq_len=t, blk=blk)
    return pl.pallas_call(
        body,
        grid=(b, npair),
        in_specs=[pl.BlockSpec((1, t, LANES), lambda i, j: (i, 0, j)),
                  pl.BlockSpec((1, t, LANES), lambda i, j: (i, 0, npair + j)),
                  pl.BlockSpec((1, t, 2 * GV), lambda i, j: (i, 0, j)),
                  pl.BlockSpec((1, t, 2 * GV), lambda i, j: (i, 0, j)),
                  pl.BlockSpec((1, t, GKD_W), lambda i, j: (i, 0, 0)),
                  pl.BlockSpec((1, 2, GKD_W, LANES), lambda i, j: (j, 0, 0, 0)),
                  pl.BlockSpec((1, 2, 1, LANES), lambda i, j: (j, 0, 0, 0)),
                  pl.BlockSpec((1, GV), lambda i, j: (0, 0))],
        out_specs=pl.BlockSpec((1, t, 2 * GV), lambda i, j: (i, 0, j)),
        out_shape=jax.ShapeDtypeStruct((b, t, GH * GV), BF16),
        scratch_shapes=[pltpu.VMEM((t, 2 * GV), F32), pltpu.VMEM((2 * GV, LANES), F32)],
        compiler_params=_cparams(("parallel", "parallel")),
        name="gla",
    )(qk, qk, gv, gg, gkd, wgk, gkb, nw)


def _outproj_body(x_ref, rw_ref, gl_ref, w_ref, g_ref, o_ref):
    m = (jnp.dot(rw_ref[...], w_ref[0:RW, :], preferred_element_type=F32)
         + jnp.dot(gl_ref[...], w_ref[RW:, :], preferred_element_type=F32))
    o_ref[...] = x_ref[...] + _rms(m, g_ref[...], NORM_EPS)


def _outproj(x2d, rw2d, gl2d, w_out, g_post, tm):
    n = x2d.shape[0]
    return pl.pallas_call(
        _outproj_body,
        grid=(n // tm,),
        in_specs=[pl.BlockSpec((tm, D_MODEL), lambda i: (i, 0)),
                  pl.BlockSpec((tm, RW), lambda i: (i, 0)),
                  pl.BlockSpec((tm, GH * GV), lambda i: (i, 0)),
                  pl.BlockSpec((D_MODEL, D_MODEL), lambda i: (0, 0)),
                  pl.BlockSpec((1, D_MODEL), lambda i: (0, 0))],
        out_specs=pl.BlockSpec((tm, D_MODEL), lambda i: (i, 0)),
        out_shape=jax.ShapeDtypeStruct((n, D_MODEL), F32),
        compiler_params=_cparams(("parallel",)),
        name="outproj",
    )(x2d, rw2d, gl2d, w_out, g_post)


def _kvproj_body(m_ref, g_ref, w_ref, kv_ref):
    h = _rms(m_ref[...], g_ref[...], NORM_EPS).astype(BF16)
    kv_ref[...] = jnp.dot(h, w_ref[...], preferred_element_type=F32).astype(kv_ref.dtype)


def _kvproj(mem2d, g_mem, wkv, tm):
    n = mem2d.shape[0]
    return pl.pallas_call(
        _kvproj_body,
        grid=(n // tm,),
        in_specs=[pl.BlockSpec((tm, D_MODEL), lambda i: (i, 0)),
                  pl.BlockSpec((1, D_MODEL), lambda i: (0, 0)),
                  pl.BlockSpec((D_MODEL, 2 * D_MODEL), lambda i: (0, 0))],
        out_specs=pl.BlockSpec((tm, 2 * D_MODEL), lambda i: (i, 0)),
        out_shape=jax.ShapeDtypeStruct((n, 2 * D_MODEL), BF16),
        compiler_params=_cparams(("parallel",)),
        name="kvproj",
    )(mem2d, g_mem, wkv)


def _xattn_body(x_ref, kv_ref, gpre_ref, wq_ref, wo_ref, gpost_ref, o_ref):
    x = x_ref[0]
    h = _rms(x, gpre_ref[...], NORM_EPS).astype(BF16)
    q = jnp.dot(h, wq_ref[...], preferred_element_type=F32).astype(BF16)
    acc = jnp.zeros(x.shape, F32)
    for hd in range(X_HEADS):
        cols = slice(hd * X_HD, (hd + 1) * X_HD)
        kh = kv_ref[0, :, hd * X_HD:(hd + 1) * X_HD]
        vh = kv_ref[0, :, D_MODEL + hd * X_HD:D_MODEL + (hd + 1) * X_HD]
        s = lax.dot_general(q[:, cols], kh, (((1,), (1,)), ((), ())),
                            preferred_element_type=F32) * (X_HD ** -0.5)
        e = jnp.exp(s - jnp.max(s, axis=-1, keepdims=True))
        p = e / jnp.sum(e, axis=-1, keepdims=True)
        oh = jnp.dot(p.astype(BF16), vh, preferred_element_type=F32)
        acc = acc + jnp.dot(oh.astype(BF16), wo_ref[cols, :], preferred_element_type=F32)
    o_ref[0] = x + _rms(acc, gpost_ref[...], NORM_EPS)


def _xattn(x3d, kv3d, g_pre, wq, wo, g_post, tm):
    b, t, _ = x3d.shape
    return pl.pallas_call(
        _xattn_body,
        grid=(b, t // tm),
        in_specs=[pl.BlockSpec((1, tm, D_MODEL), lambda i, j: (i, j, 0)),
                  pl.BlockSpec((1, N_MEM, 2 * D_MODEL), lambda i, j: (i, 0, 0)),
                  pl.BlockSpec((1, D_MODEL), lambda i, j: (0, 0)),
                  pl.BlockSpec((D_MODEL, D_MODEL), lambda i, j: (0, 0)),
                  pl.BlockSpec((D_MODEL, D_MODEL), lambda i, j: (0, 0)),
                  pl.BlockSpec((1, D_MODEL), lambda i, j: (0, 0))],
        out_specs=pl.BlockSpec((1, tm, D_MODEL), lambda i, j: (i, j, 0)),
        out_shape=jax.ShapeDtypeStruct((b, t, D_MODEL), F32),
        compiler_params=_cparams(("parallel", "parallel")),
        name="xattn",
    )(x3d, kv3d, g_pre, wq, wo, g_post)


def _ffn_body(x_ref, gpre_ref, w1_ref, w2_ref, gpost_ref, o_ref, h_scr, acc_scr):
    j = pl.program_id(1)

    @pl.when(j == 0)
    def _():
        h_scr[...] = _rms(x_ref[...], gpre_ref[...], NORM_EPS).astype(BF16)
        acc_scr[...] = jnp.zeros_like(acc_scr)

    a = jnp.dot(h_scr[...], w1_ref[...], preferred_element_type=F32)
    a = jnp.square(jnp.maximum(a, 0.0)).astype(BF16)
    acc_scr[...] += jnp.dot(a, w2_ref[...], preferred_element_type=F32)

    @pl.when(j == pl.num_programs(1) - 1)
    def _():
        o_ref[...] = x_ref[...] + _rms(acc_scr[...], gpost_ref[...], NORM_EPS)


def _ffn(x2d, g_pre, w1, w2, g_post, tm, tf):
    n = x2d.shape[0]
    return pl.pallas_call(
        _ffn_body,
        grid=(n // tm, D_FF // tf),
        in_specs=[pl.BlockSpec((tm, D_MODEL), lambda i, j: (i, 0)),
                  pl.BlockSpec((1, D_MODEL), lambda i, j: (0, 0)),
                  pl.BlockSpec((D_MODEL, tf), lambda i, j: (0, j)),
                  pl.BlockSpec((tf, D_MODEL), lambda i, j: (j, 0)),
                  pl.BlockSpec((1, D_MODEL), lambda i, j: (0, 0))],
        out_specs=pl.BlockSpec((tm, D_MODEL), lambda i, j: (i, 0)),
        out_shape=jax.ShapeDtypeStruct((n, D_MODEL), F32),
        scratch_shapes=[pltpu.VMEM((tm, D_MODEL), BF16), pltpu.VMEM((tm, D_MODEL), F32)],
        compiler_params=_cparams(("parallel", "arbitrary")),
        name="ffn",
    )(x2d, g_pre, w1, w2, g_post)


def _pack_params(p):
    w_in = p["w_in"]
    g0 = R_COLS
    zeros = lambda n: jnp.zeros((D_MODEL, n), w_in.dtype)
    w_perm = jnp.concatenate([
        w_in[:, :R_COLS], zeros(LORA_W - (R_COLS - 3 * RW)),
        w_in[:, g0:g0 + 2 * GH * GK],
        w_in[:, g0 + 2 * GH * GK:g0 + 2 * GH * GK + GH * GV],
        w_in[:, g0 + 2 * GH * GK + GH * GV + GK_LORA:],
        w_in[:, g0 + 2 * GH * GK + GH * GV:g0 + 2 * GH * GK + GH * GV + GK_LORA], zeros(GKD_W - GK_LORA),
    ], axis=1).astype(BF16)
    row = lambda v: v.reshape(1, -1)
    mu_p, mu_n = p["mu_prev"], p["mu_next"]
    pvec = jnp.concatenate([
        row(mu_p[0:RW]), row(mu_n[0:RW]), row(mu_p[RW:2 * RW]), row(mu_n[RW:2 * RW]),
        row(mu_p[2 * RW:3 * RW]), row(mu_n[2 * RW:3 * RW]),
        row(p["w0_f"]), row(p["w0_b"]), row(p["a0_f"]), row(p["a0_b"]),
        row(p["k_k"]), row(p["k_a"]), row(p["r_k"]), row(p["lnx_w"]), row(p["lnx_b"]),
        jnp.zeros((1, RW), F32)], axis=0)
    pad_l = lambda v: jnp.pad(v[3 * RW:], (0, LORA_W - (R_COLS - 3 * RW)))
    mul = jnp.stack([pad_l(mu_p), pad_l(mu_n)], axis=0)
    nhp = RW // LANES

    def lora_pair(wf, wb):
        wf = wf.reshape(-1, nhp, LANES).transpose(1, 0, 2)
        wb = wb.reshape(-1, nhp, LANES).transpose(1, 0, 2)
        z = jnp.zeros_like(wf)
        return jnp.stack([jnp.concatenate([wf, z], axis=1), jnp.concatenate([z, wb], axis=1)],
                         axis=1).astype(BF16)

    ww = lora_pair(p["w2_f"], p["w2_b"])
    wa = lora_pair(p["a2_f"], p["a2_b"])
    wg = jnp.pad(p["g2"], ((0, 2 * LANES - GATE_LORA), (0, 0)))
    wg = wg.reshape(2 * LANES, nhp, LANES).transpose(1, 0, 2).astype(BF16)
    npair = GH // 2

    def gk_pair(w):
        w = jnp.pad(w, ((0, GKD_W - GK_LORA), (0, 0)))
        return w.reshape(GKD_W, npair, LANES).transpose(1, 0, 2)

    wgk = jnp.stack([gk_pair(p["gk2_f"]), gk_pair(p["gk2_b"])], axis=1).astype(BF16)
    gkb = jnp.stack([p["gkb_f"].reshape(npair, 1, LANES), p["gkb_b"].reshape(npair, 1, LANES)], axis=1)
    return dict(
        w_perm=w_perm, pvec=pvec, mul=mul, ww=ww, wa=wa, wg=wg, wgk=wgk, gkb=gkb,
        nw=row(p["gla_norm_w"]),
        g_mix_pre=row(p["g_mix_pre"]), g_mix_post=row(p["g_mix_post"]),
        g_x_pre=row(p["g_x_pre"]), g_x_post=row(p["g_x_post"]), g_mem=row(p["g_mem"]),
        g_ffn_pre=row(p["g_ffn_pre"]), g_ffn_post=row(p["g_ffn_post"]),
        w_out=p["w_out"].astype(BF16), wq=p["wq_x"].astype(BF16), wkv=p["wkv_x"].astype(BF16),
        wo=p["wo_x"].astype(BF16), w1=p["w_ff1"].astype(BF16), w2=p["w_ff2"].astype(BF16))


def _pick(n, pref):
    t = pref
    while n % t:
        t //= 2
    return t


def _trunk(x, mem, pk):
    b, t, _ = x.shape
    n = b * t
    tm = _pick(n, 512)
    x2d = x.reshape(n, D_MODEL)
    rkv, lora, qk, gv, gg, gkd = _inproj(x2d, pk["g_mix_pre"], pk["w_perm"], tm)
    r3 = lambda a: a.reshape(b, t, a.shape[-1])
    blk = _pick(t, 128)
    rw = _rwkv(r3(rkv), r3(lora), pk["pvec"], pk["mul"], pk["ww"], pk["wa"], pk["wg"], blk)
    gl = _gla(r3(qk), r3(gv), r3(gg), r3(gkd), pk["wgk"], pk["gkb"], pk["nw"], blk)
    x1 = _outproj(x2d, rw.reshape(n, RW), gl.reshape(n, GH * GV), pk["w_out"], pk["g_mix_post"], tm)
    nm = mem.shape[0] * mem.shape[1]
    kv = _kvproj(mem.reshape(nm, D_MODEL), pk["g_mem"], pk["wkv"], _pick(nm, 512))
    x2 = _xattn(x1.reshape(b, t, D_MODEL), kv.reshape(mem.shape[0], mem.shape[1], 2 * D_MODEL),
                pk["g_x_pre"], pk["wq"], pk["wo"], pk["g_x_post"], _pick(t, 512))
    y = _ffn(x2.reshape(n, D_MODEL), pk["g_ffn_pre"], pk["w1"], pk["w2"], pk["g_ffn_post"],
             tm, 1024)
    return y.reshape(b, t, D_MODEL)


def kernel(x_prompt, x_sample, mem_prompt, mem_sample, g_mix_pre, w_in, mu_prev, mu_next, w0_f, w2_f, w0_b, w2_b, a0_f, a2_f, a0_b, a2_b, g2, k_k, k_a, r_k, lnx_w, lnx_b, gk2_f, gkb_f, gk2_b, gkb_b, gla_norm_w, w_out, g_mix_post, g_x_pre, g_mem, wq_x, wkv_x, wo_x, g_x_post, g_ffn_pre, w_ff1, w_ff2, g_ffn_post):
    params = dict(
        g_mix_pre=g_mix_pre, w_in=w_in, mu_prev=mu_prev, mu_next=mu_next, w0_f=w0_f, w2_f=w2_f,
        w0_b=w0_b, w2_b=w2_b, a0_f=a0_f, a2_f=a2_f, a0_b=a0_b, a2_b=a2_b, g2=g2, k_k=k_k, k_a=k_a,
        r_k=r_k, lnx_w=lnx_w, lnx_b=lnx_b, gk2_f=gk2_f, gkb_f=gkb_f, gk2_b=gk2_b, gkb_b=gkb_b,
        gla_norm_w=gla_norm_w, w_out=w_out, g_mix_post=g_mix_post, g_x_pre=g_x_pre, g_mem=g_mem,
        wq_x=wq_x, wkv_x=wkv_x, wo_x=wo_x, g_x_post=g_x_post, g_ffn_pre=g_ffn_pre, w_ff1=w_ff1,
        w_ff2=w_ff2, g_ffn_post=g_ffn_post)
    assert w_in.shape[0] == 1, "single-layer stack expected"
    pk = _pack_params({name: w[0] for name, w in params.items()})
    return (_trunk(x_prompt, mem_prompt, pk), _trunk(x_sample, mem_sample, pk))
```

```python
import functools
import math

import jax
import jax.numpy as jnp
from jax import lax
from jax.experimental import pallas as pl
from jax.experimental.pallas import tpu as pltpu

F32 = jnp.float32
BF16 = jnp.bfloat16

D_MODEL = 1024
RW = 512
RN = 64
DECAY_LORA = 64
AAA_LORA = 64
GATE_LORA = 160
GN_EPS = 64e-5
GH = 4
GV = 128
GK = 64
GK_LORA = 16
GATE_NORM = 16.0
CHUNK = 64
GLA_EPS = 1e-5
R_COLS = 3 * RW + 2 * DECAY_LORA + 2 * AAA_LORA + GATE_LORA
N_MEM = 256
X_HEADS = 4
X_HD = D_MODEL // X_HEADS
D_FF = 4 * D_MODEL
NORM_EPS = 1e-6

LANES = 128
LORA_W = 512
GKD_W = 128
C_RKV, C_LORA, C_QK, C_GV, C_GG, C_GKD = 0, 1536, 2048, 2560, 3072, 3584
N_PROJ = 3712
VMEM_LIMIT = 56 * 1024 * 1024


def _cparams(sem):
    return pltpu.CompilerParams(dimension_semantics=sem, vmem_limit_bytes=VMEM_LIMIT)


def _mm(a, b):
    return jnp.dot(a.astype(BF16), b.astype(BF16), preferred_element_type=F32)


def _mm_nt(a, b):
    return lax.dot_general(a.astype(BF16), b.astype(BF16), (((1,), (1,)), ((), ())),
                           preferred_element_type=F32)


def _mm_tn(a, b):
    return lax.dot_general(a.astype(BF16), b.astype(BF16), (((0,), (0,)), ((), ())),
                           preferred_element_type=F32)


def _split2(x):
    hi = x.astype(BF16)
    lo = (x - hi.astype(F32)).astype(BF16)
    return hi, lo


def _mm_hp(a, b):
    ah, al = _split2(a)
    bh, bl = _split2(b)
    d = lambda x, y: jnp.dot(x, y, preferred_element_type=F32)
    return d(ah, bh) + (d(ah, bl) + d(al, bh))


def _mm_lhs01(m01, x):
    x1 = x.astype(BF16)
    r1 = x - x1.astype(F32)
    x2 = r1.astype(BF16)
    x3 = (r1 - x2.astype(F32)).astype(BF16)
    d = lambda y: jnp.dot(m01, y, preferred_element_type=F32)
    return d(x1) + (d(x2) + d(x3))


def _mm_rhs01(x, m01):
    x1 = x.astype(BF16)
    r1 = x - x1.astype(F32)
    x2 = r1.astype(BF16)
    x3 = (r1 - x2.astype(F32)).astype(BF16)
    d = lambda y: jnp.dot(y, m01, preferred_element_type=F32)
    return d(x1) + (d(x2) + d(x3))


def _rms(x, g, eps):
    return x * lax.rsqrt(jnp.mean(x * x, axis=-1, keepdims=True) + eps) * g


def _sigmoid(x):
    return 1.0 / (1.0 + jnp.exp(-x))


def _iota2(shape, dim):
    return lax.broadcasted_iota(jnp.int32, shape, dim)


def _shifted_rows(ref, t0, nrows, seq_len, mu_p, mu_n):
    cur = ref[0, pl.ds(t0, nrows), :]
    pb = ref[0, pl.ds(pl.multiple_of(jnp.maximum(t0 - 8, 0), 8), 8), :]
    nb = ref[0, pl.ds(pl.multiple_of(jnp.minimum(t0 + nrows, seq_len - 8), 8), 8), :]
    prow = jnp.where(t0 > 0, pb[7:8, :], 0.0)
    nrow = jnp.where(t0 + nrows < seq_len, nb[0:1, :], 0.0)
    rid = _iota2(cur.shape, 0)
    prev = jnp.where(rid == 0, prow, pltpu.roll(cur, 1, 0))
    nxt = jnp.where(rid == nrows - 1, nrow, pltpu.roll(cur, nrows - 1, 0))
    return cur + mu_p * (prev - cur) + mu_n * (nxt - cur)


def _chunk_tri(n, reverse):
    t = _iota2((n, n), 0)
    s = _iota2((n, n), 1)
    same = (t >> 6) == (s >> 6)
    tri = (s >= t) if reverse else (s <= t)
    return jnp.where(same & tri, 1.0, 0.0).astype(BF16)


def _inproj_body(x_ref, g_ref, w_ref, rkv_ref, lora_ref, qk_ref, gv_ref, gg_ref, gkd_ref):
    h = _rms(x_ref[...], g_ref[...], NORM_EPS).astype(BF16)

    def proj(lo, hi):
        return jnp.dot(h, w_ref[:, lo:hi], preferred_element_type=F32)

    rkv_ref[...] = proj(C_RKV, C_LORA)
    lora_ref[...] = proj(C_LORA, C_QK)
    qk_ref[...] = proj(C_QK, C_GV)
    gv_ref[...] = proj(C_GV, C_GG)
    gg_ref[...] = proj(C_GG, C_GKD)
    gkd_ref[...] = proj(C_GKD, N_PROJ)


def _inproj(x2d, g, w_perm, tm):
    n = x2d.shape[0]
    widths = (C_LORA - C_RKV, C_QK - C_LORA, C_GV - C_QK, C_GG - C_GV, C_GKD - C_GG, N_PROJ - C_GKD)
    return pl.pallas_call(
        _inproj_body,
        grid=(n // tm,),
        in_specs=[pl.BlockSpec((tm, D_MODEL), lambda i: (i, 0)),
                  pl.BlockSpec((1, D_MODEL), lambda i: (0, 0)),
                  pl.BlockSpec((D_MODEL, N_PROJ), lambda i: (0, 0))],
        out_specs=[pl.BlockSpec((tm, w), lambda i: (i, 0)) for w in widths],
        out_shape=[jax.ShapeDtypeStruct((n, w), F32) for w in widths],
        compiler_params=_cparams(("parallel",)),
        name="inproj",
    )(x2d, g, w_perm)


(PV_MUP_R, PV_MUN_R, PV_MUP_K, PV_MUN_K, PV_MUP_V, PV_MUN_V, PV_W0_F, PV_W0_B, PV_A0_F, PV_A0_B,
 PV_KK, PV_KA, PV_RK, PV_LNW, PV_LNB) = range(15)
PV_ROWS = 16


def _rwkv_body(r_ref, k_ref, v_ref, lo_ref, pv_ref, mul_ref, ww_ref, wa_ref, wg_ref, out_ref,
               y_scr, bon_scr, s_scr, *, seq_len, blk):
    C = CHUNK
    nc = blk // C
    nblk = seq_len // blk
    pv = pv_ref[...]
    prow = lambda i: pv[i:i + 1, :]
    mul = mul_ref[...]

    lane = _iota2((1, LANES), 1)
    head_mask = [lane < RN, lane >= RN]
    li = _iota2((LANES, LANES), 0)
    lj = _iota2((LANES, LANES), 1)
    same_head = (li >> 6) == (lj >> 6)
    seg01 = jnp.where(same_head, 1.0, 0.0).astype(BF16)
    eye128 = li == lj
    ct = _iota2((C, C), 0)
    cs = _iota2((C, C), 1)
    eye64 = jnp.where(ct == cs, 1.0, 0.0)

    def seg_sum(x):
        return _mm_rhs01(x, seg01)

    def block_inputs(t0, reverse):
        d = 1 if reverse else 0
        r = _shifted_rows(r_ref, t0, blk, seq_len, prow(PV_MUP_R), prow(PV_MUN_R))
        k = _shifted_rows(k_ref, t0, blk, seq_len, prow(PV_MUP_K), prow(PV_MUN_K))
        v = _shifted_rows(v_ref, t0, blk, seq_len, prow(PV_MUP_V), prow(PV_MUN_V))
        lo = _shifted_rows(lo_ref, t0, blk, seq_len, mul[0:1, :], mul[1:2, :])
        zw = _mm(jnp.tanh(lo[:, 0:LANES]), ww_ref[0, d])
        za = _mm(lo[:, LANES:2 * LANES], wa_ref[0, d])
        lw = -_sigmoid(prow(PV_W0_F + d) + zw) * math.exp(-0.5)
        a = _sigmoid(prow(PV_A0_F + d) + za)
        kd = k * (1.0 + (a - 1.0) * prow(PV_KA))
        kk = k * prow(PV_KK)
        kk = kk * lax.rsqrt(seg_sum(kk * kk) + 1e-12)
        bonus = seg_sum(r * kd * prow(PV_RK)) * v
        g = _mm_lhs01(_chunk_tri(blk, reverse), lw)
        return dict(r=r, v=v, kd=kd, kk=kk, a=a, lw=lw, g=g, bonus=bonus, lo=lo)

    zeros_cl = jnp.zeros((C, LANES), F32)

    def block_terms(q, order, reverse):
        strict = (cs > ct) if reverse else (cs < ct)
        lane_c = _iota2((C, LANES), 1)
        row_c = _iota2((C, LANES), 0)
        col_c = lane_c & (C - 1)
        hi_half = lane_c >= C
        tri_s = (col_c > row_c) if reverse else (col_c < row_c)
        tri_i = (col_c >= row_c) if reverse else (col_c <= row_c)
        ch = {}
        for c in order:
            sl = slice(c * C, (c + 1) * C)
            g, lw, kk = q["g"][sl], q["lw"][sl], q["kk"][sl]
            r, v, kd, a = q["r"][sl], q["v"][sl], q["kd"][sl], q["a"][sl]
            g_tot = g[0:1, :] if reverse else g[C - 1:C, :]
            eng = jnp.exp(-g)
            beta = kk * a
            dec = jnp.exp(g_tot - g)
            ch[c] = dict(v=v, rt=r * jnp.exp(g), at=-kk * jnp.exp(g - lw), bh=beta * dec, kh=kd * dec,
                         gam=jnp.exp(g_tot), y2=jnp.concatenate([beta * eng, kd * eng], axis=0))
        pairs = [(c, h) for c in order for h in range(2)]
        gmat = {}
        for c, h in pairs:
            x2h = jnp.where(head_mask[h], jnp.concatenate([ch[c]["at"], ch[c]["rt"]], axis=0), 0.0)
            gmat[c, h] = _mm_nt(x2h, ch[c]["y2"])
        a_ab = {k: jnp.where(strict, gm[:C, :C], 0.0) for k, gm in gmat.items()}
        tm = {k: eye64 + jnp.where((ct >> 1) == (cs >> 1), a_ab[k], 0.0) for k in pairs}
        for lvl in range(1, 6):
            same = (ct >> (lvl + 1)) == (cs >> (lvl + 1))
            if reverse:
                off = (((ct >> lvl) & 1) == 0) & (((cs >> lvl) & 1) == 1)
            else:
                off = (((ct >> lvl) & 1) == 1) & (((cs >> lvl) & 1) == 0)
            left = {k: _mm(tm[k], jnp.where(same & off, a_ab[k], 0.0)) for k in pairs}
            tm = {k: tm[k] + _mm(left[k], tm[k]) for k in pairs}
        akv = {}
        for c, h in pairs:
            lhs = jnp.where(hi_half & tri_s, gmat[c, h][:C], 0.0)
            akv[c, h] = _mm(lhs, jnp.concatenate([zeros_cl, ch[c]["v"]], axis=0))
        wu = {(c, h): _mm(tm[c, h], jnp.concatenate([ch[c]["at"], akv[c, h]], axis=1)) for c, h in pairs}
        qy = {}
        for c, h in pairs:
            lhs = jnp.where(tri_i, gmat[c, h][C:], 0.0)
            rhs = jnp.concatenate([wu[c, h], jnp.concatenate([zeros_cl, ch[c]["v"]], axis=1)], axis=0)
            qy[c, h] = _mm(lhs, rhs)
        sel = lambda x0, x1: jnp.where(head_mask[0], x0, x1)
        terms = {}
        for c in order:
            w = sel(wu[c, 0][:, :LANES], wu[c, 1][:, :LANES])
            u0 = sel(wu[c, 0][:, LANES:], wu[c, 1][:, LANES:])
            qh = ch[c]["rt"] + sel(qy[c, 0][:, :LANES], qy[c, 1][:, :LANES])
            y0 = sel(qy[c, 0][:, LANES:], qy[c, 1][:, LANES:])
            p = (jnp.where(eye128, ch[c]["gam"], 0.0)
                 + jnp.where(same_head, _mm_tn(w, ch[c]["bh"]), 0.0))
            n0t = jnp.where(same_head,
                            _mm_tn(jnp.concatenate([u0, ch[c]["v"]], axis=0),
                                   jnp.concatenate([ch[c]["bh"], ch[c]["kh"]], axis=0)), 0.0)
            terms[c] = (qh, y0, p, n0t)
        return terms

    def run_block(t0, reverse):
        q = block_inputs(t0, reverse)
        order = list(range(nc - 1, -1, -1) if reverse else range(nc))
        terms = block_terms(q, order, reverse)
        s = s_scr[...]
        ys = [None] * nc
        for c in order:
            qh, y0, p, n0t = terms[c]
            ys[c] = _mm_nt(qh, s) + y0
            s = _mm_hp(s, p) + n0t
        s_scr[...] = s
        return q, jnp.concatenate(ys, axis=0)

    s_scr[...] = jnp.zeros(s_scr.shape, F32)

    def fwd_step(i, carry):
        t0 = pl.multiple_of(i * blk, blk)
        q, y = run_block(t0, False)
        y_scr[pl.ds(t0, blk), :] = y
        bon_scr[pl.ds(t0, blk), :] = q["bonus"]
        return carry

    lax.fori_loop(0, nblk, fwd_step, 0)

    s_scr[...] = jnp.zeros(s_scr.shape, F32)

    def bwd_step(i, carry):
        t0 = pl.multiple_of((nblk - 1 - i) * blk, blk)
        q, y = run_block(t0, True)
        y = y + y_scr[pl.ds(t0, blk), :]
        mean = seg_sum(y) * (1.0 / RN)
        dy = y - mean
        var = seg_sum(dy * dy) * (1.0 / RN)
        gn = dy * lax.rsqrt(var + GN_EPS) * prow(PV_LNW) + prow(PV_LNB)
        gate = _mm(_sigmoid(q["lo"][:, 2 * LANES:]), wg_ref[0])
        res = (gn + (bon_scr[pl.ds(t0, blk), :] + q["bonus"])) * gate
        out_ref[0, pl.ds(t0, blk), :] = res.astype(out_ref.dtype)
        return carry

    lax.fori_loop(0, nblk, bwd_step, 0)


def _rwkv(rkv, lora, pvec, mul, ww, wa, wg, blk):
    b, t, _ = rkv.shape
    nhp = RW // LANES
    col = lambda off: pl.BlockSpec((1, t, LANES), lambda i, j, off=off: (i, 0, off + j))
    body = functools.partial(_rwkv_body, seq_len=t, blk=blk)
    return pl.pallas_call(
        body,
        grid=(b, nhp),
        in_specs=[col(0), col(nhp), col(2 * nhp),
                  pl.BlockSpec((1, t, LORA_W), lambda i, j: (i, 0, 0)),
                  pl.BlockSpec((PV_ROWS, LANES), lambda i, j: (0, j)),
                  pl.BlockSpec((2, LORA_W), lambda i, j: (0, 0)),
                  pl.BlockSpec((1, 2, LANES, LANES), lambda i, j: (j, 0, 0, 0)),
                  pl.BlockSpec((1, 2, LANES, LANES), lambda i, j: (j, 0, 0, 0)),
                  pl.BlockSpec((1, 2 * LANES, LANES), lambda i, j: (j, 0, 0))],
        out_specs=pl.BlockSpec((1, t, LANES), lambda i, j: (i, 0, j)),
        out_shape=jax.ShapeDtypeStruct((b, t, RW), BF16),
        scratch_shapes=[pltpu.VMEM((t, LANES), F32), pltpu.VMEM((t, LANES), F32),
                        pltpu.VMEM((LANES, LANES), F32)],
        compiler_params=_cparams(("parallel", "parallel")),
        name="rwkv",
    )(rkv, rkv, rkv, lora, pvec, mul, ww, wa, wg)


def _gla_body(q_ref, k_ref, v_ref, gg_ref, gkd_ref, wgk_ref, gkb_ref, nw_ref, out_ref,
              o_scr, s_scr, *, seq_len, blk):
    C = CHUNK
    nc = blk // C
    nblk = seq_len // blk
    lane = _iota2((1, LANES), 1)
    head_mask = [lane < GK, lane >= GK]
    si = _iota2((2 * GV, LANES), 0)
    sj = _iota2((2 * GV, LANES), 1)
    same_head = (si >> 7) == (sj >> 6)
    ct = _iota2((C, C), 0)
    cs = _iota2((C, C), 1)

    def run_block(t0, reverse):
        d = 1 if reverse else 0
        rows = pl.ds(t0, blk)
        q = q_ref[0, rows, :] * (GK ** -0.5)
        k = k_ref[0, rows, :]
        v = v_ref[0, rows, :]
        z = _mm(gkd_ref[0, rows, :], wgk_ref[0, d]) + gkb_ref[0, d]
        lg = (jnp.minimum(z, 0.0) - jnp.log1p(jnp.exp(-jnp.abs(z)))) * (1.0 / GATE_NORM)
        bcum = _mm_lhs01(_chunk_tri(blk, reverse), lg)
        incl = (cs >= ct) if reverse else (cs <= ct)
        s = s_scr[...]
        outs = [None] * nc
        for c in (range(nc - 1, -1, -1) if reverse else range(nc)):
            sl = slice(c * C, (c + 1) * C)
            bc = bcum[sl]
            b_tot = bc[0:1, :] if reverse else bc[C - 1:C, :]
            q_in = q[sl] * jnp.exp(bc)
            k_in = k[sl] * jnp.exp(-bc)
            k_dec = k[sl] * jnp.exp(b_tot - bc)
            vc = v[sl]
            o = _mm_nt(q_in, s)
            intra = []
            for h in range(2):
                att = jnp.where(incl, _mm_nt(jnp.where(head_mask[h], q_in, 0.0), k_in), 0.0)
                intra.append(_mm(att, vc[:, h * GV:(h + 1) * GV]))
            outs[c] = o + jnp.concatenate(intra, axis=1)
            s = s * jnp.exp(b_tot) + jnp.where(same_head, _mm_tn(vc, k_dec), 0.0)
        s_scr[...] = s
        return jnp.concatenate(outs, axis=0)

    s_scr[...] = jnp.zeros(s_scr.shape, F32)

    def fwd_step(i, carry):
        t0 = pl.multiple_of(i * blk, blk)
        o_scr[pl.ds(t0, blk), :] = run_block(t0, False)
        return carry

    lax.fori_loop(0, nblk, fwd_step, 0)

    s_scr[...] = jnp.zeros(s_scr.shape, F32)

    def bwd_step(i, carry):
        t0 = pl.multiple_of((nblk - 1 - i) * blk, blk)
        o = run_block(t0, True) + o_scr[pl.ds(t0, blk), :]
        gg = gg_ref[0, pl.ds(t0, blk), :]
        res = []
        for h in range(2):
            oh = o[:, h * GV:(h + 1) * GV]
            oh = oh * lax.rsqrt(jnp.mean(oh * oh, axis=-1, keepdims=True) + GLA_EPS) * nw_ref[...]
            gh = gg[:, h * GV:(h + 1) * GV]
            res.append(oh * (gh * _sigmoid(gh)))
        out_ref[0, pl.ds(t0, blk), :] = jnp.concatenate(res, axis=1).astype(out_ref.dtype)
        return carry

    lax.fori_loop(0, nblk, bwd_step, 0)


def _gla(qk, gv, gg, gkd, wgk, gkb, nw, blk):
    b, t, _ = qk.shape
    npair = GH // 2
    body = functools.partial(_gla_body, seq_len=t, blk=blk)
    return pl.pallas_call(
        body,
        grid=(b, npair),
        in_specs=[pl.BlockSpec((1, t, LANES), lambda i, j: (i, 0, j)),
                  pl.BlockSpec((1, t, LANES), lambda i, j: (i, 0, npair + j)),
                  pl.BlockSpec((1, t, 2 * GV), lambda i, j: (i, 0, j)),
                  pl.BlockSpec((1, t, 2 * GV), lambda i, j: (i, 0, j)),
                  pl.BlockSpec((1, t, GKD_W), lambda i, j: (i, 0, 0)),
                  pl.BlockSpec((1, 2, GKD_W, LANES), lambda i, j: (j, 0, 0, 0)),
                  pl.BlockSpec((1, 2, 1, LANES), lambda i, j: (j, 0, 0, 0)),
                  pl.BlockSpec((1, GV), lambda i, j: (0, 0))],
        out_specs=pl.BlockSpec((1, t, 2 * GV), lambda i, j: (i, 0, j)),
        out_shape=jax.ShapeDtypeStruct((b, t, GH * GV), BF16),
        scratch_shapes=[pltpu.VMEM((t, 2 * GV), F32), pltpu.VMEM((2 * GV, LANES), F32)],
        compiler_params=_cparams(("parallel", "parallel")),
        name="gla",
    )(qk, qk, gv, gg, gkd, wgk, gkb, nw)


def _outproj_body(x_ref, rw_ref, gl_ref, w_ref, g_ref, o_ref):
    m = (jnp.dot(rw_ref[...], w_ref[0:RW, :], preferred_element_type=F32)
         + jnp.dot(gl_ref[...], w_ref[RW:, :], preferred_element_type=F32))
    o_ref[...] = x_ref[...] + _rms(m, g_ref[...], NORM_EPS)


def _outproj(x2d, rw2d, gl2d, w_out, g_post, tm):
    n = x2d.shape[0]
    return pl.pallas_call(
        _outproj_body,
        grid=(n // tm,),
        in_specs=[pl.BlockSpec((tm, D_MODEL), lambda i: (i, 0)),
                  pl.BlockSpec((tm, RW), lambda i: (i, 0)),
                  pl.BlockSpec((tm, GH * GV), lambda i: (i, 0)),
                  pl.BlockSpec((D_MODEL, D_MODEL), lambda i: (0, 0)),
                  pl.BlockSpec((1, D_MODEL), lambda i: (0, 0))],
        out_specs=pl.BlockSpec((tm, D_MODEL), lambda i: (i, 0)),
        out_shape=jax.ShapeDtypeStruct((n, D_MODEL), F32),
        compiler_params=_cparams(("parallel",)),
        name="outproj",
    )(x2d, rw2d, gl2d, w_out, g_post)


def _kvproj_body(m_ref, g_ref, w_ref, kv_ref):
    h = _rms(m_ref[...], g_ref[...], NORM_EPS).astype(BF16)
    kv_ref[...] = jnp.dot(h, w_ref[...], preferred_element_type=F32).astype(kv_ref.dtype)


def _kvproj(mem2d, g_mem, wkv, tm):
    n = mem2d.shape[0]
    return pl.pallas_call(
        _kvproj_body,
        grid=(n // tm,),
        in_specs=[pl.BlockSpec((tm, D_MODEL), lambda i: (i, 0)),
                  pl.BlockSpec((1, D_MODEL), lambda i: (0, 0)),
                  pl.BlockSpec((D_MODEL, 2 * D_MODEL), lambda i: (0, 0))],
        out_specs=pl.BlockSpec((tm, 2 * D_MODEL), lambda i: (i, 0)),
        out_shape=jax.ShapeDtypeStruct((n, 2 * D_MODEL), BF16),
        compiler_params=_cparams(("parallel",)),
        name="kvproj",
    )(mem2d, g_mem, wkv)


def _xattn_body(x_ref, kv_ref, gpre_ref, wq_ref, wo_ref, gpost_ref, o_ref):
    x = x_ref[0]
    h = _rms(x, gpre_ref[...], NORM_EPS).astype(BF16)
    q = jnp.dot(h, wq_ref[...], preferred_element_type=F32).astype(BF16)
    acc = jnp.zeros(x.shape, F32)
    for hd in range(X_HEADS):
        cols = slice(hd * X_HD, (hd + 1) * X_HD)
        kh = kv_ref[0, :, hd * X_HD:(hd + 1) * X_HD]
        vh = kv_ref[0, :, D_MODEL + hd * X_HD:D_MODEL + (hd + 1) * X_HD]
        s = lax.dot_general(q[:, cols], kh, (((1,), (1,)), ((), ())),
                            preferred_element_type=F32) * (X_HD ** -0.5)
        e = jnp.exp(s - jnp.max(s, axis=-1, keepdims=True))
        p = e / jnp.sum(e, axis=-1, keepdims=True)
        oh = jnp.dot(p.astype(BF16), vh, preferred_element_type=F32)
        acc = acc + jnp.dot(oh.astype(BF16), wo_ref[cols, :], preferred_element_type=F32)
    o_ref[0] = x + _rms(acc, gpost_ref[...], NORM_EPS)


def _xattn(x3d, kv3d, g_pre, wq, wo, g_post, tm):
    b, t, _ = x3d.shape
    return pl.pallas_call(
        _xattn_body,
        grid=(b, t // tm),
        in_specs=[pl.BlockSpec((1, tm, D_MODEL), lambda i, j: (i, j, 0)),
                  pl.BlockSpec((1, N_MEM, 2 * D_MODEL), lambda i, j: (i, 0, 0)),
                  pl.BlockSpec((1, D_MODEL), lambda i, j: (0, 0)),
                  pl.BlockSpec((D_MODEL, D_MODEL), lambda i, j: (0, 0)),
                  pl.BlockSpec((D_MODEL, D_MODEL), lambda i, j: (0, 0)),
                  pl.BlockSpec((1, D_MODEL), lambda i, j: (0, 0))],
        out_specs=pl.BlockSpec((1, tm, D_MODEL), lambda i, j: (i, j, 0)),
        out_shape=jax.ShapeDtypeStruct((b, t, D_MODEL), F32),
        compiler_params=_cparams(("parallel", "parallel")),
        name="xattn",
    )(x3d, kv3d, g_pre, wq, wo, g_post)


def _ffn_body(x_ref, gpre_ref, w1_ref, w2_ref, gpost_ref, o_ref, h_scr, acc_scr):
    j = pl.program_id(1)

    @pl.when(j == 0)
    def _():
        h_scr[...] = _rms(x_ref[...], gpre_ref[...], NORM_EPS).astype(BF16)
        acc_scr[...] = jnp.zeros_like(acc_scr)

    a = jnp.dot(h_scr[...], w1_ref[...], preferred_element_type=F32)
    a = jnp.square(jnp.maximum(a, 0.0)).astype(BF16)
    acc_scr[...] += jnp.dot(a, w2_ref[...], preferred_element_type=F32)

    @pl.when(j == pl.num_programs(1) - 1)
    def _():
        o_ref[...] = x_ref[...] + _rms(acc_scr[...], gpost_ref[...], NORM_EPS)


def _ffn(x2d, g_pre, w1, w2, g_post, tm, tf):
    n = x2d.shape[0]
    return pl.pallas_call(
        _ffn_body,
        grid=(n // tm, D_FF // tf),
        in_specs=[pl.BlockSpec((tm, D_MODEL), lambda i, j: (i, 0)),
                  pl.BlockSpec((1, D_MODEL), lambda i, j: (0, 0)),
                  pl.BlockSpec((D_MODEL, tf), lambda i, j: (0, j)),
                  pl.BlockSpec((tf, D_MODEL), lambda i, j: (j, 0)),
                  pl.BlockSpec((1, D_MODEL), lambda i, j: (0, 0))],
        out_specs=pl.BlockSpec((tm, D_MODEL), lambda i, j: (i, 0)),
        out_shape=jax.ShapeDtypeStruct((n, D_MODEL), F32),
        scratch_shapes=[pltpu.VMEM((tm, D_MODEL), BF16), pltpu.VMEM((tm, D_MODEL), F32)],
        compiler_params=_cparams(("parallel", "arbitrary")),
        name="ffn",
    )(x2d, g_pre, w1, w2, g_post)


def _pack_params(p):
    w_in = p["w_in"]
    g0 = R_COLS
    zeros = lambda n: jnp.zeros((D_MODEL, n), w_in.dtype)
    w_perm = jnp.concatenate([
        w_in[:, :R_COLS], zeros(LORA_W - (R_COLS - 3 * RW)),
        w_in[:, g0:g0 + 2 * GH * GK],
        w_in[:, g0 + 2 * GH * GK:g0 + 2 * GH * GK + GH * GV],
        w_in[:, g0 + 2 * GH * GK + GH * GV + GK_LORA:],
        w_in[:, g0 + 2 * GH * GK + GH * GV:g0 + 2 * GH * GK + GH * GV + GK_LORA], zeros(GKD_W - GK_LORA),
    ], axis=1).astype(BF16)
    row = lambda v: v.reshape(1, -1)
    mu_p, mu_n = p["mu_prev"], p["mu_next"]
    pvec = jnp.concatenate([
        row(mu_p[0:RW]), row(mu_n[0:RW]), row(mu_p[RW:2 * RW]), row(mu_n[RW:2 * RW]),
        row(mu_p[2 * RW:3 * RW]), row(mu_n[2 * RW:3 * RW]),
        row(p["w0_f"]), row(p["w0_b"]), row(p["a0_f"]), row(p["a0_b"]),
        row(p["k_k"]), row(p["k_a"]), row(p["r_k"]), row(p["lnx_w"]), row(p["lnx_b"]),
        jnp.zeros((1, RW), F32)], axis=0)
    pad_l = lambda v: jnp.pad(v[3 * RW:], (0, LORA_W - (R_COLS - 3 * RW)))
    mul = jnp.stack([pad_l(mu_p), pad_l(mu_n)], axis=0)
    nhp = RW // LANES

    def lora_pair(wf, wb):
        wf = wf.reshape(-1, nhp, LANES).transpose(1, 0, 2)
        wb = wb.reshape(-1, nhp, LANES).transpose(1, 0, 2)
        z = jnp.zeros_like(wf)
        return jnp.stack([jnp.concatenate([wf, z], axis=1), jnp.concatenate([z, wb], axis=1)],
                         axis=1).astype(BF16)

    ww = lora_pair(p["w2_f"], p["w2_b"])
    wa = lora_pair(p["a2_f"], p["a2_b"])
    wg = jnp.pad(p["g2"], ((0, 2 * LANES - GATE_LORA), (0, 0)))
    wg = wg.reshape(2 * LANES, nhp, LANES).transpose(1, 0, 2).astype(BF16)
    npair = GH // 2

    def gk_pair(w):
        w = jnp.pad(w, ((0, GKD_W - GK_LORA), (0, 0)))
        return w.reshape(GKD_W, npair, LANES).transpose(1, 0, 2)

    wgk = jnp.stack([gk_pair(p["gk2_f"]), gk_pair(p["gk2_b"])], axis=1).astype(BF16)
    gkb = jnp.stack([p["gkb_f"].reshape(npair, 1, LANES), p["gkb_b"].reshape(npair, 1, LANES)], axis=1)
    return dict(
        w_perm=w_perm, pvec=pvec, mul=mul, ww=ww, wa=wa, wg=wg, wgk=wgk, gkb=gkb,
        nw=row(p["gla_norm_w"]),
        g_mix_pre=row(p["g_mix_pre"]), g_mix_post=row(p["g_mix_post"]),
        g_x_pre=row(p["g_x_pre"]), g_x_post=row(p["g_x_post"]), g_mem=row(p["g_mem"]),
        g_ffn_pre=row(p["g_ffn_pre"]), g_ffn_post=row(p["g_ffn_post"]),
        w_out=p["w_out"].astype(BF16), wq=p["wq_x"].astype(BF16), wkv=p["wkv_x"].astype(BF16),
        wo=p["wo_x"].astype(BF16), w1=p["w_ff1"].astype(BF16), w2=p["w_ff2"].astype(BF16))


def _pick(n, pref):
    t = pref
    while n % t:
        t //= 2
    return t


def _trunk(x, mem, pk):
    b, t, _ = x.shape
    n = b * t
    tm = _pick(n, 512)
    x2d = x.reshape(n, D_MODEL)
    rkv, lora, qk, gv, gg, gkd = _inproj(x2d, pk["g_mix_pre"], pk["w_perm"], tm)
    r3 = lambda a: a.reshape(b, t, a.shape[-1])
    blk = _pick(t, 512)
    rw = _rwkv(r3(rkv), r3(lora), pk["pvec"], pk["mul"], pk["ww"], pk["wa"], pk["wg"], blk)
    gl = _gla(r3(qk), r3(gv), r3(gg), r3(gkd), pk["wgk"], pk["gkb"], pk["nw"], blk)
    x1 = _outproj(x2d, rw.reshape(n, RW), gl.reshape(n, GH * GV), pk["w_out"], pk["g_mix_post"], tm)
    nm = mem.shape[0] * mem.shape[1]
    kv = _kvproj(mem.reshape(nm, D_MODEL), pk["g_mem"], pk["wkv"], _pick(nm, 512))
    x2 = _xattn(x1.reshape(b, t, D_MODEL), kv.reshape(mem.shape[0], mem.shape[1], 2 * D_MODEL),
                pk["g_x_pre"], pk["wq"], pk["wo"], pk["g_x_post"], _pick(t, 512))
    y = _ffn(x2.reshape(n, D_MODEL), pk["g_ffn_pre"], pk["w1"], pk["w2"], pk["g_ffn_post"],
             tm, 1024)
    return y.reshape(b, t, D_MODEL)


def kernel(x_prompt, x_sample, mem_prompt, mem_sample, g_mix_pre, w_in, mu_prev, mu_next, w0_f, w2_f, w0_b, w2_b, a0_f, a2_f, a0_b, a2_b, g2, k_k, k_a, r_k, lnx_w, lnx_b, gk2_f, gkb_f, gk2_b, gkb_b, gla_norm_w, w_out, g_mix_post, g_x_pre, g_mem, wq_x, wkv_x, wo_x, g_x_post, g_ffn_pre, w_ff1, w_ff2, g_ffn_post):
    params = dict(
        g_mix_pre=g_mix_pre, w_in=w_in, mu_prev=mu_prev, mu_next=mu_next, w0_f=w0_f, w2_f=w2_f,
        w0_b=w0_b, w2_b=w2_b, a0_f=a0_f, a2_f=a2_f, a0_b=a0_b, a2_b=a2_b, g2=g2, k_k=k_k, k_a=k_a,
        r_k=r_k, lnx_w=lnx_w, lnx_b=lnx_b, gk2_f=gk2_f, gkb_f=gkb_f, gk2_b=gk2_b, gkb_b=gkb_b,
        gla_norm_w=gla_norm_w, w_out=w_out, g_mix_post=g_mix_post, g_x_pre=g_x_pre, g_mem=g_mem,
        wq_x=wq_x, wkv_x=wkv_x, wo_x=wo_x, g_x_post=g_x_post, g_ffn_pre=g_ffn_pre, w_ff1=w_ff1,
        w_ff2=w_ff2, g_ffn_post=g_ffn_post)
    assert w_in.shape[0] == 1, "single-layer stack expected"
    pk = _pack_params({name: w[0] for name, w in params.items()})
    return (_trunk(x_prompt, mem_prompt, pk), _trunk(x_sample, mem_sample, pk))
```

```python
import functools
import math

import jax
import jax.numpy as jnp
from jax import lax
from jax.experimental import pallas as pl
from jax.experimental.pallas import tpu as pltpu

F32 = jnp.float32
BF16 = jnp.bfloat16

D_MODEL = 1024
RW = 512
RN = 64
DECAY_LORA = 64
AAA_LORA = 64
GATE_LORA = 160
GN_EPS = 64e-5
GH = 4
GV = 128
GK = 64
GK_LORA = 16
GATE_NORM = 16.0
CHUNK = 64
GLA_EPS = 1e-5
R_COLS = 3 * RW + 2 * DECAY_LORA + 2 * AAA_LORA + GATE_LORA
N_MEM = 256
X_HEADS = 4
X_HD = D_MODEL // X_HEADS
D_FF = 4 * D_MODEL
NORM_EPS = 1e-6

LANES = 128
CUM_ROWS = 2 * CHUNK
LORA_W = 512
GKD_W = 128
C_RKV, C_LORA, C_QK, C_GV, C_GG, C_GKD = 0, 1536, 2048, 2560, 3072, 3584
N_PROJ = 3712
VMEM_LIMIT = 56 * 1024 * 1024


def _cparams(sem):
    return pltpu.CompilerParams(dimension_semantics=sem, vmem_limit_bytes=VMEM_LIMIT)


def _mm(a, b):
    return jnp.dot(a.astype(BF16), b.astype(BF16), preferred_element_type=F32)


def _mm_nt(a, b):
    return lax.dot_general(a.astype(BF16), b.astype(BF16), (((1,), (1,)), ((), ())),
                           preferred_element_type=F32)


def _mm_tn(a, b):
    return lax.dot_general(a.astype(BF16), b.astype(BF16), (((0,), (0,)), ((), ())),
                           preferred_element_type=F32)


def _split2(x):
    hi = x.astype(BF16)
    lo = (x - hi.astype(F32)).astype(BF16)
    return hi, lo


def _mm_hp(a, b):
    ah, al = _split2(a)
    bh, bl = _split2(b)
    d = lambda x, y: jnp.dot(x, y, preferred_element_type=F32)
    return d(ah, bh) + (d(ah, bl) + d(al, bh))


def _mm_lhs01(m01, x, terms):
    acc = None
    rem = x
    for i in range(terms):
        piece = rem.astype(BF16)
        part = jnp.dot(m01, piece, preferred_element_type=F32)
        acc = part if acc is None else acc + part
        if i + 1 < terms:
            rem = rem - piece.astype(F32)
    return acc


def _rms(x, g, eps):
    return x * lax.rsqrt(jnp.mean(x * x, axis=-1, keepdims=True) + eps) * g


def _sigmoid(x):
    return 1.0 / (1.0 + jnp.exp(-x))


def _iota2(shape, dim):
    return lax.broadcasted_iota(jnp.int32, shape, dim)


def _shifted_rows(ref, t0, nrows, seq_len, mu_p, mu_n):
    cur = ref[0, pl.ds(t0, nrows), :]
    pb = ref[0, pl.ds(pl.multiple_of(jnp.maximum(t0 - 8, 0), 8), 8), :]
    nb = ref[0, pl.ds(pl.multiple_of(jnp.minimum(t0 + nrows, seq_len - 8), 8), 8), :]
    prow = jnp.where(t0 > 0, pb[7:8, :], 0.0)
    nrow = jnp.where(t0 + nrows < seq_len, nb[0:1, :], 0.0)
    rid = _iota2(cur.shape, 0)
    prev = jnp.where(rid == 0, prow, pltpu.roll(cur, 1, 0))
    nxt = jnp.where(rid == nrows - 1, nrow, pltpu.roll(cur, nrows - 1, 0))
    return cur + mu_p * (prev - cur) + mu_n * (nxt - cur)


def _chunk_tri(n, reverse):
    t = _iota2((n, n), 0)
    s = _iota2((n, n), 1)
    same = (t >> 6) == (s >> 6)
    tri = (s >= t) if reverse else (s <= t)
    return jnp.where(same & tri, 1.0, 0.0).astype(BF16)


def _chunk_cumsum(x, reverse, terms):
    tri = _chunk_tri(CUM_ROWS, reverse)
    parts = [_mm_lhs01(tri, x[i:i + CUM_ROWS], terms) for i in range(0, x.shape[0], CUM_ROWS)]
    return parts[0] if len(parts) == 1 else jnp.concatenate(parts, axis=0)


def _inproj_body(x_ref, g_ref, w_ref, rkv_ref, lora_ref, qk_ref, gv_ref, gg_ref, gkd_ref):
    h = _rms(x_ref[...], g_ref[...], NORM_EPS).astype(BF16)

    def proj(lo, hi):
        return jnp.dot(h, w_ref[:, lo:hi], preferred_element_type=F32)

    rkv_ref[...] = proj(C_RKV, C_LORA)
    lora_ref[...] = proj(C_LORA, C_QK)
    qk_ref[...] = proj(C_QK, C_GV)
    gv_ref[...] = proj(C_GV, C_GG)
    gg_ref[...] = proj(C_GG, C_GKD)
    gkd_ref[...] = proj(C_GKD, N_PROJ)


def _inproj(x2d, g, w_perm, tm):
    n = x2d.shape[0]
    widths = (C_LORA - C_RKV, C_QK - C_LORA, C_GV - C_QK, C_GG - C_GV, C_GKD - C_GG, N_PROJ - C_GKD)
    return pl.pallas_call(
        _inproj_body,
        grid=(n // tm,),
        in_specs=[pl.BlockSpec((tm, D_MODEL), lambda i: (i, 0)),
                  pl.BlockSpec((1, D_MODEL), lambda i: (0, 0)),
                  pl.BlockSpec((D_MODEL, N_PROJ), lambda i: (0, 0))],
        out_specs=[pl.BlockSpec((tm, w), lambda i: (i, 0)) for w in widths],
        out_shape=[jax.ShapeDtypeStruct((n, w), F32) for w in widths],
        compiler_params=_cparams(("parallel",)),
        name="inproj",
    )(x2d, g, w_perm)


(PV_MUP_R, PV_MUN_R, PV_MUP_K, PV_MUN_K, PV_MUP_V, PV_MUN_V, PV_W0_F, PV_W0_B, PV_A0_F, PV_A0_B,
 PV_KK, PV_KA, PV_RK, PV_LNW, PV_LNB) = range(15)
PV_ROWS = 16


def _rwkv_body(r_ref, k_ref, v_ref, lo_ref, pv_ref, mul_ref, ww_ref, wa_ref, wg_ref, out_ref,
               y_scr, bon_scr, s_scr, *, seq_len, blk):
    C = CHUNK
    nc = blk // C
    nblk = seq_len // blk
    half = nblk // 2
    pv = pv_ref[...]
    prow = lambda i: pv[i:i + 1, :]
    mul = mul_ref[...]

    lane = _iota2((1, LANES), 1)
    head_mask = [lane < RN, lane >= RN]
    li = _iota2((LANES, LANES), 0)
    lj = _iota2((LANES, LANES), 1)
    same_head = (li >> 6) == (lj >> 6)
    seg01 = jnp.where(same_head, 1.0, 0.0).astype(BF16)
    eye128 = li == lj
    ct = _iota2((C, C), 0)
    cs = _iota2((C, C), 1)
    eye64 = jnp.where(ct == cs, 1.0, 0.0)
    lane_c = _iota2((C, LANES), 1)
    row_c = _iota2((C, LANES), 0)
    col_c = lane_c & (C - 1)
    hi_half = lane_c >= C
    zeros_cl = jnp.zeros((C, LANES), F32)

    def seg_sum(x):
        return _mm(x, seg01)

    def block_inputs(t0, reverse):
        d = 1 if reverse else 0
        r = _shifted_rows(r_ref, t0, blk, seq_len, prow(PV_MUP_R), prow(PV_MUN_R))
        k = _shifted_rows(k_ref, t0, blk, seq_len, prow(PV_MUP_K), prow(PV_MUN_K))
        v = _shifted_rows(v_ref, t0, blk, seq_len, prow(PV_MUP_V), prow(PV_MUN_V))
        lo = _shifted_rows(lo_ref, t0, blk, seq_len, mul[0:1, :], mul[1:2, :])
        zw = _mm(jnp.tanh(lo[:, 0:LANES]), ww_ref[0, d])
        za = _mm(lo[:, LANES:2 * LANES], wa_ref[0, d])
        lw = -_sigmoid(prow(PV_W0_F + d) + zw) * math.exp(-0.5)
        a = _sigmoid(prow(PV_A0_F + d) + za)
        kd = k * (1.0 + (a - 1.0) * prow(PV_KA))
        kk = k * prow(PV_KK)
        kk = kk * lax.rsqrt(seg_sum(kk * kk) + 1e-12)
        bonus = seg_sum(r * kd * prow(PV_RK)) * v
        g = _chunk_cumsum(lw, reverse, 2)
        return dict(r=r, v=v, kd=kd, kk=kk, a=a, lw=lw, g=g, bonus=bonus, lo=lo)

    def block_terms(ctxs):
        strict, tri_s, tri_i, ch, chains = {}, {}, {}, {}, []
        for d, q, order, reverse in ctxs:
            strict[d] = (cs > ct) if reverse else (cs < ct)
            tri_s[d] = (col_c > row_c) if reverse else (col_c < row_c)
            tri_i[d] = (col_c >= row_c) if reverse else (col_c <= row_c)
            for c in order:
                sl = slice(c * C, (c + 1) * C)
                g, lw, kk = q["g"][sl], q["lw"][sl], q["kk"][sl]
                r, v, kd, a = q["r"][sl], q["v"][sl], q["kd"][sl], q["a"][sl]
                g_tot = g[0:1, :] if reverse else g[C - 1:C, :]
                eng = jnp.exp(-g)
                beta = kk * a
                dec = jnp.exp(g_tot - g)
                ch[d, c] = dict(v=v, rt=r * jnp.exp(g), at=-kk * jnp.exp(g - lw), bh=beta * dec,
                                kh=kd * dec, gam=jnp.exp(g_tot),
                                y2=jnp.concatenate([beta * eng, kd * eng], axis=0))
                chains += [(d, c, 0), (d, c, 1)]
        gmat = {}
        for d, c, h in chains:
            x2h = jnp.where(head_mask[h], jnp.concatenate([ch[d, c]["at"], ch[d, c]["rt"]], axis=0), 0.0)
            gmat[d, c, h] = _mm_nt(x2h, ch[d, c]["y2"])
        a_ab = {k: jnp.where(strict[k[0]], gmat[k][:C, :C], 0.0) for k in chains}
        tm = {k: eye64 + jnp.where((ct >> 1) == (cs >> 1), a_ab[k], 0.0) for k in chains}
        for lvl in range(1, 6):
            same = (ct >> (lvl + 1)) == (cs >> (lvl + 1))
            lower = (((ct >> lvl) & 1) == 1) & (((cs >> lvl) & 1) == 0)
            upper = (((ct >> lvl) & 1) == 0) & (((cs >> lvl) & 1) == 1)
            off = {d: same & (upper if reverse else lower) for d, _, _, reverse in ctxs}
            left = {k: _mm(tm[k], jnp.where(off[k[0]], a_ab[k], 0.0)) for k in chains}
            tm = {k: tm[k] + _mm(left[k], tm[k]) for k in chains}
        akv = {}
        for d, c, h in chains:
            lhs = jnp.where(hi_half & tri_s[d], gmat[d, c, h][:C], 0.0)
            akv[d, c, h] = _mm(lhs, jnp.concatenate([zeros_cl, ch[d, c]["v"]], axis=0))
        wu = {(d, c, h): _mm(tm[d, c, h], jnp.concatenate([ch[d, c]["at"], akv[d, c, h]], axis=1))
              for d, c, h in chains}
        qy = {}
        for d, c, h in chains:
            lhs = jnp.where(tri_i[d], gmat[d, c, h][C:], 0.0)
            rhs = jnp.concatenate([wu[d, c, h], jnp.concatenate([zeros_cl, ch[d, c]["v"]], axis=1)], axis=0)
            qy[d, c, h] = _mm(lhs, rhs)
        sel = lambda x0, x1: jnp.where(head_mask[0], x0, x1)
        terms = {}
        for d, c in ch:
            e = ch[d, c]
            w = sel(wu[d, c, 0][:, :LANES], wu[d, c, 1][:, :LANES])
            u0 = sel(wu[d, c, 0][:, LANES:], wu[d, c, 1][:, LANES:])
            qh = e["rt"] + sel(qy[d, c, 0][:, :LANES], qy[d, c, 1][:, :LANES])
            y0 = sel(qy[d, c, 0][:, LANES:], qy[d, c, 1][:, LANES:])
            p = jnp.where(eye128, e["gam"], 0.0) + jnp.where(same_head, _mm_tn(w, e["bh"]), 0.0)
            n0t = jnp.where(same_head,
                            _mm_tn(jnp.concatenate([u0, e["v"]], axis=0),
                                   jnp.concatenate([e["bh"], e["kh"]], axis=0)), 0.0)
            terms[d, c] = (qh, y0, p, n0t)
        return terms

    def run_pair(tf, tb):
        qf = block_inputs(tf, False)
        qb = block_inputs(tb, True)
        ctxs = [(0, qf, list(range(nc)), False), (1, qb, list(range(nc - 1, -1, -1)), True)]
        terms = block_terms(ctxs)
        s = [s_scr[0], s_scr[1]]
        ys = [[None] * nc, [None] * nc]
        for step in range(nc):
            for d, _, order, _ in ctxs:
                qh, y0, p, n0t = terms[d, order[step]]
                ys[d][order[step]] = _mm_nt(qh, s[d]) + y0
                s[d] = _mm_hp(s[d], p) + n0t
        s_scr[0] = s[0]
        s_scr[1] = s[1]
        return qf, jnp.concatenate(ys[0], axis=0), qb, jnp.concatenate(ys[1], axis=0)

    def finish(q, y, bonus, t0):
        mean = seg_sum(y) * (1.0 / RN)
        dy = y - mean
        var = seg_sum(dy * dy) * (1.0 / RN)
        gn = dy * lax.rsqrt(var + GN_EPS) * prow(PV_LNW) + prow(PV_LNB)
        gate = _mm(_sigmoid(q["lo"][:, 2 * LANES:]), wg_ref[0])
        out_ref[0, pl.ds(t0, blk), :] = ((gn + bonus) * gate).astype(out_ref.dtype)

    s_scr[...] = jnp.zeros(s_scr.shape, F32)

    def block_starts(i):
        return pl.multiple_of(i * blk, blk), pl.multiple_of((nblk - 1 - i) * blk, blk)

    def first_half(i, carry):
        tf, tb = block_starts(i)
        qf, yf, qb, yb = run_pair(tf, tb)
        y_scr[pl.ds(tf, blk), :] = yf
        bon_scr[pl.ds(tf, blk), :] = qf["bonus"]
        y_scr[pl.ds(tb, blk), :] = yb
        bon_scr[pl.ds(tb, blk), :] = qb["bonus"]
        return carry

    def second_half(i, carry):
        tf, tb = block_starts(i)
        qf, yf, qb, yb = run_pair(tf, tb)
        finish(qf, yf + y_scr[pl.ds(tf, blk), :], qf["bonus"] + bon_scr[pl.ds(tf, blk), :], tf)
        finish(qb, yb + y_scr[pl.ds(tb, blk), :], qb["bonus"] + bon_scr[pl.ds(tb, blk), :], tb)
        return carry

    lax.fori_loop(0, half, first_half, 0)
    lax.fori_loop(half, nblk, second_half, 0)


def _rwkv(rkv, lora, pvec, mul, ww, wa, wg, blk):
    b, t, _ = rkv.shape
    assert t % (2 * blk) == 0 and blk % CUM_ROWS == 0
    nhp = RW // LANES
    col = lambda off: pl.BlockSpec((1, t, LANES), lambda i, j, off=off: (i, 0, off + j))
    body = functools.partial(_rwkv_body, seq_len=t, blk=blk)
    return pl.pallas_call(
        body,
        grid=(b, nhp),
        in_specs=[col(0), col(nhp), col(2 * nhp),
                  pl.BlockSpec((1, t, LORA_W), lambda i, j: (i, 0, 0)),
                  pl.BlockSpec((PV_ROWS, LANES), lambda i, j: (0, j)),
                  pl.BlockSpec((2, LORA_W), lambda i, j: (0, 0)),
                  pl.BlockSpec((1, 2, LANES, LANES), lambda i, j: (j, 0, 0, 0)),
                  pl.BlockSpec((1, 2, LANES, LANES), lambda i, j: (j, 0, 0, 0)),
                  pl.BlockSpec((1, 2 * LANES, LANES), lambda i, j: (j, 0, 0))],
        out_specs=pl.BlockSpec((1, t, LANES), lambda i, j: (i, 0, j)),
        out_shape=jax.ShapeDtypeStruct((b, t, RW), BF16),
        scratch_shapes=[pltpu.VMEM((t, LANES), F32), pltpu.VMEM((t, LANES), F32),
                        pltpu.VMEM((2, LANES, LANES), F32)],
        compiler_params=_cparams(("parallel", "parallel")),
        name="rwkv",
    )(rkv, rkv, rkv, lora, pvec, mul, ww, wa, wg)


def _gla_body(q_ref, k_ref, v_ref, gg_ref, gkd_ref, wgk_ref, gkb_ref, nw_ref, out_ref,
              o_scr, s_scr, *, seq_len, blk):
    C = CHUNK
    nc = blk // C
    nblk = seq_len // blk
    lane = _iota2((1, LANES), 1)
    head_mask = [lane < GK, lane >= GK]
    si = _iota2((2 * GV, LANES), 0)
    sj = _iota2((2 * GV, LANES), 1)
    same_head = (si >> 7) == (sj >> 6)
    ct = _iota2((C, C), 0)
    cs = _iota2((C, C), 1)

    def run_block(t0, reverse):
        d = 1 if reverse else 0
        rows = pl.ds(t0, blk)
        q = q_ref[0, rows, :] * (GK ** -0.5)
        k = k_ref[0, rows, :]
        v = v_ref[0, rows, :]
        z = _mm(gkd_ref[0, rows, :], wgk_ref[0, d]) + gkb_ref[0, d]
        lg = (jnp.minimum(z, 0.0) - jnp.log1p(jnp.exp(-jnp.abs(z)))) * (1.0 / GATE_NORM)
        bcum = _chunk_cumsum(lg, reverse, 3)
        incl = (cs >= ct) if reverse else (cs <= ct)
        order = list(range(nc - 1, -1, -1) if reverse else range(nc))
        ch = {}
        for c in order:
            sl = slice(c * C, (c + 1) * C)
            bc = bcum[sl]
            b_tot = bc[0:1, :] if reverse else bc[C - 1:C, :]
            ch[c] = dict(q_in=q[sl] * jnp.exp(bc), k_in=k[sl] * jnp.exp(-bc),
                         k_dec=k[sl] * jnp.exp(b_tot - bc), v=v[sl], gam=jnp.exp(b_tot))
        att = {(c, h): jnp.where(incl, _mm_nt(jnp.where(head_mask[h], ch[c]["q_in"], 0.0), ch[c]["k_in"]), 0.0)
               for c in order for h in range(2)}
        intra = {(c, h): _mm(att[c, h], ch[c]["v"][:, h * GV:(h + 1) * GV]) for c in order for h in range(2)}
        ds = {c: jnp.where(same_head, _mm_tn(ch[c]["v"], ch[c]["k_dec"]), 0.0) for c in order}
        s = s_scr[...]
        s_in = {}
        for c in order:
            s_in[c] = s
            s = s * ch[c]["gam"] + ds[c]
        s_scr[...] = s
        outs = [_mm_nt(ch[c]["q_in"], s_in[c]) + jnp.concatenate([intra[c, 0], intra[c, 1]], axis=1)
                for c in range(nc)]
        return jnp.concatenate(outs, axis=0)

    s_scr[...] = jnp.zeros(s_scr.shape, F32)

    def fwd_step(i, carry):
        t0 = pl.multiple_of(i * blk, blk)
        o_scr[pl.ds(t0, blk), :] = run_block(t0, False)
        return carry

    lax.fori_loop(0, nblk, fwd_step, 0)

    s_scr[...] = jnp.zeros(s_scr.shape, F32)

    def bwd_step(i, carry):
        t0 = pl.multiple_of((nblk - 1 - i) * blk, blk)
        o = run_block(t0, True) + o_scr[pl.ds(t0, blk), :]
        gg = gg_ref[0, pl.ds(t0, blk), :]
        res = []
        for h in range(2):
            oh = o[:, h * GV:(h + 1) * GV]
            oh = oh * lax.rsqrt(jnp.mean(oh * oh, axis=-1, keepdims=True) + GLA_EPS) * nw_ref[...]
            gh = gg[:, h * GV:(h + 1) * GV]
            res.append(oh * (gh * _sigmoid(gh)))
        out_ref[0, pl.ds(t0, blk), :] = jnp.concatenate(res, axis=1).astype(out_ref.dtype)
        return carry

    lax.fori_loop(0, nblk, bwd_step, 0)


def _gla(qk, gv, gg, gkd, wgk, gkb, nw, blk):
    b, t, _ = qk.shape
    assert t % blk == 0 and blk % CUM_ROWS == 0
    npair = GH // 2
    body = functools.partial(_gla_body, seq_len=t, blk=blk)
    return pl.pallas_call(
        body,
        grid=(b, npair),
        in_specs=[pl.BlockSpec((1, t, LANES), lambda i, j: (i, 0, j)),
                  pl.BlockSpec((1, t, LANES), lambda i, j: (i, 0, npair + j)),
                  pl.BlockSpec((1, t, 2 * GV), lambda i, j: (i, 0, j)),
                  pl.BlockSpec((1, t, 2 * GV), lambda i, j: (i, 0, j)),
                  pl.BlockSpec((1, t, GKD_W), lambda i, j: (i, 0, 0)),
                  pl.BlockSpec((1, 2, GKD_W, LANES), lambda i, j: (j, 0, 0, 0)),
                  pl.BlockSpec((1, 2, 1, LANES), lambda i, j: (j, 0, 0, 0)),
                  pl.BlockSpec((1, GV), lambda i, j: (0, 0))],
        out_specs=pl.BlockSpec((1, t, 2 * GV), lambda i, j: (i, 0, j)),
        out_shape=jax.ShapeDtypeStruct((b, t, GH * GV), BF16),
        scratch_shapes=[pltpu.VMEM((t, 2 * GV), F32), pltpu.VMEM((2 * GV, LANES), F32)],
        compiler_params=_cparams(("parallel", "parallel")),
        name="gla",
    )(qk, qk, gv, gg, gkd, wgk, gkb, nw)


def _outproj_body(x_ref, rw_ref, gl_ref, w_ref, g_ref, o_ref):
    m = (jnp.dot(rw_ref[...], w_ref[0:RW, :], preferred_element_type=F32)
         + jnp.dot(gl_ref[...], w_ref[RW:, :], preferred_element_type=F32))
    o_ref[...] = x_ref[...] + _rms(m, g_ref[...], NORM_EPS)


def _outproj(x2d, rw2d, gl2d, w_out, g_post, tm):
    n = x2d.shape[0]
    return pl.pallas_call(
        _outproj_body,
        grid=(n // tm,),
        in_specs=[pl.BlockSpec((tm, D_MODEL), lambda i: (i, 0)),
                  pl.BlockSpec((tm, RW), lambda i: (i, 0)),
                  pl.BlockSpec((tm, GH * GV), lambda i: (i, 0)),
                  pl.BlockSpec((D_MODEL, D_MODEL), lambda i: (0, 0)),
                  pl.BlockSpec((1, D_MODEL), lambda i: (0, 0))],
        out_specs=pl.BlockSpec((tm, D_MODEL), lambda i: (i, 0)),
        out_shape=jax.ShapeDtypeStruct((n, D_MODEL), F32),
        compiler_params=_cparams(("parallel",)),
        name="outproj",
    )(x2d, rw2d, gl2d, w_out, g_post)


def _kvproj_body(m_ref, g_ref, w_ref, kv_ref):
    h = _rms(m_ref[...], g_ref[...], NORM_EPS).astype(BF16)
    kv_ref[...] = jnp.dot(h, w_ref[...], preferred_element_type=F32).astype(kv_ref.dtype)


def _kvproj(mem2d, g_mem, wkv, tm):
    n = mem2d.shape[0]
    return pl.pallas_call(
        _kvproj_body,
        grid=(n // tm,),
        in_specs=[pl.BlockSpec((tm, D_MODEL), lambda i: (i, 0)),
                  pl.BlockSpec((1, D_MODEL), lambda i: (0, 0)),
                  pl.BlockSpec((D_MODEL, 2 * D_MODEL), lambda i: (0, 0))],
        out_specs=pl.BlockSpec((tm, 2 * D_MODEL), lambda i: (i, 0)),
        out_shape=jax.ShapeDtypeStruct((n, 2 * D_MODEL), BF16),
        compiler_params=_cparams(("parallel",)),
        name="kvproj",
    )(mem2d, g_mem, wkv)


def _xattn_body(x_ref, kv_ref, gpre_ref, wq_ref, wo_ref, gpost_ref, o_ref):
    x = x_ref[0]
    h = _rms(x, gpre_ref[...], NORM_EPS).astype(BF16)
    q = jnp.dot(h, wq_ref[...], preferred_element_type=F32).astype(BF16)
    acc = jnp.zeros(x.shape, F32)
    for hd in range(X_HEADS):
        cols = slice(hd * X_HD, (hd + 1) * X_HD)
        kh = kv_ref[0, :, hd * X_HD:(hd + 1) * X_HD]
        vh = kv_ref[0, :, D_MODEL + hd * X_HD:D_MODEL + (hd + 1) * X_HD]
        s = lax.dot_general(q[:, cols], kh, (((1,), (1,)), ((), ())),
                            preferred_element_type=F32) * (X_HD ** -0.5)
        e = jnp.exp(s - jnp.max(s, axis=-1, keepdims=True))
        p = e / jnp.sum(e, axis=-1, keepdims=True)
        oh = jnp.dot(p.astype(BF16), vh, preferred_element_type=F32)
        acc = acc + jnp.dot(oh.astype(BF16), wo_ref[cols, :], preferred_element_type=F32)
    o_ref[0] = x + _rms(acc, gpost_ref[...], NORM_EPS)


def _xattn(x3d, kv3d, g_pre, wq, wo, g_post, tm):
    b, t, _ = x3d.shape
    return pl.pallas_call(
        _xattn_body,
        grid=(b, t // tm),
        in_specs=[pl.BlockSpec((1, tm, D_MODEL), lambda i, j: (i, j, 0)),
                  pl.BlockSpec((1, N_MEM, 2 * D_MODEL), lambda i, j: (i, 0, 0)),
                  pl.BlockSpec((1, D_MODEL), lambda i, j: (0, 0)),
                  pl.BlockSpec((D_MODEL, D_MODEL), lambda i, j: (0, 0)),
                  pl.BlockSpec((D_MODEL, D_MODEL), lambda i, j: (0, 0)),
                  pl.BlockSpec((1, D_MODEL), lambda i, j: (0, 0))],
        out_specs=pl.BlockSpec((1, tm, D_MODEL), lambda i, j: (i, j, 0)),
        out_shape=jax.ShapeDtypeStruct((b, t, D_MODEL), F32),
        compiler_params=_cparams(("parallel", "parallel")),
        name="xattn",
    )(x3d, kv3d, g_pre, wq, wo, g_post)


def _ffn_body(x_ref, gpre_ref, w1_ref, w2_ref, gpost_ref, o_ref, h_scr, acc_scr):
    j = pl.program_id(1)

    @pl.when(j == 0)
    def _():
        h_scr[...] = _rms(x_ref[...], gpre_ref[...], NORM_EPS).astype(BF16)
        acc_scr[...] = jnp.zeros_like(acc_scr)

    a = jnp.dot(h_scr[...], w1_ref[...], preferred_element_type=F32)
    a = jnp.square(jnp.maximum(a, 0.0)).astype(BF16)
    acc_scr[...] += jnp.dot(a, w2_ref[...], preferred_element_type=F32)

    @pl.when(j == pl.num_programs(1) - 1)
    def _():
        o_ref[...] = x_ref[...] + _rms(acc_scr[...], gpost_ref[...], NORM_EPS)


def _ffn(x2d, g_pre, w1, w2, g_post, tm, tf):
    n = x2d.shape[0]
    return pl.pallas_call(
        _ffn_body,
        grid=(n // tm, D_FF // tf),
        in_specs=[pl.BlockSpec((tm, D_MODEL), lambda i, j: (i, 0)),
                  pl.BlockSpec((1, D_MODEL), lambda i, j: (0, 0)),
                  pl.BlockSpec((D_MODEL, tf), lambda i, j: (0, j)),
                  pl.BlockSpec((tf, D_MODEL), lambda i, j: (j, 0)),
                  pl.BlockSpec((1, D_MODEL), lambda i, j: (0, 0))],
        out_specs=pl.BlockSpec((tm, D_MODEL), lambda i, j: (i, 0)),
        out_shape=jax.ShapeDtypeStruct((n, D_MODEL), F32),
        scratch_shapes=[pltpu.VMEM((tm, D_MODEL), BF16), pltpu.VMEM((tm, D_MODEL), F32)],
        compiler_params=_cparams(("parallel", "arbitrary")),
        name="ffn",
    )(x2d, g_pre, w1, w2, g_post)


def _pack_params(p):
    w_in = p["w_in"]
    g0 = R_COLS
    zeros = lambda n: jnp.zeros((D_MODEL, n), w_in.dtype)
    w_perm = jnp.concatenate([
        w_in[:, :R_COLS], zeros(LORA_W - (R_COLS - 3 * RW)),
        w_in[:, g0:g0 + 2 * GH * GK],
        w_in[:, g0 + 2 * GH * GK:g0 + 2 * GH * GK + GH * GV],
        w_in[:, g0 + 2 * GH * GK + GH * GV + GK_LORA:],
        w_in[:, g0 + 2 * GH * GK + GH * GV:g0 + 2 * GH * GK + GH * GV + GK_LORA], zeros(GKD_W - GK_LORA),
    ], axis=1).astype(BF16)
    row = lambda v: v.reshape(1, -1)
    mu_p, mu_n = p["mu_prev"], p["mu_next"]
    pvec = jnp.concatenate([
        row(mu_p[0:RW]), row(mu_n[0:RW]), row(mu_p[RW:2 * RW]), row(mu_n[RW:2 * RW]),
        row(mu_p[2 * RW:3 * RW]), row(mu_n[2 * RW:3 * RW]),
        row(p["w0_f"]), row(p["w0_b"]), row(p["a0_f"]), row(p["a0_b"]),
        row(p["k_k"]), row(p["k_a"]), row(p["r_k"]), row(p["lnx_w"]), row(p["lnx_b"]),
        jnp.zeros((1, RW), F32)], axis=0)
    pad_l = lambda v: jnp.pad(v[3 * RW:], (0, LORA_W - (R_COLS - 3 * RW)))
    mul = jnp.stack([pad_l(mu_p), pad_l(mu_n)], axis=0)
    nhp = RW // LANES

    def lora_pair(wf, wb):
        wf = wf.reshape(-1, nhp, LANES).transpose(1, 0, 2)
        wb = wb.reshape(-1, nhp, LANES).transpose(1, 0, 2)
        z = jnp.zeros_like(wf)
        return jnp.stack([jnp.concatenate([wf, z], axis=1), jnp.concatenate([z, wb], axis=1)],
                         axis=1).astype(BF16)

    ww = lora_pair(p["w2_f"], p["w2_b"])
    wa = lora_pair(p["a2_f"], p["a2_b"])
    wg = jnp.pad(p["g2"], ((0, 2 * LANES - GATE_LORA), (0, 0)))
    wg = wg.reshape(2 * LANES, nhp, LANES).transpose(1, 0, 2).astype(BF16)
    npair = GH // 2

    def gk_pair(w):
        w = jnp.pad(w, ((0, GKD_W - GK_LORA), (0, 0)))
        return w.reshape(GKD_W, npair, LANES).transpose(1, 0, 2)

    wgk = jnp.stack([gk_pair(p["gk2_f"]), gk_pair(p["gk2_b"])], axis=1).astype(BF16)
    gkb = jnp.stack([p["gkb_f"].reshape(npair, 1, LANES), p["gkb_b"].reshape(npair, 1, LANES)], axis=1)
    return dict(
        w_perm=w_perm, pvec=pvec, mul=mul, ww=ww, wa=wa, wg=wg, wgk=wgk, gkb=gkb,
        nw=row(p["gla_norm_w"]),
        g_mix_pre=row(p["g_mix_pre"]), g_mix_post=row(p["g_mix_post"]),
        g_x_pre=row(p["g_x_pre"]), g_x_post=row(p["g_x_post"]), g_mem=row(p["g_mem"]),
        g_ffn_pre=row(p["g_ffn_pre"]), g_ffn_post=row(p["g_ffn_post"]),
        w_out=p["w_out"].astype(BF16), wq=p["wq_x"].astype(BF16), wkv=p["wkv_x"].astype(BF16),
        wo=p["wo_x"].astype(BF16), w1=p["w_ff1"].astype(BF16), w2=p["w_ff2"].astype(BF16))


def _pick(n, pref):
    t = pref
    while n % t:
        t //= 2
    return t


def _trunk(x, mem, pk):
    b, t, _ = x.shape
    n = b * t
    tm = _pick(n, 512)
    x2d = x.reshape(n, D_MODEL)
    rkv, lora, qk, gv, gg, gkd = _inproj(x2d, pk["g_mix_pre"], pk["w_perm"], tm)
    r3 = lambda a: a.reshape(b, t, a.shape[-1])
    rw = _rwkv(r3(rkv), r3(lora), pk["pvec"], pk["mul"], pk["ww"], pk["wa"], pk["wg"], _pick(t // 2, 512))
    gl = _gla(r3(qk), r3(gv), r3(gg), r3(gkd), pk["wgk"], pk["gkb"], pk["nw"], _pick(t, 512))
    x1 = _outproj(x2d, rw.reshape(n, RW), gl.reshape(n, GH * GV), pk["w_out"], pk["g_mix_post"], tm)
    nm = mem.shape[0] * mem.shape[1]
    kv = _kvproj(mem.reshape(nm, D_MODEL), pk["g_mem"], pk["wkv"], _pick(nm, 512))
    x2 = _xattn(x1.reshape(b, t, D_MODEL), kv.reshape(mem.shape[0], mem.shape[1], 2 * D_MODEL),
                pk["g_x_pre"], pk["wq"], pk["wo"], pk["g_x_post"], _pick(t, 512))
    y = _ffn(x2.reshape(n, D_MODEL), pk["g_ffn_pre"], pk["w1"], pk["w2"], pk["g_ffn_post"],
             tm, 1024)
    return y.reshape(b, t, D_MODEL)


def kernel(x_prompt, x_sample, mem_prompt, mem_sample, g_mix_pre, w_in, mu_prev, mu_next, w0_f, w2_f, w0_b, w2_b, a0_f, a2_f, a0_b, a2_b, g2, k_k, k_a, r_k, lnx_w, lnx_b, gk2_f, gkb_f, gk2_b, gkb_b, gla_norm_w, w_out, g_mix_post, g_x_pre, g_mem, wq_x, wkv_x, wo_x, g_x_post, g_ffn_pre, w_ff1, w_ff2, g_ffn_post):
    params = dict(
        g_mix_pre=g_mix_pre, w_in=w_in, mu_prev=mu_prev, mu_next=mu_next, w0_f=w0_f, w2_f=w2_f,
        w0_b=w0_b, w2_b=w2_b, a0_f=a0_f, a2_f=a2_f, a0_b=a0_b, a2_b=a2_b, g2=g2, k_k=k_k, k_a=k_a,
        r_k=r_k, lnx_w=lnx_w, lnx_b=lnx_b, gk2_f=gk2_f, gkb_f=gkb_f, gk2_b=gk2_b, gkb_b=gkb_b,
        gla_norm_w=gla_norm_w, w_out=w_out, g_mix_post=g_mix_post, g_x_pre=g_x_pre, g_mem=g_mem,
        wq_x=wq_x, wkv_x=wkv_x, wo_x=wo_x, g_x_post=g_x_post, g_ffn_pre=g_ffn_pre, w_ff1=w_ff1,
        w_ff2=w_ff2, g_ffn_post=g_ffn_post)
    assert w_in.shape[0] == 1, "single-layer stack expected"
    pk = _pack_params({name: w[0] for name, w in params.items()})
    return (_trunk(x_prompt, mem_prompt, pk), _trunk(x_sample, mem_sample, pk))
```

```python
import functools
import math

import jax
import jax.numpy as jnp
from jax import lax
from jax.experimental import pallas as pl
from jax.experimental.pallas import tpu as pltpu

F32 = jnp.float32
BF16 = jnp.bfloat16

D_MODEL = 1024
RW = 512
RN = 64
DECAY_LORA = 64
AAA_LORA = 64
GATE_LORA = 160
GN_EPS = 64e-5
GH = 4
GV = 128
GK = 64
GK_LORA = 16
GATE_NORM = 16.0
CHUNK = 64
GLA_EPS = 1e-5
R_COLS = 3 * RW + 2 * DECAY_LORA + 2 * AAA_LORA + GATE_LORA
N_MEM = 256
X_HEADS = 4
X_HD = D_MODEL // X_HEADS
D_FF = 4 * D_MODEL
NORM_EPS = 1e-6

LANES = 128
CUM_ROWS = 2 * CHUNK
HALO = 16
SHIFT_COLS = 512
LORA_W = 512
GKD_W = 128
C_RKV, C_LORA, C_QK, C_GV, C_GG, C_GKD = 0, 1536, 2048, 2560, 3072, 3584
N_PROJ = 3712
VMEM_LIMIT = 56 * 1024 * 1024


def _cparams(sem):
    return pltpu.CompilerParams(dimension_semantics=sem, vmem_limit_bytes=VMEM_LIMIT)


def _mm(a, b):
    return jnp.dot(a.astype(BF16), b.astype(BF16), preferred_element_type=F32)


def _mm_nt(a, b):
    return lax.dot_general(a.astype(BF16), b.astype(BF16), (((1,), (1,)), ((), ())),
                           preferred_element_type=F32)


def _mm_tn(a, b):
    return lax.dot_general(a.astype(BF16), b.astype(BF16), (((0,), (0,)), ((), ())),
                           preferred_element_type=F32)


def _split2(x):
    hi = x.astype(BF16)
    lo = (x - hi.astype(F32)).astype(BF16)
    return hi, lo


def _mm_hp(a, b):
    ah, al = _split2(a)
    bh, bl = _split2(b)
    d = lambda x, y: jnp.dot(x, y, preferred_element_type=F32)
    return d(ah, bh) + (d(ah, bl) + d(al, bh))


def _mm_lhs01(m01, x, terms):
    acc = None
    rem = x
    for i in range(terms):
        piece = rem.astype(BF16)
        part = jnp.dot(m01, piece, preferred_element_type=F32)
        acc = part if acc is None else acc + part
        if i + 1 < terms:
            rem = rem - piece.astype(F32)
    return acc


def _rms(x, g, eps):
    return x * lax.rsqrt(jnp.mean(x * x, axis=-1, keepdims=True) + eps) * g


def _sigmoid(x):
    return 1.0 / (1.0 + jnp.exp(-x))


def _iota2(shape, dim):
    return lax.broadcasted_iota(jnp.int32, shape, dim)


def _chunk_tri(n, reverse):
    t = _iota2((n, n), 0)
    s = _iota2((n, n), 1)
    same = (t >> 6) == (s >> 6)
    tri = (s >= t) if reverse else (s <= t)
    return jnp.where(same & tri, 1.0, 0.0).astype(BF16)


def _chunk_cumsum(x, reverse, terms):
    tri = _chunk_tri(CUM_ROWS, reverse)
    parts = [_mm_lhs01(tri, x[i:i + CUM_ROWS], terms) for i in range(0, x.shape[0], CUM_ROWS)]
    return parts[0] if len(parts) == 1 else jnp.concatenate(parts, axis=0)


def _inproj_body(x_ref, xp_ref, xn_ref, g_ref, w_ref, mu_ref, rkv_ref, lora_ref, qk_ref, gv_ref, gg_ref,
                 gkd_ref, *, tiles_per_seq):
    tm = x_ref.shape[0]
    i = pl.program_id(0)
    g = g_ref[...]
    h = _rms(x_ref[...], g, NORM_EPS).astype(BF16)
    h_prev = _rms(xp_ref[...], g, NORM_EPS).astype(BF16)
    h_next = _rms(xn_ref[...], g, NORM_EPS).astype(BF16)
    h_ext = jnp.concatenate([h_prev, h, h_next], axis=0)
    pos = i % tiles_per_seq
    has_prev = pos > 0
    has_next = pos < tiles_per_seq - 1
    rid = _iota2((tm, SHIFT_COLS), 0)

    def proj(lo, hi):
        return jnp.dot(h, w_ref[:, lo:hi], preferred_element_type=F32)

    for lo in range(C_RKV, C_QK, SHIFT_COLS):
        p_ext = jnp.dot(h_ext, w_ref[:, lo:lo + SHIFT_COLS], preferred_element_type=F32)
        p = p_ext[HALO:HALO + tm, :]
        p_before = p_ext[HALO - 1:HALO, :]
        p_after = p_ext[HALO + tm:HALO + tm + 1, :]
        prev = jnp.where(rid == 0, jnp.where(has_prev, p_before, 0.0), pltpu.roll(p, 1, 0))
        nxt = jnp.where(rid == tm - 1, jnp.where(has_next, p_after, 0.0), pltpu.roll(p, tm - 1, 0))
        mu = mu_ref[:, lo:lo + SHIFT_COLS]
        shifted = p + mu[0:1, :] * (prev - p) + mu[1:2, :] * (nxt - p)
        if lo < C_LORA:
            rkv_ref[:, lo:lo + SHIFT_COLS] = shifted
        else:
            lora_ref[:, lo - C_LORA:lo - C_LORA + SHIFT_COLS] = shifted
    qk_ref[...] = proj(C_QK, C_GV)
    gv_ref[...] = proj(C_GV, C_GG)
    gg_ref[...] = proj(C_GG, C_GKD)
    gkd_ref[...] = proj(C_GKD, N_PROJ)


def _inproj(x2d, g, w_perm, mu, tm, seq_len):
    n = x2d.shape[0]
    assert seq_len % tm == 0 and tm % HALO == 0
    widths = (C_LORA - C_RKV, C_QK - C_LORA, C_GV - C_QK, C_GG - C_GV, C_GKD - C_GG, N_PROJ - C_GKD)
    hb = tm // HALO
    last = n // HALO - 1
    body = functools.partial(_inproj_body, tiles_per_seq=seq_len // tm)
    return pl.pallas_call(
        body,
        grid=(n // tm,),
        in_specs=[pl.BlockSpec((tm, D_MODEL), lambda i: (i, 0)),
                  pl.BlockSpec((HALO, D_MODEL), lambda i: (jnp.maximum(i * hb - 1, 0), 0)),
                  pl.BlockSpec((HALO, D_MODEL), lambda i: (jnp.minimum((i + 1) * hb, last), 0)),
                  pl.BlockSpec((1, D_MODEL), lambda i: (0, 0)),
                  pl.BlockSpec((D_MODEL, N_PROJ), lambda i: (0, 0)),
                  pl.BlockSpec((2, C_QK), lambda i: (0, 0))],
        out_specs=[pl.BlockSpec((tm, w), lambda i: (i, 0)) for w in widths],
        out_shape=[jax.ShapeDtypeStruct((n, w), F32) for w in widths],
        compiler_params=_cparams(("parallel",)),
        name="inproj",
    )(x2d, x2d, x2d, g, w_perm, mu)


PV_W0_F, PV_W0_B, PV_A0_F, PV_A0_B, PV_KK, PV_KA, PV_RK, PV_LNW, PV_LNB = range(9)
PV_ROWS = 16


def _rwkv_body(r_ref, k_ref, v_ref, lo_ref, pv_ref, ww_ref, wa_ref, wg_ref, out_ref,
               y_scr, bon_scr, s_scr, *, seq_len, blk):
    C = CHUNK
    nc = blk // C
    nblk = seq_len // blk
    half = nblk // 2
    pv = pv_ref[...]
    prow = lambda i: pv[i:i + 1, :]

    lane = _iota2((1, LANES), 1)
    head_mask = [lane < RN, lane >= RN]
    li = _iota2((LANES, LANES), 0)
    lj = _iota2((LANES, LANES), 1)
    same_head = (li >> 6) == (lj >> 6)
    seg01 = jnp.where(same_head, 1.0, 0.0).astype(BF16)
    eye128 = li == lj
    ct = _iota2((C, C), 0)
    cs = _iota2((C, C), 1)
    eye64 = jnp.where(ct == cs, 1.0, 0.0)
    lane_c = _iota2((C, LANES), 1)
    row_c = _iota2((C, LANES), 0)
    col_c = lane_c & (C - 1)
    hi_half = lane_c >= C
    zeros_cl = jnp.zeros((C, LANES), F32)

    def seg_sum(x):
        return _mm(x, seg01)

    def block_inputs(t0, reverse):
        d = 1 if reverse else 0
        rows = pl.ds(t0, blk)
        r, k, v, lo = r_ref[0, rows, :], k_ref[0, rows, :], v_ref[0, rows, :], lo_ref[0, rows, :]
        zw = _mm(jnp.tanh(lo[:, 0:LANES]), ww_ref[0, d])
        za = _mm(lo[:, LANES:2 * LANES], wa_ref[0, d])
        lw = -_sigmoid(prow(PV_W0_F + d) + zw) * math.exp(-0.5)
        a = _sigmoid(prow(PV_A0_F + d) + za)
        kd = k * (1.0 + (a - 1.0) * prow(PV_KA))
        kk = k * prow(PV_KK)
        kk = kk * lax.rsqrt(seg_sum(kk * kk) + 1e-12)
        bonus = seg_sum(r * kd * prow(PV_RK)) * v
        g = _chunk_cumsum(lw, reverse, 2)
        return dict(r=r, v=v, kd=kd, kk=kk, a=a, lw=lw, g=g, bonus=bonus, lo=lo)

    def block_terms(ctxs):
        strict, tri_s, tri_i, ch, chains = {}, {}, {}, {}, []
        for d, q, order, reverse in ctxs:
            strict[d] = (cs > ct) if reverse else (cs < ct)
            tri_s[d] = (col_c > row_c) if reverse else (col_c < row_c)
            tri_i[d] = (col_c >= row_c) if reverse else (col_c <= row_c)
            for c in order:
                sl = slice(c * C, (c + 1) * C)
                g, lw, kk = q["g"][sl], q["lw"][sl], q["kk"][sl]
                r, v, kd, a = q["r"][sl], q["v"][sl], q["kd"][sl], q["a"][sl]
                g_tot = g[0:1, :] if reverse else g[C - 1:C, :]
                eng = jnp.exp(-g)
                beta = kk * a
                dec = jnp.exp(g_tot - g)
                ch[d, c] = dict(v=v, rt=r * jnp.exp(g), at=-kk * jnp.exp(g - lw), bh=beta * dec,
                                kh=kd * dec, gam=jnp.exp(g_tot),
                                y2=jnp.concatenate([beta * eng, kd * eng], axis=0))
                chains += [(d, c, 0), (d, c, 1)]
        gmat = {}
        for d, c, h in chains:
            x2h = jnp.where(head_mask[h], jnp.concatenate([ch[d, c]["at"], ch[d, c]["rt"]], axis=0), 0.0)
            gmat[d, c, h] = _mm_nt(x2h, ch[d, c]["y2"])
        a_ab = {k: jnp.where(strict[k[0]], gmat[k][:C, :C], 0.0) for k in chains}
        tm = {k: eye64 + jnp.where((ct >> 1) == (cs >> 1), a_ab[k], 0.0) for k in chains}
        for lvl in range(1, 6):
            same = (ct >> (lvl + 1)) == (cs >> (lvl + 1))
            lower = (((ct >> lvl) & 1) == 1) & (((cs >> lvl) & 1) == 0)
            upper = (((ct >> lvl) & 1) == 0) & (((cs >> lvl) & 1) == 1)
            off = {d: same & (upper if reverse else lower) for d, _, _, reverse in ctxs}
            left = {k: _mm(tm[k], jnp.where(off[k[0]], a_ab[k], 0.0)) for k in chains}
            tm = {k: tm[k] + _mm(left[k], tm[k]) for k in chains}
        akv = {}
        for d, c, h in chains:
            lhs = jnp.where(hi_half & tri_s[d], gmat[d, c, h][:C], 0.0)
            akv[d, c, h] = _mm(lhs, jnp.concatenate([zeros_cl, ch[d, c]["v"]], axis=0))
        wu = {(d, c, h): _mm(tm[d, c, h], jnp.concatenate([ch[d, c]["at"], akv[d, c, h]], axis=1))
              for d, c, h in chains}
        qy = {}
        for d, c, h in chains:
            lhs = jnp.where(tri_i[d], gmat[d, c, h][C:], 0.0)
            rhs = jnp.concatenate([wu[d, c, h], jnp.concatenate([zeros_cl, ch[d, c]["v"]], axis=1)], axis=0)
            qy[d, c, h] = _mm(lhs, rhs)
        sel = lambda x0, x1: jnp.where(head_mask[0], x0, x1)
        terms = {}
        for d, c in ch:
            e = ch[d, c]
            w = sel(wu[d, c, 0][:, :LANES], wu[d, c, 1][:, :LANES])
            u0 = sel(wu[d, c, 0][:, LANES:], wu[d, c, 1][:, LANES:])
            qh = e["rt"] + sel(qy[d, c, 0][:, :LANES], qy[d, c, 1][:, :LANES])
            y0 = sel(qy[d, c, 0][:, LANES:], qy[d, c, 1][:, LANES:])
            p = jnp.where(eye128, e["gam"], 0.0) + jnp.where(same_head, _mm_tn(w, e["bh"]), 0.0)
            n0t = jnp.where(same_head,
                            _mm_tn(jnp.concatenate([u0, e["v"]], axis=0),
                                   jnp.concatenate([e["bh"], e["kh"]], axis=0)), 0.0)
            terms[d, c] = (qh, y0, p, n0t)
        return terms

    def run_pair(tf, tb):
        qf = block_inputs(tf, False)
        qb = block_inputs(tb, True)
        ctxs = [(0, qf, list(range(nc)), False), (1, qb, list(range(nc - 1, -1, -1)), True)]
        terms = block_terms(ctxs)
        s = [s_scr[0], s_scr[1]]
        ys = [[None] * nc, [None] * nc]
        for step in range(nc):
            for d, _, order, _ in ctxs:
                qh, y0, p, n0t = terms[d, order[step]]
                ys[d][order[step]] = _mm_nt(qh, s[d]) + y0
                s[d] = _mm_hp(s[d], p) + n0t
        s_scr[0] = s[0]
        s_scr[1] = s[1]
        return qf, jnp.concatenate(ys[0], axis=0), qb, jnp.concatenate(ys[1], axis=0)

    def finish(q, y, bonus, t0):
        mean = seg_sum(y) * (1.0 / RN)
        dy = y - mean
        var = seg_sum(dy * dy) * (1.0 / RN)
        gn = dy * lax.rsqrt(var + GN_EPS) * prow(PV_LNW) + prow(PV_LNB)
        gate = _mm(_sigmoid(q["lo"][:, 2 * LANES:]), wg_ref[0])
        out_ref[0, pl.ds(t0, blk), :] = ((gn + bonus) * gate).astype(out_ref.dtype)

    s_scr[...] = jnp.zeros(s_scr.shape, F32)

    def block_starts(i):
        return pl.multiple_of(i * blk, blk), pl.multiple_of((nblk - 1 - i) * blk, blk)

    def first_half(i, carry):
        tf, tb = block_starts(i)
        qf, yf, qb, yb = run_pair(tf, tb)
        y_scr[pl.ds(tf, blk), :] = yf
        bon_scr[pl.ds(tf, blk), :] = qf["bonus"]
        y_scr[pl.ds(tb, blk), :] = yb
        bon_scr[pl.ds(tb, blk), :] = qb["bonus"]
        return carry

    def second_half(i, carry):
        tf, tb = block_starts(i)
        qf, yf, qb, yb = run_pair(tf, tb)
        finish(qf, yf + y_scr[pl.ds(tf, blk), :], qf["bonus"] + bon_scr[pl.ds(tf, blk), :], tf)
        finish(qb, yb + y_scr[pl.ds(tb, blk), :], qb["bonus"] + bon_scr[pl.ds(tb, blk), :], tb)
        return carry

    lax.fori_loop(0, half, first_half, 0)
    lax.fori_loop(half, nblk, second_half, 0)


def _rwkv(rkv, lora, pvec, ww, wa, wg, blk):
    b, t, _ = rkv.shape
    assert t % (2 * blk) == 0 and blk % CUM_ROWS == 0
    nhp = RW // LANES
    col = lambda off: pl.BlockSpec((1, t, LANES), lambda i, j, off=off: (i, 0, off + j))
    body = functools.partial(_rwkv_body, seq_len=t, blk=blk)
    return pl.pallas_call(
        body,
        grid=(b, nhp),
        in_specs=[col(0), col(nhp), col(2 * nhp),
                  pl.BlockSpec((1, t, LORA_W), lambda i, j: (i, 0, 0)),
                  pl.BlockSpec((PV_ROWS, LANES), lambda i, j: (0, j)),
                  pl.BlockSpec((1, 2, LANES, LANES), lambda i, j: (j, 0, 0, 0)),
                  pl.BlockSpec((1, 2, LANES, LANES), lambda i, j: (j, 0, 0, 0)),
                  pl.BlockSpec((1, 2 * LANES, LANES), lambda i, j: (j, 0, 0))],
        out_specs=pl.BlockSpec((1, t, LANES), lambda i, j: (i, 0, j)),
        out_shape=jax.ShapeDtypeStruct((b, t, RW), BF16),
        scratch_shapes=[pltpu.VMEM((t, LANES), F32), pltpu.VMEM((t, LANES), F32),
                        pltpu.VMEM((2, LANES, LANES), F32)],
        compiler_params=_cparams(("parallel", "parallel")),
        name="rwkv",
    )(rkv, rkv, rkv, lora, pvec, ww, wa, wg)


def _gla_body(q_ref, k_ref, v_ref, gg_ref, gkd_ref, wgk_ref, gkb_ref, nw_ref, out_ref,
              o_scr, s_scr, *, seq_len, blk):
    C = CHUNK
    nc = blk // C
    nblk = seq_len // blk
    lane = _iota2((1, LANES), 1)
    head_mask = [lane < GK, lane >= GK]
    si = _iota2((2 * GV, LANES), 0)
    sj = _iota2((2 * GV, LANES), 1)
    same_head = (si >> 7) == (sj >> 6)
    ct = _iota2((C, C), 0)
    cs = _iota2((C, C), 1)

    def run_block(t0, reverse):
        d = 1 if reverse else 0
        rows = pl.ds(t0, blk)
        q = q_ref[0, rows, :] * (GK ** -0.5)
        k = k_ref[0, rows, :]
        v = v_ref[0, rows, :]
        z = _mm(gkd_ref[0, rows, :], wgk_ref[0, d]) + gkb_ref[0, d]
        lg = (jnp.minimum(z, 0.0) - jnp.log1p(jnp.exp(-jnp.abs(z)))) * (1.0 / GATE_NORM)
        bcum = _chunk_cumsum(lg, reverse, 3)
        incl = (cs >= ct) if reverse else (cs <= ct)
        order = list(range(nc - 1, -1, -1) if reverse else range(nc))
        ch = {}
        for c in order:
            sl = slice(c * C, (c + 1) * C)
            bc = bcum[sl]
            b_tot = bc[0:1, :] if reverse else bc[C - 1:C, :]
            ch[c] = dict(q_in=q[sl] * jnp.exp(bc), k_in=k[sl] * jnp.exp(-bc),
                         k_dec=k[sl] * jnp.exp(b_tot - bc), v=v[sl], gam=jnp.exp(b_tot))
        att = {(c, h): jnp.where(incl, _mm_nt(jnp.where(head_mask[h], ch[c]["q_in"], 0.0), ch[c]["k_in"]), 0.0)
               for c in order for h in range(2)}
        intra = {(c, h): _mm(att[c, h], ch[c]["v"][:, h * GV:(h + 1) * GV]) for c in order for h in range(2)}
        ds = {c: jnp.where(same_head, _mm_tn(ch[c]["v"], ch[c]["k_dec"]), 0.0) for c in order}
        s = s_scr[...]
        s_in = {}
        for c in order:
            s_in[c] = s
            s = s * ch[c]["gam"] + ds[c]
        s_scr[...] = s
        outs = [_mm_nt(ch[c]["q_in"], s_in[c]) + jnp.concatenate([intra[c, 0], intra[c, 1]], axis=1)
                for c in range(nc)]
        return jnp.concatenate(outs, axis=0)

    s_scr[...] = jnp.zeros(s_scr.shape, F32)

    def fwd_step(i, carry):
        t0 = pl.multiple_of(i * blk, blk)
        o_scr[pl.ds(t0, blk), :] = run_block(t0, False)
        return carry

    lax.fori_loop(0, nblk, fwd_step, 0)

    s_scr[...] = jnp.zeros(s_scr.shape, F32)

    def bwd_step(i, carry):
        t0 = pl.multiple_of((nblk - 1 - i) * blk, blk)
        o = run_block(t0, True) + o_scr[pl.ds(t0, blk), :]
        gg = gg_ref[0, pl.ds(t0, blk), :]
        res = []
        for h in range(2):
            oh = o[:, h * GV:(h + 1) * GV]
            oh = oh * lax.rsqrt(jnp.mean(oh * oh, axis=-1, keepdims=True) + GLA_EPS) * nw_ref[...]
            gh = gg[:, h * GV:(h + 1) * GV]
            res.append(oh * (gh * _sigmoid(gh)))
        out_ref[0, pl.ds(t0, blk), :] = jnp.concatenate(res, axis=1).astype(out_ref.dtype)
        return carry

    lax.fori_loop(0, nblk, bwd_step, 0)


def _gla(qk, gv, gg, gkd, wgk, gkb, nw, blk):
    b, t, _ = qk.shape
    assert t % blk == 0 and blk % CUM_ROWS == 0
    npair = GH // 2
    body = functools.partial(_gla_body, seq_len=t, blk=blk)
    return pl.pallas_call(
        body,
        grid=(b, npair),
        in_specs=[pl.BlockSpec((1, t, LANES), lambda i, j: (i, 0, j)),
                  pl.BlockSpec((1, t, LANES), lambda i, j: (i, 0, npair + j)),
                  pl.BlockSpec((1, t, 2 * GV), lambda i, j: (i, 0, j)),
                  pl.BlockSpec((1, t, 2 * GV), lambda i, j: (i, 0, j)),
                  pl.BlockSpec((1, t, GKD_W), lambda i, j: (i, 0, 0)),
                  pl.BlockSpec((1, 2, GKD_W, LANES), lambda i, j: (j, 0, 0, 0)),
                  pl.BlockSpec((1, 2, 1, LANES), lambda i, j: (j, 0, 0, 0)),
                  pl.BlockSpec((1, GV), lambda i, j: (0, 0))],
        out_specs=pl.BlockSpec((1, t, 2 * GV), lambda i, j: (i, 0, j)),
        out_shape=jax.ShapeDtypeStruct((b, t, GH * GV), BF16),
        scratch_shapes=[pltpu.VMEM((t, 2 * GV), F32), pltpu.VMEM((2 * GV, LANES), F32)],
        compiler_params=_cparams(("parallel", "parallel")),
        name="gla",
    )(qk, qk, gv, gg, gkd, wgk, gkb, nw)


def _outproj_body(x_ref, rw_ref, gl_ref, w_ref, g_ref, o_ref):
    m = (jnp.dot(rw_ref[...], w_ref[0:RW, :], preferred_element_type=F32)
         + jnp.dot(gl_ref[...], w_ref[RW:, :], preferred_element_type=F32))
    o_ref[...] = x_ref[...] + _rms(m, g_ref[...], NORM_EPS)


def _outproj(x2d, rw2d, gl2d, w_out, g_post, tm):
    n = x2d.shape[0]
    return pl.pallas_call(
        _outproj_body,
        grid=(n // tm,),
        in_specs=[pl.BlockSpec((tm, D_MODEL), lambda i: (i, 0)),
                  pl.BlockSpec((tm, RW), lambda i: (i, 0)),
                  pl.BlockSpec((tm, GH * GV), lambda i: (i, 0)),
                  pl.BlockSpec((D_MODEL, D_MODEL), lambda i: (0, 0)),
                  pl.BlockSpec((1, D_MODEL), lambda i: (0, 0))],
        out_specs=pl.BlockSpec((tm, D_MODEL), lambda i: (i, 0)),
        out_shape=jax.ShapeDtypeStruct((n, D_MODEL), F32),
        compiler_params=_cparams(("parallel",)),
        name="outproj",
    )(x2d, rw2d, gl2d, w_out, g_post)


def _kvproj_body(m_ref, g_ref, w_ref, kv_ref):
    h = _rms(m_ref[...], g_ref[...], NORM_EPS).astype(BF16)
    kv_ref[...] = jnp.dot(h, w_ref[...], preferred_element_type=F32).astype(kv_ref.dtype)


def _kvproj(mem2d, g_mem, wkv, tm):
    n = mem2d.shape[0]
    return pl.pallas_call(
        _kvproj_body,
        grid=(n // tm,),
        in_specs=[pl.BlockSpec((tm, D_MODEL), lambda i: (i, 0)),
                  pl.BlockSpec((1, D_MODEL), lambda i: (0, 0)),
                  pl.BlockSpec((D_MODEL, 2 * D_MODEL), lambda i: (0, 0))],
        out_specs=pl.BlockSpec((tm, 2 * D_MODEL), lambda i: (i, 0)),
        out_shape=jax.ShapeDtypeStruct((n, 2 * D_MODEL), BF16),
        compiler_params=_cparams(("parallel",)),
        name="kvproj",
    )(mem2d, g_mem, wkv)


def _xattn_body(x_ref, kv_ref, gpre_ref, wq_ref, wo_ref, gpost_ref, o_ref):
    x = x_ref[0]
    h = _rms(x, gpre_ref[...], NORM_EPS).astype(BF16)
    q = jnp.dot(h, wq_ref[...], preferred_element_type=F32).astype(BF16)
    acc = jnp.zeros(x.shape, F32)
    for hd in range(X_HEADS):
        cols = slice(hd * X_HD, (hd + 1) * X_HD)
        kh = kv_ref[0, :, hd * X_HD:(hd + 1) * X_HD]
        vh = kv_ref[0, :, D_MODEL + hd * X_HD:D_MODEL + (hd + 1) * X_HD]
        s = lax.dot_general(q[:, cols], kh, (((1,), (1,)), ((), ())),
                            preferred_element_type=F32) * (X_HD ** -0.5)
        e = jnp.exp(s - jnp.max(s, axis=-1, keepdims=True))
        p = e / jnp.sum(e, axis=-1, keepdims=True)
        oh = jnp.dot(p.astype(BF16), vh, preferred_element_type=F32)
        acc = acc + jnp.dot(oh.astype(BF16), wo_ref[cols, :], preferred_element_type=F32)
    o_ref[0] = x + _rms(acc, gpost_ref[...], NORM_EPS)


def _xattn(x3d, kv3d, g_pre, wq, wo, g_post, tm):
    b, t, _ = x3d.shape
    return pl.pallas_call(
        _xattn_body,
        grid=(b, t // tm),
        in_specs=[pl.BlockSpec((1, tm, D_MODEL), lambda i, j: (i, j, 0)),
                  pl.BlockSpec((1, N_MEM, 2 * D_MODEL), lambda i, j: (i, 0, 0)),
                  pl.BlockSpec((1, D_MODEL), lambda i, j: (0, 0)),
                  pl.BlockSpec((D_MODEL, D_MODEL), lambda i, j: (0, 0)),
                  pl.BlockSpec((D_MODEL, D_MODEL), lambda i, j: (0, 0)),
                  pl.BlockSpec((1, D_MODEL), lambda i, j: (0, 0))],
        out_specs=pl.BlockSpec((1, tm, D_MODEL), lambda i, j: (i, j, 0)),
        out_shape=jax.ShapeDtypeStruct((b, t, D_MODEL), F32),
        compiler_params=_cparams(("parallel", "parallel")),
        name="xattn",
    )(x3d, kv3d, g_pre, wq, wo, g_post)


def _ffn_body(x_ref, gpre_ref, w1_ref, w2_ref, gpost_ref, o_ref, h_scr, acc_scr):
    j = pl.program_id(1)

    @pl.when(j == 0)
    def _():
        h_scr[...] = _rms(x_ref[...], gpre_ref[...], NORM_EPS).astype(BF16)
        acc_scr[...] = jnp.zeros_like(acc_scr)

    a = jnp.dot(h_scr[...], w1_ref[...], preferred_element_type=F32)
    a = jnp.square(jnp.maximum(a, 0.0)).astype(BF16)
    acc_scr[...] += jnp.dot(a, w2_ref[...], preferred_element_type=F32)

    @pl.when(j == pl.num_programs(1) - 1)
    def _():
        o_ref[...] = x_ref[...] + _rms(acc_scr[...], gpost_ref[...], NORM_EPS)


def _ffn(x2d, g_pre, w1, w2, g_post, tm, tf):
    n = x2d.shape[0]
    return pl.pallas_call(
        _ffn_body,
        grid=(n // tm, D_FF // tf),
        in_specs=[pl.BlockSpec((tm, D_MODEL), lambda i, j: (i, 0)),
                  pl.BlockSpec((1, D_MODEL), lambda i, j: (0, 0)),
                  pl.BlockSpec((D_MODEL, tf), lambda i, j: (0, j)),
                  pl.BlockSpec((tf, D_MODEL), lambda i, j: (j, 0)),
                  pl.BlockSpec((1, D_MODEL), lambda i, j: (0, 0))],
        out_specs=pl.BlockSpec((tm, D_MODEL), lambda i, j: (i, 0)),
        out_shape=jax.ShapeDtypeStruct((n, D_MODEL), F32),
        scratch_shapes=[pltpu.VMEM((tm, D_MODEL), BF16), pltpu.VMEM((tm, D_MODEL), F32)],
        compiler_params=_cparams(("parallel", "arbitrary")),
        name="ffn",
    )(x2d, g_pre, w1, w2, g_post)


def _pack_params(p):
    w_in = p["w_in"]
    g0 = R_COLS
    zeros = lambda n: jnp.zeros((D_MODEL, n), w_in.dtype)
    w_perm = jnp.concatenate([
        w_in[:, :R_COLS], zeros(LORA_W - (R_COLS - 3 * RW)),
        w_in[:, g0:g0 + 2 * GH * GK],
        w_in[:, g0 + 2 * GH * GK:g0 + 2 * GH * GK + GH * GV],
        w_in[:, g0 + 2 * GH * GK + GH * GV + GK_LORA:],
        w_in[:, g0 + 2 * GH * GK + GH * GV:g0 + 2 * GH * GK + GH * GV + GK_LORA], zeros(GKD_W - GK_LORA),
    ], axis=1).astype(BF16)
    row = lambda v: v.reshape(1, -1)
    mu_p, mu_n = p["mu_prev"], p["mu_next"]
    pvec = jnp.concatenate([
        row(p["w0_f"]), row(p["w0_b"]), row(p["a0_f"]), row(p["a0_b"]),
        row(p["k_k"]), row(p["k_a"]), row(p["r_k"]), row(p["lnx_w"]), row(p["lnx_b"]),
        jnp.zeros((PV_ROWS - 9, RW), F32)], axis=0)
    pad_mu = lambda v: jnp.pad(v, (0, C_QK - R_COLS))
    mu = jnp.stack([pad_mu(mu_p), pad_mu(mu_n)], axis=0)
    nhp = RW // LANES

    def lora_pair(wf, wb):
        wf = wf.reshape(-1, nhp, LANES).transpose(1, 0, 2)
        wb = wb.reshape(-1, nhp, LANES).transpose(1, 0, 2)
        z = jnp.zeros_like(wf)
        return jnp.stack([jnp.concatenate([wf, z], axis=1), jnp.concatenate([z, wb], axis=1)],
                         axis=1).astype(BF16)

    ww = lora_pair(p["w2_f"], p["w2_b"])
    wa = lora_pair(p["a2_f"], p["a2_b"])
    wg = jnp.pad(p["g2"], ((0, 2 * LANES - GATE_LORA), (0, 0)))
    wg = wg.reshape(2 * LANES, nhp, LANES).transpose(1, 0, 2).astype(BF16)
    npair = GH // 2

    def gk_pair(w):
        w = jnp.pad(w, ((0, GKD_W - GK_LORA), (0, 0)))
        return w.reshape(GKD_W, npair, LANES).transpose(1, 0, 2)

    wgk = jnp.stack([gk_pair(p["gk2_f"]), gk_pair(p["gk2_b"])], axis=1).astype(BF16)
    gkb = jnp.stack([p["gkb_f"].reshape(npair, 1, LANES), p["gkb_b"].reshape(npair, 1, LANES)], axis=1)
    return dict(
        w_perm=w_perm, pvec=pvec, mu=mu, ww=ww, wa=wa, wg=wg, wgk=wgk, gkb=gkb,
        nw=row(p["gla_norm_w"]),
        g_mix_pre=row(p["g_mix_pre"]), g_mix_post=row(p["g_mix_post"]),
        g_x_pre=row(p["g_x_pre"]), g_x_post=row(p["g_x_post"]), g_mem=row(p["g_mem"]),
        g_ffn_pre=row(p["g_ffn_pre"]), g_ffn_post=row(p["g_ffn_post"]),
        w_out=p["w_out"].astype(BF16), wq=p["wq_x"].astype(BF16), wkv=p["wkv_x"].astype(BF16),
        wo=p["wo_x"].astype(BF16), w1=p["w_ff1"].astype(BF16), w2=p["w_ff2"].astype(BF16))


def _pick(n, pref):
    t = pref
    while n % t:
        t //= 2
    return t


def _trunk(x, mem, pk):
    b, t, _ = x.shape
    n = b * t
    tm = _pick(n, 512)
    x2d = x.reshape(n, D_MODEL)
    rkv, lora, qk, gv, gg, gkd = _inproj(x2d, pk["g_mix_pre"], pk["w_perm"], pk["mu"], _pick(t, 512), t)
    r3 = lambda a: a.reshape(b, t, a.shape[-1])
    rw = _rwkv(r3(rkv), r3(lora), pk["pvec"], pk["ww"], pk["wa"], pk["wg"], _pick(t // 2, 512))
    gl = _gla(r3(qk), r3(gv), r3(gg), r3(gkd), pk["wgk"], pk["gkb"], pk["nw"], _pick(t, 512))
    x1 = _outproj(x2d, rw.reshape(n, RW), gl.reshape(n, GH * GV), pk["w_out"], pk["g_mix_post"], tm)
    nm = mem.shape[0] * mem.shape[1]
    kv = _kvproj(mem.reshape(nm, D_MODEL), pk["g_mem"], pk["wkv"], _pick(nm, 512))
    x2 = _xattn(x1.reshape(b, t, D_MODEL), kv.reshape(mem.shape[0], mem.shape[1], 2 * D_MODEL),
                pk["g_x_pre"], pk["wq"], pk["wo"], pk["g_x_post"], _pick(t, 512))
    y = _ffn(x2.reshape(n, D_MODEL), pk["g_ffn_pre"], pk["w1"], pk["w2"], pk["g_ffn_post"],
             tm, 1024)
    return y.reshape(b, t, D_MODEL)


def kernel(x_prompt, x_sample, mem_prompt, mem_sample, g_mix_pre, w_in, mu_prev, mu_next, w0_f, w2_f, w0_b, w2_b, a0_f, a2_f, a0_b, a2_b, g2, k_k, k_a, r_k, lnx_w, lnx_b, gk2_f, gkb_f, gk2_b, gkb_b, gla_norm_w, w_out, g_mix_post, g_x_pre, g_mem, wq_x, wkv_x, wo_x, g_x_post, g_ffn_pre, w_ff1, w_ff2, g_ffn_post):
    params = dict(
        g_mix_pre=g_mix_pre, w_in=w_in, mu_prev=mu_prev, mu_next=mu_next, w0_f=w0_f, w2_f=w2_f,
        w0_b=w0_b, w2_b=w2_b, a0_f=a0_f, a2_f=a2_f, a0_b=a0_b, a2_b=a2_b, g2=g2, k_k=k_k, k_a=k_a,
        r_k=r_k, lnx_w=lnx_w, lnx_b=lnx_b, gk2_f=gk2_f, gkb_f=gkb_f, gk2_b=gk2_b, gkb_b=gkb_b,
        gla_norm_w=gla_norm_w, w_out=w_out, g_mix_post=g_mix_post, g_x_pre=g_x_pre, g_mem=g_mem,
        wq_x=wq_x, wkv_x=wkv_x, wo_x=wo_x, g_x_post=g_x_post, g_ffn_pre=g_ffn_pre, w_ff1=w_ff1,
        w_ff2=w_ff2, g_ffn_post=g_ffn_post)
    assert w_in.shape[0] == 1, "single-layer stack expected"
    pk = _pack_params({name: w[0] for name, w in params.items()})
    return (_trunk(x_prompt, mem_prompt, pk), _trunk(x_sample, mem_sample, pk))
```

```python
import functools
import math

import jax
import jax.numpy as jnp
from jax import lax
from jax.experimental import pallas as pl
from jax.experimental.pallas import tpu as pltpu

F32 = jnp.float32
BF16 = jnp.bfloat16

D_MODEL = 1024
RW = 512
RN = 64
DECAY_LORA = 64
AAA_LORA = 64
GATE_LORA = 160
GN_EPS = 64e-5
GH = 4
GV = 128
GK = 64
GK_LORA = 16
GATE_NORM = 16.0
CHUNK = 64
GLA_EPS = 1e-5
R_COLS = 3 * RW + 2 * DECAY_LORA + 2 * AAA_LORA + GATE_LORA
N_MEM = 256
X_HEADS = 4
X_HD = D_MODEL // X_HEADS
D_FF = 4 * D_MODEL
NORM_EPS = 1e-6

LANES = 128
CUM_ROWS = 2 * CHUNK
HALO = 16
SHIFT_COLS = 512
LORA_W = 512
GKD_W = 128
C_RKV, C_LORA, C_QK, C_GV, C_GG, C_GKD = 0, 1536, 2048, 2560, 3072, 3584
N_PROJ = 3712
VMEM_LIMIT = 56 * 1024 * 1024


def _cparams(sem):
    return pltpu.CompilerParams(dimension_semantics=sem, vmem_limit_bytes=VMEM_LIMIT)


def _mm(a, b):
    return jnp.dot(a.astype(BF16), b.astype(BF16), preferred_element_type=F32)


def _mm_nt(a, b):
    return lax.dot_general(a.astype(BF16), b.astype(BF16), (((1,), (1,)), ((), ())),
                           preferred_element_type=F32)


def _mm_tn(a, b):
    return lax.dot_general(a.astype(BF16), b.astype(BF16), (((0,), (0,)), ((), ())),
                           preferred_element_type=F32)


def _split2(x):
    hi = x.astype(BF16)
    lo = (x - hi.astype(F32)).astype(BF16)
    return hi, lo


def _mm_hp(a, b):
    ah, al = _split2(a)
    bh, bl = _split2(b)
    d = lambda x, y: jnp.dot(x, y, preferred_element_type=F32)
    return d(ah, bh) + (d(ah, bl) + d(al, bh))


def _mm_lhs01(m01, x, terms):
    acc = None
    rem = x
    for i in range(terms):
        piece = rem.astype(BF16)
        part = jnp.dot(m01, piece, preferred_element_type=F32)
        acc = part if acc is None else acc + part
        if i + 1 < terms:
            rem = rem - piece.astype(F32)
    return acc


def _rms(x, g, eps):
    return x * lax.rsqrt(jnp.mean(x * x, axis=-1, keepdims=True) + eps) * g


def _sigmoid(x):
    return 1.0 / (1.0 + jnp.exp(-x))


def _iota2(shape, dim):
    return lax.broadcasted_iota(jnp.int32, shape, dim)


def _chunk_tri(n, reverse):
    t = _iota2((n, n), 0)
    s = _iota2((n, n), 1)
    same = (t >> 6) == (s >> 6)
    tri = (s >= t) if reverse else (s <= t)
    return jnp.where(same & tri, 1.0, 0.0).astype(BF16)


def _chunk_cumsum(x, reverse, terms):
    tri = _chunk_tri(CUM_ROWS, reverse)
    parts = [_mm_lhs01(tri, x[i:i + CUM_ROWS], terms) for i in range(0, x.shape[0], CUM_ROWS)]
    return parts[0] if len(parts) == 1 else jnp.concatenate(parts, axis=0)


def _inproj_body(x_ref, xp_ref, xn_ref, g_ref, w_ref, mu_ref, rkv_ref, lora_ref, qk_ref, gv_ref, gg_ref,
                 gkd_ref, *, tiles_per_seq):
    tm = x_ref.shape[0]
    i = pl.program_id(0)
    g = g_ref[...]
    h = _rms(x_ref[...], g, NORM_EPS).astype(BF16)
    h_prev = _rms(xp_ref[...], g, NORM_EPS).astype(BF16)
    h_next = _rms(xn_ref[...], g, NORM_EPS).astype(BF16)
    h_ext = jnp.concatenate([h_prev, h, h_next], axis=0)
    pos = i % tiles_per_seq
    has_prev = pos > 0
    has_next = pos < tiles_per_seq - 1
    rid = _iota2((tm, SHIFT_COLS), 0)

    def proj(lo, hi):
        return jnp.dot(h, w_ref[:, lo:hi], preferred_element_type=F32)

    for lo in range(C_RKV, C_QK, SHIFT_COLS):
        p_ext = jnp.dot(h_ext, w_ref[:, lo:lo + SHIFT_COLS], preferred_element_type=F32)
        p = p_ext[HALO:HALO + tm, :]
        p_before = p_ext[HALO - 1:HALO, :]
        p_after = p_ext[HALO + tm:HALO + tm + 1, :]
        prev = jnp.where(rid == 0, jnp.where(has_prev, p_before, 0.0), pltpu.roll(p, 1, 0))
        nxt = jnp.where(rid == tm - 1, jnp.where(has_next, p_after, 0.0), pltpu.roll(p, tm - 1, 0))
        mu = mu_ref[:, lo:lo + SHIFT_COLS]
        shifted = p + mu[0:1, :] * (prev - p) + mu[1:2, :] * (nxt - p)
        if lo < C_LORA:
            rkv_ref[:, lo:lo + SHIFT_COLS] = shifted
        else:
            lora_ref[:, lo - C_LORA:lo - C_LORA + SHIFT_COLS] = shifted
    qk_ref[...] = proj(C_QK, C_GV)
    gv_ref[...] = proj(C_GV, C_GG)
    gg_ref[...] = proj(C_GG, C_GKD)
    gkd_ref[...] = proj(C_GKD, N_PROJ)


def _inproj(x2d, g, w_perm, mu, tm, seq_len):
    n = x2d.shape[0]
    assert seq_len % tm == 0 and tm % HALO == 0
    widths = (C_LORA - C_RKV, C_QK - C_LORA, C_GV - C_QK, C_GG - C_GV, C_GKD - C_GG, N_PROJ - C_GKD)
    hb = tm // HALO
    last = n // HALO - 1
    body = functools.partial(_inproj_body, tiles_per_seq=seq_len // tm)
    return pl.pallas_call(
        body,
        grid=(n // tm,),
        in_specs=[pl.BlockSpec((tm, D_MODEL), lambda i: (i, 0)),
                  pl.BlockSpec((HALO, D_MODEL), lambda i: (jnp.maximum(i * hb - 1, 0), 0)),
                  pl.BlockSpec((HALO, D_MODEL), lambda i: (jnp.minimum((i + 1) * hb, last), 0)),
                  pl.BlockSpec((1, D_MODEL), lambda i: (0, 0)),
                  pl.BlockSpec((D_MODEL, N_PROJ), lambda i: (0, 0)),
                  pl.BlockSpec((2, C_QK), lambda i: (0, 0))],
        out_specs=[pl.BlockSpec((tm, w), lambda i: (i, 0)) for w in widths],
        out_shape=[jax.ShapeDtypeStruct((n, w), F32) for w in widths],
        compiler_params=_cparams(("parallel",)),
        name="inproj",
    )(x2d, x2d, x2d, g, w_perm, mu)


PV_W0_F, PV_W0_B, PV_A0_F, PV_A0_B, PV_KK, PV_KA, PV_RK, PV_LNW, PV_LNB = range(9)
PV_ROWS = 16


def _rwkv_body(r_ref, k_ref, v_ref, lo_ref, pv_ref, ww_ref, wa_ref, wg_ref, out_ref,
               y_scr, bon_scr, s_scr, *, seq_len, blk):
    C = CHUNK
    nc = blk // C
    nblk = seq_len // blk
    half = nblk // 2
    pv = pv_ref[...]
    prow = lambda i: pv[i:i + 1, :]

    lane = _iota2((1, LANES), 1)
    head_mask = [lane < RN, lane >= RN]
    li = _iota2((LANES, LANES), 0)
    lj = _iota2((LANES, LANES), 1)
    same_head = (li >> 6) == (lj >> 6)
    seg01 = jnp.where(same_head, 1.0, 0.0).astype(BF16)
    eye128 = li == lj
    row_c = _iota2((C, LANES), 0)
    col_c = _iota2((C, LANES), 1) & (C - 1)
    eye_p = jnp.where(row_c == col_c, 1.0, 0.0)
    zeros_cl = jnp.zeros((C, LANES), F32)

    def seg_sum(x):
        return _mm(x, seg01)

    def block_inputs(t0, reverse):
        d = 1 if reverse else 0
        rows = pl.ds(t0, blk)
        r, k, v, lo = r_ref[0, rows, :], k_ref[0, rows, :], v_ref[0, rows, :], lo_ref[0, rows, :]
        zw = _mm(jnp.tanh(lo[:, 0:LANES]), ww_ref[0, d])
        za = _mm(lo[:, LANES:2 * LANES], wa_ref[0, d])
        lw = -_sigmoid(prow(PV_W0_F + d) + zw) * math.exp(-0.5)
        a = _sigmoid(prow(PV_A0_F + d) + za)
        kd = k * (1.0 + (a - 1.0) * prow(PV_KA))
        kk = k * prow(PV_KK)
        kk = kk * lax.rsqrt(seg_sum(kk * kk) + 1e-12)
        bonus = seg_sum(r * kd * prow(PV_RK)) * v
        g = _chunk_cumsum(lw, reverse, 2)
        return dict(r=r, v=v, kd=kd, kk=kk, a=a, lw=lw, g=g, bonus=bonus, lo=lo)

    def bd(x):
        return jnp.concatenate([jnp.where(head_mask[0], x, 0.0), jnp.where(head_mask[1], x, 0.0)], axis=0)

    def block_terms(ctxs):
        tri_s, tri_i, ch, chains = {}, {}, {}, []
        for d, q, order, reverse in ctxs:
            tri_s[d] = (col_c > row_c) if reverse else (col_c < row_c)
            tri_i[d] = (col_c >= row_c) if reverse else (col_c <= row_c)
            for c in order:
                sl = slice(c * C, (c + 1) * C)
                g, lw, kk = q["g"][sl], q["lw"][sl], q["kk"][sl]
                r, v, kd, a = q["r"][sl], q["v"][sl], q["kd"][sl], q["a"][sl]
                g_tot = g[0:1, :] if reverse else g[C - 1:C, :]
                eng = jnp.exp(-g)
                beta = kk * a
                dec = jnp.exp(g_tot - g)
                ch[d, c] = dict(v=v, rt=r * jnp.exp(g), at=-kk * jnp.exp(g - lw), bt=beta * eng, kt=kd * eng,
                                bh=beta * dec, kh=kd * dec, gam=jnp.exp(g_tot))
                chains.append((d, c))
        a_ab, a_ak, a_rbk = {}, {}, {}
        for k in chains:
            e = ch[k]
            gm = _mm_nt(jnp.concatenate([e["at"], e["rt"]], axis=0),
                        jnp.concatenate([bd(e["bt"]), bd(e["kt"])], axis=0))
            a_ab[k] = jnp.where(tri_s[k[0]], gm[:C, :LANES], 0.0)
            a_ak[k] = jnp.where(tri_s[k[0]], gm[:C, LANES:], 0.0)
            a_rbk[k] = jnp.concatenate([jnp.where(tri_i[k[0]], gm[C:, :LANES], 0.0),
                                        jnp.where(tri_i[k[0]], gm[C:, LANES:], 0.0)], axis=1)
        tm = {k: eye_p + jnp.where((row_c >> 1) == (col_c >> 1), a_ab[k], 0.0) for k in chains}
        for lvl in range(1, 6):
            same = (row_c >> (lvl + 1)) == (col_c >> (lvl + 1))
            lower = (((row_c >> lvl) & 1) == 1) & (((col_c >> lvl) & 1) == 0)
            upper = (((row_c >> lvl) & 1) == 0) & (((col_c >> lvl) & 1) == 1)
            off = {d: same & (upper if reverse else lower) for d, _, _, reverse in ctxs}
            left = {k: _mm(tm[k], bd(jnp.where(off[k[0]], a_ab[k], 0.0))) for k in chains}
            tm = {k: tm[k] + _mm(left[k], bd(tm[k])) for k in chains}
        akv = {k: _mm(a_ak[k], bd(ch[k]["v"])) for k in chains}
        wu = {k: _mm(tm[k], jnp.concatenate([bd(ch[k]["at"]), bd(akv[k])], axis=1)) for k in chains}
        def chunk_terms(k):
            e = ch[k]
            rhs = jnp.concatenate(
                [jnp.concatenate([bd(wu[k][:, :LANES]), bd(wu[k][:, LANES:])], axis=1),
                 jnp.concatenate([jnp.zeros((LANES, LANES), F32), bd(e["v"])], axis=1)], axis=0)
            qy = _mm(a_rbk[k], rhs)
            lhs = jnp.concatenate([wu[k], jnp.concatenate([zeros_cl, e["v"]], axis=1)], axis=0)
            pn = _mm_tn(lhs, jnp.concatenate([e["bh"], e["kh"]], axis=0))
            p = jnp.where(eye128, e["gam"], 0.0) + jnp.where(same_head, pn[:LANES], 0.0)
            n0t = jnp.where(same_head, pn[LANES:], 0.0)
            return e["rt"] + qy[:, :LANES], qy[:, LANES:], p, n0t

        return chunk_terms

    def run_pair(tf, tb):
        qf = block_inputs(tf, False)
        qb = block_inputs(tb, True)
        ctxs = [(0, qf, list(range(nc)), False), (1, qb, list(range(nc - 1, -1, -1)), True)]
        chunk_terms = block_terms(ctxs)
        s = [s_scr[0], s_scr[1]]
        ys = [[None] * nc, [None] * nc]
        terms = {(d, order[0]): chunk_terms((d, order[0])) for d, _, order, _ in ctxs}
        for step in range(nc):
            if step + 1 < nc:
                for d, _, order, _ in ctxs:
                    terms[d, order[step + 1]] = chunk_terms((d, order[step + 1]))
            for d, _, order, _ in ctxs:
                qh, y0, p, n0t = terms[d, order[step]]
                ys[d][order[step]] = _mm_nt(qh, s[d]) + y0
                s[d] = _mm_hp(s[d], p) + n0t
        s_scr[0] = s[0]
        s_scr[1] = s[1]
        return qf, jnp.concatenate(ys[0], axis=0), qb, jnp.concatenate(ys[1], axis=0)

    def finish(q, y, bonus, t0):
        mean = seg_sum(y) * (1.0 / RN)
        dy = y - mean
        var = seg_sum(dy * dy) * (1.0 / RN)
        gn = dy * lax.rsqrt(var + GN_EPS) * prow(PV_LNW) + prow(PV_LNB)
        gate = _mm(_sigmoid(q["lo"][:, 2 * LANES:]), wg_ref[0])
        out_ref[0, pl.ds(t0, blk), :] = ((gn + bonus) * gate).astype(out_ref.dtype)

    s_scr[...] = jnp.zeros(s_scr.shape, F32)

    def block_starts(i):
        return pl.multiple_of(i * blk, blk), pl.multiple_of((nblk - 1 - i) * blk, blk)

    def first_half(i, carry):
        tf, tb = block_starts(i)
        qf, yf, qb, yb = run_pair(tf, tb)
        y_scr[pl.ds(tf, blk), :] = yf
        bon_scr[pl.ds(tf, blk), :] = qf["bonus"]
        y_scr[pl.ds(tb, blk), :] = yb
        bon_scr[pl.ds(tb, blk), :] = qb["bonus"]
        return carry

    def second_half(i, carry):
        tf, tb = block_starts(i)
        qf, yf, qb, yb = run_pair(tf, tb)
        finish(qf, yf + y_scr[pl.ds(tf, blk), :], qf["bonus"] + bon_scr[pl.ds(tf, blk), :], tf)
        finish(qb, yb + y_scr[pl.ds(tb, blk), :], qb["bonus"] + bon_scr[pl.ds(tb, blk), :], tb)
        return carry

    lax.fori_loop(0, half, first_half, 0)
    lax.fori_loop(half, nblk, second_half, 0)


def _rwkv(rkv, lora, pvec, ww, wa, wg, blk):
    b, t, _ = rkv.shape
    assert t % (2 * blk) == 0 and blk % CUM_ROWS == 0
    nhp = RW // LANES
    col = lambda off: pl.BlockSpec((1, t, LANES), lambda i, j, off=off: (i, 0, off + j))
    body = functools.partial(_rwkv_body, seq_len=t, blk=blk)
    return pl.pallas_call(
        body,
        grid=(b, nhp),
        in_specs=[col(0), col(nhp), col(2 * nhp),
                  pl.BlockSpec((1, t, LORA_W), lambda i, j: (i, 0, 0)),
                  pl.BlockSpec((PV_ROWS, LANES), lambda i, j: (0, j)),
                  pl.BlockSpec((1, 2, LANES, LANES), lambda i, j: (j, 0, 0, 0)),
                  pl.BlockSpec((1, 2, LANES, LANES), lambda i, j: (j, 0, 0, 0)),
                  pl.BlockSpec((1, 2 * LANES, LANES), lambda i, j: (j, 0, 0))],
        out_specs=pl.BlockSpec((1, t, LANES), lambda i, j: (i, 0, j)),
        out_shape=jax.ShapeDtypeStruct((b, t, RW), BF16),
        scratch_shapes=[pltpu.VMEM((t, LANES), F32), pltpu.VMEM((t, LANES), F32),
                        pltpu.VMEM((2, LANES, LANES), F32)],
        compiler_params=_cparams(("parallel", "parallel")),
        name="rwkv",
    )(rkv, rkv, rkv, lora, pvec, ww, wa, wg)


def _gla_body(q_ref, k_ref, v_ref, gg_ref, gkd_ref, wgk_ref, gkb_ref, nw_ref, out_ref,
              o_scr, s_scr, *, seq_len, blk):
    C = CHUNK
    nc = blk // C
    nblk = seq_len // blk
    lane = _iota2((1, LANES), 1)
    head_mask = [lane < GK, lane >= GK]
    si = _iota2((2 * GV, LANES), 0)
    sj = _iota2((2 * GV, LANES), 1)
    same_head = (si >> 7) == (sj >> 6)
    ct = _iota2((C, C), 0)
    cs = _iota2((C, C), 1)

    def run_block(t0, reverse):
        d = 1 if reverse else 0
        rows = pl.ds(t0, blk)
        q = q_ref[0, rows, :] * (GK ** -0.5)
        k = k_ref[0, rows, :]
        v = v_ref[0, rows, :]
        z = _mm(gkd_ref[0, rows, :], wgk_ref[0, d]) + gkb_ref[0, d]
        lg = (jnp.minimum(z, 0.0) - jnp.log1p(jnp.exp(-jnp.abs(z)))) * (1.0 / GATE_NORM)
        bcum = _chunk_cumsum(lg, reverse, 3)
        incl = (cs >= ct) if reverse else (cs <= ct)
        order = list(range(nc - 1, -1, -1) if reverse else range(nc))
        ch = {}
        for c in order:
            sl = slice(c * C, (c + 1) * C)
            bc = bcum[sl]
            b_tot = bc[0:1, :] if reverse else bc[C - 1:C, :]
            ch[c] = dict(q_in=q[sl] * jnp.exp(bc), k_in=k[sl] * jnp.exp(-bc),
                         k_dec=k[sl] * jnp.exp(b_tot - bc), v=v[sl], gam=jnp.exp(b_tot))
        att = {(c, h): jnp.where(incl, _mm_nt(jnp.where(head_mask[h], ch[c]["q_in"], 0.0), ch[c]["k_in"]), 0.0)
               for c in order for h in range(2)}
        intra = {(c, h): _mm(att[c, h], ch[c]["v"][:, h * GV:(h + 1) * GV]) for c in order for h in range(2)}
        ds = {c: jnp.where(same_head, _mm_tn(ch[c]["v"], ch[c]["k_dec"]), 0.0) for c in order}
        s = s_scr[...]
        s_in = {}
        for c in order:
            s_in[c] = s
            s = s * ch[c]["gam"] + ds[c]
        s_scr[...] = s
        outs = [_mm_nt(ch[c]["q_in"], s_in[c]) + jnp.concatenate([intra[c, 0], intra[c, 1]], axis=1)
                for c in range(nc)]
        return jnp.concatenate(outs, axis=0)

    s_scr[...] = jnp.zeros(s_scr.shape, F32)

    def fwd_step(i, carry):
        t0 = pl.multiple_of(i * blk, blk)
        o_scr[pl.ds(t0, blk), :] = run_block(t0, False)
        return carry

    lax.fori_loop(0, nblk, fwd_step, 0)

    s_scr[...] = jnp.zeros(s_scr.shape, F32)

    def bwd_step(i, carry):
        t0 = pl.multiple_of((nblk - 1 - i) * blk, blk)
        o = run_block(t0, True) + o_scr[pl.ds(t0, blk), :]
        gg = gg_ref[0, pl.ds(t0, blk), :]
        res = []
        for h in range(2):
            oh = o[:, h * GV:(h + 1) * GV]
            oh = oh * lax.rsqrt(jnp.mean(oh * oh, axis=-1, keepdims=True) + GLA_EPS) * nw_ref[...]
            gh = gg[:, h * GV:(h + 1) * GV]
            res.append(oh * (gh * _sigmoid(gh)))
        out_ref[0, pl.ds(t0, blk), :] = jnp.concatenate(res, axis=1).astype(out_ref.dtype)
        return carry

    lax.fori_loop(0, nblk, bwd_step, 0)


def _gla(qk, gv, gg, gkd, wgk, gkb, nw, blk):
    b, t, _ = qk.shape
    assert t % blk == 0 and blk % CUM_ROWS == 0
    npair = GH // 2
    body = functools.partial(_gla_body, seq_len=t, blk=blk)
    return pl.pallas_call(
        body,
        grid=(b, npair),
        in_specs=[pl.BlockSpec((1, t, LANES), lambda i, j: (i, 0, j)),
                  pl.BlockSpec((1, t, LANES), lambda i, j: (i, 0, npair + j)),
                  pl.BlockSpec((1, t, 2 * GV), lambda i, j: (i, 0, j)),
                  pl.BlockSpec((1, t, 2 * GV), lambda i, j: (i, 0, j)),
                  pl.BlockSpec((1, t, GKD_W), lambda i, j: (i, 0, 0)),
                  pl.BlockSpec((1, 2, GKD_W, LANES), lambda i, j: (j, 0, 0, 0)),
                  pl.BlockSpec((1, 2, 1, LANES), lambda i, j: (j, 0, 0, 0)),
                  pl.BlockSpec((1, GV), lambda i, j: (0, 0))],
        out_specs=pl.BlockSpec((1, t, 2 * GV), lambda i, j: (i, 0, j)),
        out_shape=jax.ShapeDtypeStruct((b, t, GH * GV), BF16),
        scratch_shapes=[pltpu.VMEM((t, 2 * GV), F32), pltpu.VMEM((2 * GV, LANES), F32)],
        compiler_params=_cparams(("parallel", "parallel")),
        name="gla",
    )(qk, qk, gv, gg, gkd, wgk, gkb, nw)


def _kvproj_body(m_ref, g_ref, w_ref, kv_ref):
    h = _rms(m_ref[...], g_ref[...], NORM_EPS).astype(BF16)
    kv_ref[...] = jnp.dot(h, w_ref[...], preferred_element_type=F32).astype(kv_ref.dtype)


def _kvproj(mem2d, g_mem, wkv, tm):
    n = mem2d.shape[0]
    return pl.pallas_call(
        _kvproj_body,
        grid=(n // tm,),
        in_specs=[pl.BlockSpec((tm, D_MODEL), lambda i: (i, 0)),
                  pl.BlockSpec((1, D_MODEL), lambda i: (0, 0)),
                  pl.BlockSpec((D_MODEL, 2 * D_MODEL), lambda i: (0, 0))],
        out_specs=pl.BlockSpec((tm, 2 * D_MODEL), lambda i: (i, 0)),
        out_shape=jax.ShapeDtypeStruct((n, 2 * D_MODEL), BF16),
        compiler_params=_cparams(("parallel",)),
        name="kvproj",
    )(mem2d, g_mem, wkv)


def _xattn_body(x_ref, rw_ref, gl_ref, kv_ref, wout_ref, gmix_ref, gpre_ref, wq_ref, wo_ref, gpost_ref, o_ref):
    mixed = (jnp.dot(rw_ref[0], wout_ref[0:RW, :], preferred_element_type=F32)
             + jnp.dot(gl_ref[0], wout_ref[RW:, :], preferred_element_type=F32))
    x = x_ref[0] + _rms(mixed, gmix_ref[...], NORM_EPS)
    h = _rms(x, gpre_ref[...], NORM_EPS).astype(BF16)
    q = jnp.dot(h, wq_ref[...], preferred_element_type=F32).astype(BF16)
    heads = range(X_HEADS)
    scores = [lax.dot_general(q[:, hd * X_HD:(hd + 1) * X_HD], kv_ref[0, :, hd * X_HD:(hd + 1) * X_HD],
                              (((1,), (1,)), ((), ())), preferred_element_type=F32) * (X_HD ** -0.5)
              for hd in heads]
    probs = []
    for s in scores:
        e = jnp.exp(s - jnp.max(s, axis=-1, keepdims=True))
        probs.append((e * (1.0 / jnp.sum(e, axis=-1, keepdims=True))).astype(BF16))
    outs = [jnp.dot(probs[hd], kv_ref[0, :, D_MODEL + hd * X_HD:D_MODEL + (hd + 1) * X_HD],
                    preferred_element_type=F32).astype(BF16) for hd in heads]
    att = jnp.dot(jnp.concatenate(outs, axis=1), wo_ref[...], preferred_element_type=F32)
    o_ref[0] = x + _rms(att, gpost_ref[...], NORM_EPS)


def _xattn(x3d, rw3d, gl3d, kv3d, w_out, g_mix_post, g_pre, wq, wo, g_post, tm):
    b, t, _ = x3d.shape
    row = lambda w: pl.BlockSpec((1, tm, w), lambda i, j: (i, j, 0))
    const = lambda r, c: pl.BlockSpec((r, c), lambda i, j: (0, 0))
    return pl.pallas_call(
        _xattn_body,
        grid=(b, t // tm),
        in_specs=[row(D_MODEL), row(RW), row(GH * GV),
                  pl.BlockSpec((1, N_MEM, 2 * D_MODEL), lambda i, j: (i, 0, 0)),
                  const(D_MODEL, D_MODEL), const(1, D_MODEL), const(1, D_MODEL),
                  const(D_MODEL, D_MODEL), const(D_MODEL, D_MODEL), const(1, D_MODEL)],
        out_specs=row(D_MODEL),
        out_shape=jax.ShapeDtypeStruct((b, t, D_MODEL), F32),
        compiler_params=_cparams(("parallel", "parallel")),
        name="xattn",
    )(x3d, rw3d, gl3d, kv3d, w_out, g_mix_post, g_pre, wq, wo, g_post)


def _ffn_body(x_ref, gpre_ref, w1_ref, w2_ref, gpost_ref, o_ref, h_scr, acc_scr):
    j = pl.program_id(1)

    @pl.when(j == 0)
    def _():
        h_scr[...] = _rms(x_ref[...], gpre_ref[...], NORM_EPS).astype(BF16)
        acc_scr[...] = jnp.zeros_like(acc_scr)

    a = jnp.dot(h_scr[...], w1_ref[...], preferred_element_type=F32)
    a = jnp.square(jnp.maximum(a, 0.0)).astype(BF16)
    acc_scr[...] += jnp.dot(a, w2_ref[...], preferred_element_type=F32)

    @pl.when(j == pl.num_programs(1) - 1)
    def _():
        o_ref[...] = x_ref[...] + _rms(acc_scr[...], gpost_ref[...], NORM_EPS)


def _ffn(x2d, g_pre, w1, w2, g_post, tm, tf):
    n = x2d.shape[0]
    return pl.pallas_call(
        _ffn_body,
        grid=(n // tm, D_FF // tf),
        in_specs=[pl.BlockSpec((tm, D_MODEL), lambda i, j: (i, 0)),
                  pl.BlockSpec((1, D_MODEL), lambda i, j: (0, 0)),
                  pl.BlockSpec((D_MODEL, tf), lambda i, j: (0, j)),
                  pl.BlockSpec((tf, D_MODEL), lambda i, j: (j, 0)),
                  pl.BlockSpec((1, D_MODEL), lambda i, j: (0, 0))],
        out_specs=pl.BlockSpec((tm, D_MODEL), lambda i, j: (i, 0)),
        out_shape=jax.ShapeDtypeStruct((n, D_MODEL), F32),
        scratch_shapes=[pltpu.VMEM((tm, D_MODEL), BF16), pltpu.VMEM((tm, D_MODEL), F32)],
        compiler_params=_cparams(("parallel", "arbitrary")),
        name="ffn",
    )(x2d, g_pre, w1, w2, g_post)


def _pack_params(p):
    w_in = p["w_in"]
    g0 = R_COLS
    zeros = lambda n: jnp.zeros((D_MODEL, n), w_in.dtype)
    w_perm = jnp.concatenate([
        w_in[:, :R_COLS], zeros(LORA_W - (R_COLS - 3 * RW)),
        w_in[:, g0:g0 + 2 * GH * GK],
        w_in[:, g0 + 2 * GH * GK:g0 + 2 * GH * GK + GH * GV],
        w_in[:, g0 + 2 * GH * GK + GH * GV + GK_LORA:],
        w_in[:, g0 + 2 * GH * GK + GH * GV:g0 + 2 * GH * GK + GH * GV + GK_LORA], zeros(GKD_W - GK_LORA),
    ], axis=1).astype(BF16)
    row = lambda v: v.reshape(1, -1)
    mu_p, mu_n = p["mu_prev"], p["mu_next"]
    pvec = jnp.concatenate([
        row(p["w0_f"]), row(p["w0_b"]), row(p["a0_f"]), row(p["a0_b"]),
        row(p["k_k"]), row(p["k_a"]), row(p["r_k"]), row(p["lnx_w"]), row(p["lnx_b"]),
        jnp.zeros((PV_ROWS - 9, RW), F32)], axis=0)
    pad_mu = lambda v: jnp.pad(v, (0, C_QK - R_COLS))
    mu = jnp.stack([pad_mu(mu_p), pad_mu(mu_n)], axis=0)
    nhp = RW // LANES

    def lora_pair(wf, wb):
        wf = wf.reshape(-1, nhp, LANES).transpose(1, 0, 2)
        wb = wb.reshape(-1, nhp, LANES).transpose(1, 0, 2)
        z = jnp.zeros_like(wf)
        return jnp.stack([jnp.concatenate([wf, z], axis=1), jnp.concatenate([z, wb], axis=1)],
                         axis=1).astype(BF16)

    ww = lora_pair(p["w2_f"], p["w2_b"])
    wa = lora_pair(p["a2_f"], p["a2_b"])
    wg = jnp.pad(p["g2"], ((0, 2 * LANES - GATE_LORA), (0, 0)))
    wg = wg.reshape(2 * LANES, nhp, LANES).transpose(1, 0, 2).astype(BF16)
    npair = GH // 2

    def gk_pair(w):
        w = jnp.pad(w, ((0, GKD_W - GK_LORA), (0, 0)))
        return w.reshape(GKD_W, npair, LANES).transpose(1, 0, 2)

    wgk = jnp.stack([gk_pair(p["gk2_f"]), gk_pair(p["gk2_b"])], axis=1).astype(BF16)
    gkb = jnp.stack([p["gkb_f"].reshape(npair, 1, LANES), p["gkb_b"].reshape(npair, 1, LANES)], axis=1)
    return dict(
        w_perm=w_perm, pvec=pvec, mu=mu, ww=ww, wa=wa, wg=wg, wgk=wgk, gkb=gkb,
        nw=row(p["gla_norm_w"]),
        g_mix_pre=row(p["g_mix_pre"]), g_mix_post=row(p["g_mix_post"]),
        g_x_pre=row(p["g_x_pre"]), g_x_post=row(p["g_x_post"]), g_mem=row(p["g_mem"]),
        g_ffn_pre=row(p["g_ffn_pre"]), g_ffn_post=row(p["g_ffn_post"]),
        w_out=p["w_out"].astype(BF16), wq=p["wq_x"].astype(BF16), wkv=p["wkv_x"].astype(BF16),
        wo=p["wo_x"].astype(BF16), w1=p["w_ff1"].astype(BF16), w2=p["w_ff2"].astype(BF16))


def _pick(n, pref):
    t = pref
    while n % t:
        t //= 2
    return t


def _trunk(x, mem, pk):
    b, t, _ = x.shape
    n = b * t
    tm = _pick(n, 512)
    x2d = x.reshape(n, D_MODEL)
    rkv, lora, qk, gv, gg, gkd = _inproj(x2d, pk["g_mix_pre"], pk["w_perm"], pk["mu"], _pick(t, 512), t)
    r3 = lambda a: a.reshape(b, t, a.shape[-1])
    rw = _rwkv(r3(rkv), r3(lora), pk["pvec"], pk["ww"], pk["wa"], pk["wg"], _pick(t // 2, 512))
    gl = _gla(r3(qk), r3(gv), r3(gg), r3(gkd), pk["wgk"], pk["gkb"], pk["nw"], _pick(t, 512))
    nm = mem.shape[0] * mem.shape[1]
    kv = _kvproj(mem.reshape(nm, D_MODEL), pk["g_mem"], pk["wkv"], _pick(nm, 512))
    x2 = _xattn(x, rw, gl, kv.reshape(mem.shape[0], mem.shape[1], 2 * D_MODEL), pk["w_out"],
                pk["g_mix_post"], pk["g_x_pre"], pk["wq"], pk["wo"], pk["g_x_post"], _pick(t, 512))
    y = _ffn(x2.reshape(n, D_MODEL), pk["g_ffn_pre"], pk["w1"], pk["w2"], pk["g_ffn_post"],
             tm, 1024)
    return y.reshape(b, t, D_MODEL)


def kernel(x_prompt, x_sample, mem_prompt, mem_sample, g_mix_pre, w_in, mu_prev, mu_next, w0_f, w2_f, w0_b, w2_b, a0_f, a2_f, a0_b, a2_b, g2, k_k, k_a, r_k, lnx_w, lnx_b, gk2_f, gkb_f, gk2_b, gkb_b, gla_norm_w, w_out, g_mix_post, g_x_pre, g_mem, wq_x, wkv_x, wo_x, g_x_post, g_ffn_pre, w_ff1, w_ff2, g_ffn_post):
    params = dict(
        g_mix_pre=g_mix_pre, w_in=w_in, mu_prev=mu_prev, mu_next=mu_next, w0_f=w0_f, w2_f=w2_f,
        w0_b=w0_b, w2_b=w2_b, a0_f=a0_f, a2_f=a2_f, a0_b=a0_b, a2_b=a2_b, g2=g2, k_k=k_k, k_a=k_a,
        r_k=r_k, lnx_w=lnx_w, lnx_b=lnx_b, gk2_f=gk2_f, gkb_f=gkb_f, gk2_b=gk2_b, gkb_b=gkb_b,
        gla_norm_w=gla_norm_w, w_out=w_out, g_mix_post=g_mix_post, g_x_pre=g_x_pre, g_mem=g_mem,
        wq_x=wq_x, wkv_x=wkv_x, wo_x=wo_x, g_x_post=g_x_post, g_ffn_pre=g_ffn_pre, w_ff1=w_ff1,
        w_ff2=w_ff2, g_ffn_post=g_ffn_post)
    assert w_in.shape[0] == 1, "single-layer stack expected"
    pk = _pack_params({name: w[0] for name, w in params.items()})
    return (_trunk(x_prompt, mem_prompt, pk), _trunk(x_sample, mem_sample, pk))
```

```python
import functools
import math

import jax
import jax.numpy as jnp
from jax import lax
from jax.experimental import pallas as pl
from jax.experimental.pallas import tpu as pltpu

F32 = jnp.float32
BF16 = jnp.bfloat16

D_MODEL = 1024
RW = 512
RN = 64
DECAY_LORA = 64
AAA_LORA = 64
GATE_LORA = 160
GN_EPS = 64e-5
GH = 4
GV = 128
GK = 64
GK_LORA = 16
GATE_NORM = 16.0
CHUNK = 64
GLA_EPS = 1e-5
R_COLS = 3 * RW + 2 * DECAY_LORA + 2 * AAA_LORA + GATE_LORA
N_MEM = 256
X_HEADS = 4
X_HD = D_MODEL // X_HEADS
D_FF = 4 * D_MODEL
NORM_EPS = 1e-6

LANES = 128
CUM_ROWS = 2 * CHUNK
HALO = 16
SHIFT_COLS = 512
LORA_W = 512
GKD_W = 128
C_RKV, C_LORA, C_QK, C_GV, C_GG, C_GKD = 0, 1536, 2048, 2560, 3072, 3584
N_PROJ = 3712
VMEM_LIMIT = 56 * 1024 * 1024


def _cparams(sem):
    return pltpu.CompilerParams(dimension_semantics=sem, vmem_limit_bytes=VMEM_LIMIT)


def _mm(a, b):
    return jnp.dot(a.astype(BF16), b.astype(BF16), preferred_element_type=F32)


def _mm_nt(a, b):
    return lax.dot_general(a.astype(BF16), b.astype(BF16), (((1,), (1,)), ((), ())),
                           preferred_element_type=F32)


def _mm_tn(a, b):
    return lax.dot_general(a.astype(BF16), b.astype(BF16), (((0,), (0,)), ((), ())),
                           preferred_element_type=F32)


def _mm_lhs01(m01, x, terms):
    acc = None
    rem = x
    for i in range(terms):
        piece = rem.astype(BF16)
        part = jnp.dot(m01, piece, preferred_element_type=F32)
        acc = part if acc is None else acc + part
        if i + 1 < terms:
            rem = rem - piece.astype(F32)
    return acc


def _rms(x, g, eps):
    return x * lax.rsqrt(jnp.mean(x * x, axis=-1, keepdims=True) + eps) * g


def _sigmoid(x):
    return 1.0 / (1.0 + jnp.exp(-x))


def _iota2(shape, dim):
    return lax.broadcasted_iota(jnp.int32, shape, dim)


def _chunk_tri(n, reverse):
    t = _iota2((n, n), 0)
    s = _iota2((n, n), 1)
    same = (t >> 6) == (s >> 6)
    tri = (s >= t) if reverse else (s <= t)
    return jnp.where(same & tri, 1.0, 0.0).astype(BF16)


def _chunk_cumsum(x, reverse, terms):
    tri = _chunk_tri(CUM_ROWS, reverse)
    parts = [_mm_lhs01(tri, x[i:i + CUM_ROWS], terms) for i in range(0, x.shape[0], CUM_ROWS)]
    return parts[0] if len(parts) == 1 else jnp.concatenate(parts, axis=0)


def _inproj_body(x_ref, xp_ref, xn_ref, g_ref, w_ref, mu_ref, rkv_ref, lora_ref, qk_ref, gv_ref, gg_ref,
                 gkd_ref, *, tiles_per_seq):
    tm = x_ref.shape[0]
    i = pl.program_id(0)
    g = g_ref[...]
    h = _rms(x_ref[...], g, NORM_EPS).astype(BF16)
    h_prev = _rms(xp_ref[...], g, NORM_EPS).astype(BF16)
    h_next = _rms(xn_ref[...], g, NORM_EPS).astype(BF16)
    h_ext = jnp.concatenate([h_prev, h, h_next], axis=0)
    pos = i % tiles_per_seq
    has_prev = pos > 0
    has_next = pos < tiles_per_seq - 1
    rid = _iota2((tm, SHIFT_COLS), 0)

    def proj(lo, hi):
        return jnp.dot(h, w_ref[:, lo:hi], preferred_element_type=F32)

    for lo in range(C_RKV, C_QK, SHIFT_COLS):
        p_ext = jnp.dot(h_ext, w_ref[:, lo:lo + SHIFT_COLS], preferred_element_type=F32)
        p = p_ext[HALO:HALO + tm, :]
        p_before = p_ext[HALO - 1:HALO, :]
        p_after = p_ext[HALO + tm:HALO + tm + 1, :]
        prev = jnp.where(rid == 0, jnp.where(has_prev, p_before, 0.0), pltpu.roll(p, 1, 0))
        nxt = jnp.where(rid == tm - 1, jnp.where(has_next, p_after, 0.0), pltpu.roll(p, tm - 1, 0))
        mu = mu_ref[:, lo:lo + SHIFT_COLS]
        shifted = p + mu[0:1, :] * (prev - p) + mu[1:2, :] * (nxt - p)
        if lo < C_LORA:
            rkv_ref[:, lo:lo + SHIFT_COLS] = shifted
        else:
            lora_ref[:, lo - C_LORA:lo - C_LORA + SHIFT_COLS] = shifted
    qk_ref[...] = proj(C_QK, C_GV)
    gv_ref[...] = proj(C_GV, C_GG)
    gg_ref[...] = proj(C_GG, C_GKD)
    gkd_ref[...] = proj(C_GKD, N_PROJ)


def _inproj(x2d, g, w_perm, mu, tm, seq_len):
    n = x2d.shape[0]
    assert seq_len % tm == 0 and tm % HALO == 0
    widths = (C_LORA - C_RKV, C_QK - C_LORA, C_GV - C_QK, C_GG - C_GV, C_GKD - C_GG, N_PROJ - C_GKD)
    hb = tm // HALO
    last = n // HALO - 1
    body = functools.partial(_inproj_body, tiles_per_seq=seq_len // tm)
    return pl.pallas_call(
        body,
        grid=(n // tm,),
        in_specs=[pl.BlockSpec((tm, D_MODEL), lambda i: (i, 0)),
                  pl.BlockSpec((HALO, D_MODEL), lambda i: (jnp.maximum(i * hb - 1, 0), 0)),
                  pl.BlockSpec((HALO, D_MODEL), lambda i: (jnp.minimum((i + 1) * hb, last), 0)),
                  pl.BlockSpec((1, D_MODEL), lambda i: (0, 0)),
                  pl.BlockSpec((D_MODEL, N_PROJ), lambda i: (0, 0)),
                  pl.BlockSpec((2, C_QK), lambda i: (0, 0))],
        out_specs=[pl.BlockSpec((tm, w), lambda i: (i, 0)) for w in widths],
        out_shape=[jax.ShapeDtypeStruct((n, w), F32) for w in widths],
        compiler_params=_cparams(("parallel",)),
        name="inproj",
    )(x2d, x2d, x2d, g, w_perm, mu)


PV_W0_F, PV_W0_B, PV_A0_F, PV_A0_B, PV_KK, PV_KA, PV_RK, PV_LNW, PV_LNB = range(9)
PV_ROWS = 16


def _rwkv_body(r_ref, k_ref, v_ref, lo_ref, pv_ref, ww_ref, wa_ref, wg_ref, out_ref,
               y_scr, bon_scr, s_scr, *, seq_len, blk):
    C = CHUNK
    nc = blk // C
    nblk = seq_len // blk
    half = nblk // 2
    pv = pv_ref[...]
    prow = lambda i: pv[i:i + 1, :]

    lane = _iota2((1, LANES), 1)
    head_mask = [lane < RN, lane >= RN]
    li = _iota2((LANES, LANES), 0)
    lj = _iota2((LANES, LANES), 1)
    same_head = (li >> 6) == (lj >> 6)
    seg01 = jnp.where(same_head, 1.0, 0.0).astype(BF16)
    row_c = _iota2((C, LANES), 0)
    col_c = _iota2((C, LANES), 1) & (C - 1)
    eye_p = jnp.where(row_c == col_c, 1.0, 0.0)
    zeros_cl = jnp.zeros((C, LANES), F32)

    def seg_sum(x):
        return _mm(x, seg01)

    def block_inputs(t0, reverse):
        d = 1 if reverse else 0
        rows = pl.ds(t0, blk)
        r, k, v, lo = r_ref[0, rows, :], k_ref[0, rows, :], v_ref[0, rows, :], lo_ref[0, rows, :]
        zw = _mm(jnp.tanh(lo[:, 0:LANES]), ww_ref[0, d])
        za = _mm(lo[:, LANES:2 * LANES], wa_ref[0, d])
        lw = -_sigmoid(prow(PV_W0_F + d) + zw) * math.exp(-0.5)
        a = _sigmoid(prow(PV_A0_F + d) + za)
        kd = k * (1.0 + (a - 1.0) * prow(PV_KA))
        kk = k * prow(PV_KK)
        kk = kk * lax.rsqrt(seg_sum(kk * kk) + 1e-12)
        bonus = seg_sum(r * kd * prow(PV_RK)) * v
        g = _chunk_cumsum(lw, reverse, 2)
        return dict(r=r, v=v, kd=kd, kk=kk, a=a, lw=lw, g=g, bonus=bonus, lo=lo)

    def bd(x):
        return jnp.concatenate([jnp.where(head_mask[0], x, 0.0), jnp.where(head_mask[1], x, 0.0)], axis=0)

    def block_terms(ctxs):
        tri_s, tri_i, ch, chains = {}, {}, {}, []
        for d, q, order, reverse in ctxs:
            tri_s[d] = (col_c > row_c) if reverse else (col_c < row_c)
            tri_i[d] = (col_c >= row_c) if reverse else (col_c <= row_c)
            for c in order:
                sl = slice(c * C, (c + 1) * C)
                g, lw, kk = q["g"][sl], q["lw"][sl], q["kk"][sl]
                r, v, kd, a = q["r"][sl], q["v"][sl], q["kd"][sl], q["a"][sl]
                g_tot = g[0:1, :] if reverse else g[C - 1:C, :]
                eng = jnp.exp(-g)
                beta = kk * a
                dec = jnp.exp(g_tot - g)
                ch[d, c] = dict(v=v, rt=r * jnp.exp(g), at=-kk * jnp.exp(g - lw), bt=beta * eng, kt=kd * eng,
                                bh=beta * dec, kh=kd * dec, gam=jnp.exp(g_tot))
                chains.append((d, c))
        a_ab, a_ak, a_rbk = {}, {}, {}
        for k in chains:
            e = ch[k]
            gm = _mm_nt(jnp.concatenate([e["at"], e["rt"]], axis=0),
                        jnp.concatenate([bd(e["bt"]), bd(e["kt"])], axis=0))
            a_ab[k] = jnp.where(tri_s[k[0]], gm[:C, :LANES], 0.0)
            a_ak[k] = jnp.where(tri_s[k[0]], gm[:C, LANES:], 0.0)
            a_rbk[k] = jnp.concatenate([jnp.where(tri_i[k[0]], gm[C:, :LANES], 0.0),
                                        jnp.where(tri_i[k[0]], gm[C:, LANES:], 0.0)], axis=1)
        tm = {k: eye_p + jnp.where((row_c >> 1) == (col_c >> 1), a_ab[k], 0.0) for k in chains}
        for lvl in range(1, 6):
            same = (row_c >> (lvl + 1)) == (col_c >> (lvl + 1))
            lower = (((row_c >> lvl) & 1) == 1) & (((col_c >> lvl) & 1) == 0)
            upper = (((row_c >> lvl) & 1) == 0) & (((col_c >> lvl) & 1) == 1)
            off = {d: same & (upper if reverse else lower) for d, _, _, reverse in ctxs}
            left = {k: _mm(tm[k], bd(jnp.where(off[k[0]], a_ab[k], 0.0))) for k in chains}
            tm = {k: tm[k] + _mm(left[k], bd(tm[k])) for k in chains}
        akv = {k: _mm(a_ak[k], bd(ch[k]["v"])) for k in chains}
        wu = {k: _mm(tm[k], jnp.concatenate([bd(ch[k]["at"]), bd(akv[k])], axis=1)) for k in chains}
        def chunk_terms(k):
            e = ch[k]
            rhs = jnp.concatenate(
                [jnp.concatenate([bd(wu[k][:, :LANES]), bd(wu[k][:, LANES:])], axis=1),
                 jnp.concatenate([jnp.zeros((LANES, LANES), F32), bd(e["v"])], axis=1)], axis=0)
            qy = _mm(a_rbk[k], rhs)
            lhs = jnp.concatenate([wu[k], jnp.concatenate([zeros_cl, e["v"]], axis=1)], axis=0)
            pn = _mm_tn(lhs, jnp.concatenate([e["bh"], e["kh"]], axis=0))
            p_off = jnp.where(same_head, pn[:LANES], 0.0).astype(BF16)
            n0t = jnp.where(same_head, pn[LANES:], 0.0)
            return e["rt"] + qy[:, :LANES], qy[:, LANES:], e["gam"], p_off, n0t

        return chunk_terms

    def run_pair(tf, tb):
        qf = block_inputs(tf, False)
        qb = block_inputs(tb, True)
        ctxs = [(0, qf, list(range(nc)), False), (1, qb, list(range(nc - 1, -1, -1)), True)]
        chunk_terms = block_terms(ctxs)
        s = [s_scr[0], s_scr[1]]
        ys = [[None] * nc, [None] * nc]
        terms = {(d, order[0]): chunk_terms((d, order[0])) for d, _, order, _ in ctxs}
        for step in range(nc):
            if step + 1 < nc:
                for d, _, order, _ in ctxs:
                    terms[d, order[step + 1]] = chunk_terms((d, order[step + 1]))
            for d, _, order, _ in ctxs:
                qh, y0, gam, p_off, n0t = terms[d, order[step]]
                sb = s[d].astype(BF16)
                ys[d][order[step]] = _mm_nt(qh, sb) + y0
                s[d] = (s[d] * gam + n0t) + jnp.dot(sb, p_off, preferred_element_type=F32)
        s_scr[0] = s[0]
        s_scr[1] = s[1]
        return qf, jnp.concatenate(ys[0], axis=0), qb, jnp.concatenate(ys[1], axis=0)

    def finish(q, y, bonus, t0):
        mean = seg_sum(y) * (1.0 / RN)
        dy = y - mean
        var = seg_sum(dy * dy) * (1.0 / RN)
        gn = dy * lax.rsqrt(var + GN_EPS) * prow(PV_LNW) + prow(PV_LNB)
        gate = _mm(_sigmoid(q["lo"][:, 2 * LANES:]), wg_ref[0])
        out_ref[0, pl.ds(t0, blk), :] = ((gn + bonus) * gate).astype(out_ref.dtype)

    s_scr[...] = jnp.zeros(s_scr.shape, F32)

    def block_starts(i):
        return pl.multiple_of(i * blk, blk), pl.multiple_of((nblk - 1 - i) * blk, blk)

    def first_half(i, carry):
        tf, tb = block_starts(i)
        qf, yf, qb, yb = run_pair(tf, tb)
        y_scr[pl.ds(tf, blk), :] = yf
        bon_scr[pl.ds(tf, blk), :] = qf["bonus"]
        y_scr[pl.ds(tb, blk), :] = yb
        bon_scr[pl.ds(tb, blk), :] = qb["bonus"]
        return carry

    def second_half(i, carry):
        tf, tb = block_starts(i)
        qf, yf, qb, yb = run_pair(tf, tb)
        finish(qf, yf + y_scr[pl.ds(tf, blk), :], qf["bonus"] + bon_scr[pl.ds(tf, blk), :], tf)
        finish(qb, yb + y_scr[pl.ds(tb, blk), :], qb["bonus"] + bon_scr[pl.ds(tb, blk), :], tb)
        return carry

    lax.fori_loop(0, half, first_half, 0)
    lax.fori_loop(half, nblk, second_half, 0)


def _rwkv(rkv, lora, pvec, ww, wa, wg, blk):
    b, t, _ = rkv.shape
    assert t % (2 * blk) == 0 and blk % CUM_ROWS == 0
    nhp = RW // LANES
    col = lambda off: pl.BlockSpec((1, t, LANES), lambda i, j, off=off: (i, 0, off + j))
    body = functools.partial(_rwkv_body, seq_len=t, blk=blk)
    return pl.pallas_call(
        body,
        grid=(b, nhp),
        in_specs=[col(0), col(nhp), col(2 * nhp),
                  pl.BlockSpec((1, t, LORA_W), lambda i, j: (i, 0, 0)),
                  pl.BlockSpec((PV_ROWS, LANES), lambda i, j: (0, j)),
                  pl.BlockSpec((1, 2, LANES, LANES), lambda i, j: (j, 0, 0, 0)),
                  pl.BlockSpec((1, 2, LANES, LANES), lambda i, j: (j, 0, 0, 0)),
                  pl.BlockSpec((1, 2 * LANES, LANES), lambda i, j: (j, 0, 0))],
        out_specs=pl.BlockSpec((1, t, LANES), lambda i, j: (i, 0, j)),
        out_shape=jax.ShapeDtypeStruct((b, t, RW), BF16),
        scratch_shapes=[pltpu.VMEM((t, LANES), F32), pltpu.VMEM((t, LANES), F32),
                        pltpu.VMEM((2, LANES, LANES), F32)],
        compiler_params=_cparams(("parallel", "parallel")),
        name="rwkv",
    )(rkv, rkv, rkv, lora, pvec, ww, wa, wg)


def _gla_body(q_ref, k_ref, v_ref, gg_ref, gkd_ref, wgk_ref, gkb_ref, nw_ref, out_ref,
              o_scr, s_scr, *, seq_len, blk):
    C = CHUNK
    nc = blk // C
    nblk = seq_len // blk
    half = nblk // 2
    lane = _iota2((1, LANES), 1)
    head_mask = [lane < GK, lane >= GK]
    si = _iota2((2 * GV, LANES), 0)
    sj = _iota2((2 * GV, LANES), 1)
    same_head = (si >> 7) == (sj >> 6)
    row_c = _iota2((C, LANES), 0)
    col_c = _iota2((C, LANES), 1) & (C - 1)
    zeros_cv = jnp.zeros((C, GV), F32)

    def run_pair(tf, tb):
        ctxs = [dict(d=0, rows=pl.ds(tf, blk), reverse=False, order=list(range(nc))),
                dict(d=1, rows=pl.ds(tb, blk), reverse=True, order=list(range(nc - 1, -1, -1)))]
        for cx in ctxs:
            cx["z"] = _mm(gkd_ref[0, cx["rows"], :], wgk_ref[0, cx["d"]]) + gkb_ref[0, cx["d"]]
        for cx in ctxs:
            z = cx["z"]
            lg = (jnp.minimum(z, 0.0) - jnp.log1p(jnp.exp(-jnp.abs(z)))) * (1.0 / GATE_NORM)
            cx["bcum"] = _chunk_cumsum(lg, cx["reverse"], 3)
        ch, chains, tri = {}, [], {}
        for cx in ctxs:
            d, reverse = cx["d"], cx["reverse"]
            tri[d] = (col_c >= row_c) if reverse else (col_c <= row_c)
            q = q_ref[0, cx["rows"], :] * (GK ** -0.5)
            k = k_ref[0, cx["rows"], :]
            v = v_ref[0, cx["rows"], :]
            for c in cx["order"]:
                sl = slice(c * C, (c + 1) * C)
                bc = cx["bcum"][sl]
                b_tot = bc[0:1, :] if reverse else bc[C - 1:C, :]
                ch[d, c] = dict(q_in=q[sl] * jnp.exp(bc), k_in=k[sl] * jnp.exp(-bc),
                                k_dec=k[sl] * jnp.exp(b_tot - bc), v=v[sl], gam=jnp.exp(b_tot))
                chains.append((d, c))
        att, intra, ds = {}, {}, {}
        for kk in chains:
            k_in = ch[kk]["k_in"]
            k_bd = jnp.concatenate([jnp.where(head_mask[0], k_in, 0.0), jnp.where(head_mask[1], k_in, 0.0)], axis=0)
            att[kk] = jnp.where(tri[kk[0]], _mm_nt(ch[kk]["q_in"], k_bd), 0.0)
        for kk in chains:
            v = ch[kk]["v"]
            v_bd = jnp.concatenate([jnp.concatenate([v[:, :GV], zeros_cv], axis=1),
                                    jnp.concatenate([zeros_cv, v[:, GV:]], axis=1)], axis=0)
            intra[kk] = _mm(att[kk], v_bd)
        for kk in chains:
            ds[kk] = jnp.where(same_head, _mm_tn(ch[kk]["v"], ch[kk]["k_dec"]), 0.0)
        s_in = {}
        for cx in ctxs:
            d = cx["d"]
            s = s_scr[d]
            for c in cx["order"]:
                s_in[d, c] = s
                s = s * ch[d, c]["gam"] + ds[d, c]
            s_scr[d] = s
        outs = {kk: _mm_nt(ch[kk]["q_in"], s_in[kk]) + intra[kk] for kk in chains}
        return (jnp.concatenate([outs[0, c] for c in range(nc)], axis=0),
                jnp.concatenate([outs[1, c] for c in range(nc)], axis=0))

    def finish(o, t0):
        gg = gg_ref[0, pl.ds(t0, blk), :]
        res = []
        for h in range(2):
            oh = o[:, h * GV:(h + 1) * GV]
            oh = oh * lax.rsqrt(jnp.mean(oh * oh, axis=-1, keepdims=True) + GLA_EPS) * nw_ref[...]
            gh = gg[:, h * GV:(h + 1) * GV]
            res.append(oh * (gh * _sigmoid(gh)))
        out_ref[0, pl.ds(t0, blk), :] = jnp.concatenate(res, axis=1).astype(out_ref.dtype)

    s_scr[...] = jnp.zeros(s_scr.shape, F32)

    def block_starts(i):
        return pl.multiple_of(i * blk, blk), pl.multiple_of((nblk - 1 - i) * blk, blk)

    def first_half(i, carry):
        tf, tb = block_starts(i)
        of, ob = run_pair(tf, tb)
        o_scr[pl.ds(tf, blk), :] = of
        o_scr[pl.ds(tb, blk), :] = ob
        return carry

    def second_half(i, carry):
        tf, tb = block_starts(i)
        of, ob = run_pair(tf, tb)
        finish(of + o_scr[pl.ds(tf, blk), :], tf)
        finish(ob + o_scr[pl.ds(tb, blk), :], tb)
        return carry

    lax.fori_loop(0, half, first_half, 0)
    lax.fori_loop(half, nblk, second_half, 0)


def _gla(qk, gv, gg, gkd, wgk, gkb, nw, blk):
    b, t, _ = qk.shape
    assert t % (2 * blk) == 0 and blk % CUM_ROWS == 0
    npair = GH // 2
    body = functools.partial(_gla_body, seq_len=t, blk=blk)
    return pl.pallas_call(
        body,
        grid=(b, npair),
        in_specs=[pl.BlockSpec((1, t, LANES), lambda i, j: (i, 0, j)),
                  pl.BlockSpec((1, t, LANES), lambda i, j: (i, 0, npair + j)),
                  pl.BlockSpec((1, t, 2 * GV), lambda i, j: (i, 0, j)),
                  pl.BlockSpec((1, t, 2 * GV), lambda i, j: (i, 0, j)),
                  pl.BlockSpec((1, t, GKD_W), lambda i, j: (i, 0, 0)),
                  pl.BlockSpec((1, 2, GKD_W, LANES), lambda i, j: (j, 0, 0, 0)),
                  pl.BlockSpec((1, 2, 1, LANES), lambda i, j: (j, 0, 0, 0)),
                  pl.BlockSpec((1, GV), lambda i, j: (0, 0))],
        out_specs=pl.BlockSpec((1, t, 2 * GV), lambda i, j: (i, 0, j)),
        out_shape=jax.ShapeDtypeStruct((b, t, GH * GV), BF16),
        scratch_shapes=[pltpu.VMEM((t, 2 * GV), F32), pltpu.VMEM((2, 2 * GV, LANES), F32)],
        compiler_params=_cparams(("parallel", "parallel")),
        name="gla",
    )(qk, qk, gv, gg, gkd, wgk, gkb, nw)


def _kvproj_body(m_ref, g_ref, w_ref, kv_ref):
    h = _rms(m_ref[...], g_ref[...], NORM_EPS).astype(BF16)
    kv_ref[...] = jnp.dot(h, w_ref[...], preferred_element_type=F32).astype(kv_ref.dtype)


def _kvproj(mem2d, g_mem, wkv, tm):
    n = mem2d.shape[0]
    return pl.pallas_call(
        _kvproj_body,
        grid=(n // tm,),
        in_specs=[pl.BlockSpec((tm, D_MODEL), lambda i: (i, 0)),
                  pl.BlockSpec((1, D_MODEL), lambda i: (0, 0)),
                  pl.BlockSpec((D_MODEL, 2 * D_MODEL), lambda i: (0, 0))],
        out_specs=pl.BlockSpec((tm, 2 * D_MODEL), lambda i: (i, 0)),
        out_shape=jax.ShapeDtypeStruct((n, 2 * D_MODEL), BF16),
        compiler_params=_cparams(("parallel",)),
        name="kvproj",
    )(mem2d, g_mem, wkv)


def _xattn_body(x_ref, rw_ref, gl_ref, kv_ref, wout_ref, gmix_ref, gpre_ref, wq_ref, wo_ref, gpost_ref, o_ref):
    mixed = (jnp.dot(rw_ref[0], wout_ref[0:RW, :], preferred_element_type=F32)
             + jnp.dot(gl_ref[0], wout_ref[RW:, :], preferred_element_type=F32))
    x = x_ref[0] + _rms(mixed, gmix_ref[...], NORM_EPS)
    h = _rms(x, gpre_ref[...], NORM_EPS).astype(BF16)
    q = jnp.dot(h, wq_ref[...], preferred_element_type=F32).astype(BF16)
    heads = range(X_HEADS)
    scores = [lax.dot_general(q[:, hd * X_HD:(hd + 1) * X_HD], kv_ref[0, :, hd * X_HD:(hd + 1) * X_HD],
                              (((1,), (1,)), ((), ())), preferred_element_type=F32) * (X_HD ** -0.5)
              for hd in heads]
    probs = []
    for s in scores:
        e = jnp.exp(s - jnp.max(s, axis=-1, keepdims=True))
        probs.append((e * (1.0 / jnp.sum(e, axis=-1, keepdims=True))).astype(BF16))
    outs = [jnp.dot(probs[hd], kv_ref[0, :, D_MODEL + hd * X_HD:D_MODEL + (hd + 1) * X_HD],
                    preferred_element_type=F32).astype(BF16) for hd in heads]
    att = jnp.dot(jnp.concatenate(outs, axis=1), wo_ref[...], preferred_element_type=F32)
    o_ref[0] = x + _rms(att, gpost_ref[...], NORM_EPS)


def _xattn(x3d, rw3d, gl3d, kv3d, w_out, g_mix_post, g_pre, wq, wo, g_post, tm):
    b, t, _ = x3d.shape
    row = lambda w: pl.BlockSpec((1, tm, w), lambda i, j: (i, j, 0))
    const = lambda r, c: pl.BlockSpec((r, c), lambda i, j: (0, 0))
    return pl.pallas_call(
        _xattn_body,
        grid=(b, t // tm),
        in_specs=[row(D_MODEL), row(RW), row(GH * GV),
                  pl.BlockSpec((1, N_MEM, 2 * D_MODEL), lambda i, j: (i, 0, 0)),
                  const(D_MODEL, D_MODEL), const(1, D_MODEL), const(1, D_MODEL),
                  const(D_MODEL, D_MODEL), const(D_MODEL, D_MODEL), const(1, D_MODEL)],
        out_specs=row(D_MODEL),
        out_shape=jax.ShapeDtypeStruct((b, t, D_MODEL), F32),
        compiler_params=_cparams(("parallel", "parallel")),
        name="xattn",
    )(x3d, rw3d, gl3d, kv3d, w_out, g_mix_post, g_pre, wq, wo, g_post)


def _ffn_body(x_ref, gpre_ref, w1_ref, w2_ref, gpost_ref, o_ref, h_scr, acc_scr):
    j = pl.program_id(1)

    @pl.when(j == 0)
    def _():
        h_scr[...] = _rms(x_ref[...], gpre_ref[...], NORM_EPS).astype(BF16)
        acc_scr[...] = jnp.zeros_like(acc_scr)

    a = jnp.dot(h_scr[...], w1_ref[...], preferred_element_type=F32)
    a = jnp.square(jnp.maximum(a, 0.0)).astype(BF16)
    acc_scr[...] += jnp.dot(a, w2_ref[...], preferred_element_type=F32)

    @pl.when(j == pl.num_programs(1) - 1)
    def _():
        o_ref[...] = x_ref[...] + _rms(acc_scr[...], gpost_ref[...], NORM_EPS)


def _ffn(x2d, g_pre, w1, w2, g_post, tm, tf):
    n = x2d.shape[0]
    return pl.pallas_call(
        _ffn_body,
        grid=(n // tm, D_FF // tf),
        in_specs=[pl.BlockSpec((tm, D_MODEL), lambda i, j: (i, 0)),
                  pl.BlockSpec((1, D_MODEL), lambda i, j: (0, 0)),
                  pl.BlockSpec((D_MODEL, tf), lambda i, j: (0, j)),
                  pl.BlockSpec((tf, D_MODEL), lambda i, j: (j, 0)),
                  pl.BlockSpec((1, D_MODEL), lambda i, j: (0, 0))],
        out_specs=pl.BlockSpec((tm, D_MODEL), lambda i, j: (i, 0)),
        out_shape=jax.ShapeDtypeStruct((n, D_MODEL), F32),
        scratch_shapes=[pltpu.VMEM((tm, D_MODEL), BF16), pltpu.VMEM((tm, D_MODEL), F32)],
        compiler_params=_cparams(("parallel", "arbitrary")),
        name="ffn",
    )(x2d, g_pre, w1, w2, g_post)


def _pack_params(p):
    w_in = p["w_in"]
    g0 = R_COLS
    zeros = lambda n: jnp.zeros((D_MODEL, n), w_in.dtype)
    w_perm = jnp.concatenate([
        w_in[:, :R_COLS], zeros(LORA_W - (R_COLS - 3 * RW)),
        w_in[:, g0:g0 + 2 * GH * GK],
        w_in[:, g0 + 2 * GH * GK:g0 + 2 * GH * GK + GH * GV],
        w_in[:, g0 + 2 * GH * GK + GH * GV + GK_LORA:],
        w_in[:, g0 + 2 * GH * GK + GH * GV:g0 + 2 * GH * GK + GH * GV + GK_LORA], zeros(GKD_W - GK_LORA),
    ], axis=1).astype(BF16)
    row = lambda v: v.reshape(1, -1)
    mu_p, mu_n = p["mu_prev"], p["mu_next"]
    pvec = jnp.concatenate([
        row(p["w0_f"]), row(p["w0_b"]), row(p["a0_f"]), row(p["a0_b"]),
        row(p["k_k"]), row(p["k_a"]), row(p["r_k"]), row(p["lnx_w"]), row(p["lnx_b"]),
        jnp.zeros((PV_ROWS - 9, RW), F32)], axis=0)
    pad_mu = lambda v: jnp.pad(v, (0, C_QK - R_COLS))
    mu = jnp.stack([pad_mu(mu_p), pad_mu(mu_n)], axis=0)
    nhp = RW // LANES

    def lora_pair(wf, wb):
        wf = wf.reshape(-1, nhp, LANES).transpose(1, 0, 2)
        wb = wb.reshape(-1, nhp, LANES).transpose(1, 0, 2)
        z = jnp.zeros_like(wf)
        return jnp.stack([jnp.concatenate([wf, z], axis=1), jnp.concatenate([z, wb], axis=1)],
                         axis=1).astype(BF16)

    ww = lora_pair(p["w2_f"], p["w2_b"])
    wa = lora_pair(p["a2_f"], p["a2_b"])
    wg = jnp.pad(p["g2"], ((0, 2 * LANES - GATE_LORA), (0, 0)))
    wg = wg.reshape(2 * LANES, nhp, LANES).transpose(1, 0, 2).astype(BF16)
    npair = GH // 2

    def gk_pair(w):
        w = jnp.pad(w, ((0, GKD_W - GK_LORA), (0, 0)))
        return w.reshape(GKD_W, npair, LANES).transpose(1, 0, 2)

    wgk = jnp.stack([gk_pair(p["gk2_f"]), gk_pair(p["gk2_b"])], axis=1).astype(BF16)
    gkb = jnp.stack([p["gkb_f"].reshape(npair, 1, LANES), p["gkb_b"].reshape(npair, 1, LANES)], axis=1)
    return dict(
        w_perm=w_perm, pvec=pvec, mu=mu, ww=ww, wa=wa, wg=wg, wgk=wgk, gkb=gkb,
        nw=row(p["gla_norm_w"]),
        g_mix_pre=row(p["g_mix_pre"]), g_mix_post=row(p["g_mix_post"]),
        g_x_pre=row(p["g_x_pre"]), g_x_post=row(p["g_x_post"]), g_mem=row(p["g_mem"]),
        g_ffn_pre=row(p["g_ffn_pre"]), g_ffn_post=row(p["g_ffn_post"]),
        w_out=p["w_out"].astype(BF16), wq=p["wq_x"].astype(BF16), wkv=p["wkv_x"].astype(BF16),
        wo=p["wo_x"].astype(BF16), w1=p["w_ff1"].astype(BF16), w2=p["w_ff2"].astype(BF16))


def _pick(n, pref):
    t = pref
    while n % t:
        t //= 2
    return t


def _trunk(x, mem, pk):
    b, t, _ = x.shape
    n = b * t
    tm = _pick(n, 512)
    x2d = x.reshape(n, D_MODEL)
    rkv, lora, qk, gv, gg, gkd = _inproj(x2d, pk["g_mix_pre"], pk["w_perm"], pk["mu"], _pick(t, 512), t)
    r3 = lambda a: a.reshape(b, t, a.shape[-1])
    blk = _pick(t // 2, 512)
    rw = _rwkv(r3(rkv), r3(lora), pk["pvec"], pk["ww"], pk["wa"], pk["wg"], blk)
    gl = _gla(r3(qk), r3(gv), r3(gg), r3(gkd), pk["wgk"], pk["gkb"], pk["nw"], blk)
    nm = mem.shape[0] * mem.shape[1]
    kv = _kvproj(mem.reshape(nm, D_MODEL), pk["g_mem"], pk["wkv"], _pick(nm, 512))
    x2 = _xattn(x, rw, gl, kv.reshape(mem.shape[0], mem.shape[1], 2 * D_MODEL), pk["w_out"],
                pk["g_mix_post"], pk["g_x_pre"], pk["wq"], pk["wo"], pk["g_x_post"], _pick(t, 512))
    y = _ffn(x2.reshape(n, D_MODEL), pk["g_ffn_pre"], pk["w1"], pk["w2"], pk["g_ffn_post"],
             tm, 1024)
    return y.reshape(b, t, D_MODEL)


def kernel(x_prompt, x_sample, mem_prompt, mem_sample, g_mix_pre, w_in, mu_prev, mu_next, w0_f, w2_f, w0_b, w2_b, a0_f, a2_f, a0_b, a2_b, g2, k_k, k_a, r_k, lnx_w, lnx_b, gk2_f, gkb_f, gk2_b, gkb_b, gla_norm_w, w_out, g_mix_post, g_x_pre, g_mem, wq_x, wkv_x, wo_x, g_x_post, g_ffn_pre, w_ff1, w_ff2, g_ffn_post):
    params = dict(
        g_mix_pre=g_mix_pre, w_in=w_in, mu_prev=mu_prev, mu_next=mu_next, w0_f=w0_f, w2_f=w2_f,
        w0_b=w0_b, w2_b=w2_b, a0_f=a0_f, a2_f=a2_f, a0_b=a0_b, a2_b=a2_b, g2=g2, k_k=k_k, k_a=k_a,
        r_k=r_k, lnx_w=lnx_w, lnx_b=lnx_b, gk2_f=gk2_f, gkb_f=gkb_f, gk2_b=gk2_b, gkb_b=gkb_b,
        gla_norm_w=gla_norm_w, w_out=w_out, g_mix_post=g_mix_post, g_x_pre=g_x_pre, g_mem=g_mem,
        wq_x=wq_x, wkv_x=wkv_x, wo_x=wo_x, g_x_post=g_x_post, g_ffn_pre=g_ffn_pre, w_ff1=w_ff1,
        w_ff2=w_ff2, g_ffn_post=g_ffn_post)
    assert w_in.shape[0] == 1, "single-layer stack expected"
    pk = _pack_params({name: w[0] for name, w in params.items()})
    return (_trunk(x_prompt, mem_prompt, pk), _trunk(x_sample, mem_sample, pk))
```

```python
import functools
import math

import jax
import jax.numpy as jnp
from jax import lax
from jax.experimental import pallas as pl
from jax.experimental.pallas import tpu as pltpu

F32 = jnp.float32
BF16 = jnp.bfloat16

D_MODEL = 1024
RW = 512
RN = 64
DECAY_LORA = 64
AAA_LORA = 64
GATE_LORA = 160
GN_EPS = 64e-5
GH = 4
GV = 128
GK = 64
GK_LORA = 16
GATE_NORM = 16.0
CHUNK = 64
GLA_EPS = 1e-5
R_COLS = 3 * RW + 2 * DECAY_LORA + 2 * AAA_LORA + GATE_LORA
N_MEM = 256
X_HEADS = 4
X_HD = D_MODEL // X_HEADS
D_FF = 4 * D_MODEL
NORM_EPS = 1e-6

LANES = 128
CUM_ROWS = 2 * CHUNK
HALO = 16
SHIFT_COLS = 512
LORA_W = 512
GKD_W = 128
C_RKV, C_LORA, C_QK, C_GV, C_GG, C_GKD = 0, 1536, 2048, 2560, 3072, 3584
N_PROJ = 3712
VMEM_LIMIT = 56 * 1024 * 1024


def _cparams(sem):
    return pltpu.CompilerParams(dimension_semantics=sem, vmem_limit_bytes=VMEM_LIMIT)


def _mm(a, b):
    return jnp.dot(a.astype(BF16), b.astype(BF16), preferred_element_type=F32)


def _mm_nt(a, b):
    return lax.dot_general(a.astype(BF16), b.astype(BF16), (((1,), (1,)), ((), ())),
                           preferred_element_type=F32)


def _mm_tn(a, b):
    return lax.dot_general(a.astype(BF16), b.astype(BF16), (((0,), (0,)), ((), ())),
                           preferred_element_type=F32)


def _mm_lhs01(m01, x, terms):
    acc = None
    rem = x
    for i in range(terms):
        piece = rem.astype(BF16)
        part = jnp.dot(m01, piece, preferred_element_type=F32)
        acc = part if acc is None else acc + part
        if i + 1 < terms:
            rem = rem - piece.astype(F32)
    return acc


def _rms(x, g, eps):
    return x * lax.rsqrt(jnp.mean(x * x, axis=-1, keepdims=True) + eps) * g


def _sigmoid(x):
    return 1.0 / (1.0 + jnp.exp(-x))


def _iota2(shape, dim):
    return lax.broadcasted_iota(jnp.int32, shape, dim)


def _chunk_tri(n, reverse):
    t = _iota2((n, n), 0)
    s = _iota2((n, n), 1)
    same = (t >> 6) == (s >> 6)
    tri = (s >= t) if reverse else (s <= t)
    return jnp.where(same & tri, 1.0, 0.0).astype(BF16)


def _chunk_cumsum(x, reverse, terms):
    tri = _chunk_tri(CUM_ROWS, reverse)
    parts = [_mm_lhs01(tri, x[i:i + CUM_ROWS], terms) for i in range(0, x.shape[0], CUM_ROWS)]
    return parts[0] if len(parts) == 1 else jnp.concatenate(parts, axis=0)


def _inproj_body(x_ref, xp_ref, xn_ref, g_ref, w_ref, mu_ref, rkv_ref, lora_ref, qk_ref, gv_ref, gg_ref,
                 gkd_ref, *, tiles_per_seq):
    tm = x_ref.shape[0]
    i = pl.program_id(0)
    g = g_ref[...]
    h = _rms(x_ref[...], g, NORM_EPS).astype(BF16)
    h_prev = _rms(xp_ref[...], g, NORM_EPS).astype(BF16)
    h_next = _rms(xn_ref[...], g, NORM_EPS).astype(BF16)
    h_ext = jnp.concatenate([h_prev, h, h_next], axis=0)
    pos = i % tiles_per_seq
    has_prev = pos > 0
    has_next = pos < tiles_per_seq - 1
    rid = _iota2((tm, SHIFT_COLS), 0)

    def proj(lo, hi):
        return jnp.dot(h, w_ref[:, lo:hi], preferred_element_type=F32)

    for lo in range(C_RKV, C_QK, SHIFT_COLS):
        p_ext = jnp.dot(h_ext, w_ref[:, lo:lo + SHIFT_COLS], preferred_element_type=F32)
        p = p_ext[HALO:HALO + tm, :]
        p_before = p_ext[HALO - 1:HALO, :]
        p_after = p_ext[HALO + tm:HALO + tm + 1, :]
        prev = jnp.where(rid == 0, jnp.where(has_prev, p_before, 0.0), pltpu.roll(p, 1, 0))
        nxt = jnp.where(rid == tm - 1, jnp.where(has_next, p_after, 0.0), pltpu.roll(p, tm - 1, 0))
        mu = mu_ref[:, lo:lo + SHIFT_COLS]
        shifted = p + mu[0:1, :] * (prev - p) + mu[1:2, :] * (nxt - p)
        if lo < C_LORA:
            rkv_ref[:, lo:lo + SHIFT_COLS] = shifted
        else:
            lora_ref[:, lo - C_LORA:lo - C_LORA + SHIFT_COLS] = shifted
    qk_ref[...] = proj(C_QK, C_GV)
    gv_ref[...] = proj(C_GV, C_GG)
    gg_ref[...] = proj(C_GG, C_GKD)
    gkd_ref[...] = proj(C_GKD, N_PROJ)


def _inproj(x2d, g, w_perm, mu, tm, seq_len):
    n = x2d.shape[0]
    assert seq_len % tm == 0 and tm % HALO == 0
    widths = (C_LORA - C_RKV, C_QK - C_LORA, C_GV - C_QK, C_GG - C_GV, C_GKD - C_GG, N_PROJ - C_GKD)
    hb = tm // HALO
    last = n // HALO - 1
    body = functools.partial(_inproj_body, tiles_per_seq=seq_len // tm)
    return pl.pallas_call(
        body,
        grid=(n // tm,),
        in_specs=[pl.BlockSpec((tm, D_MODEL), lambda i: (i, 0)),
                  pl.BlockSpec((HALO, D_MODEL), lambda i: (jnp.maximum(i * hb - 1, 0), 0)),
                  pl.BlockSpec((HALO, D_MODEL), lambda i: (jnp.minimum((i + 1) * hb, last), 0)),
                  pl.BlockSpec((1, D_MODEL), lambda i: (0, 0)),
                  pl.BlockSpec((D_MODEL, N_PROJ), lambda i: (0, 0)),
                  pl.BlockSpec((2, C_QK), lambda i: (0, 0))],
        out_specs=[pl.BlockSpec((tm, w), lambda i: (i, 0)) for w in widths],
        out_shape=[jax.ShapeDtypeStruct((n, w), F32) for w in widths],
        compiler_params=_cparams(("parallel",)),
        name="inproj",
    )(x2d, x2d, x2d, g, w_perm, mu)


PV_W0_F, PV_W0_B, PV_A0_F, PV_A0_B, PV_KK, PV_KA, PV_RK, PV_LNW, PV_LNB = range(9)
PV_ROWS = 16


def _rwkv_body(r_ref, k_ref, v_ref, lo_ref, pv_ref, ww_ref, wa_ref, wg_ref, out_ref,
               y_scr, bon_scr, s_scr, *, seq_len, blk):
    C = CHUNK
    nc = blk // C
    nblk = seq_len // blk
    half = nblk // 2
    pv = pv_ref[...]
    prow = lambda i: pv[i:i + 1, :]

    lane = _iota2((1, LANES), 1)
    head_mask = [lane < RN, lane >= RN]
    li = _iota2((LANES, LANES), 0)
    lj = _iota2((LANES, LANES), 1)
    same_head = (li >> 6) == (lj >> 6)
    seg01 = jnp.where(same_head, 1.0, 0.0).astype(BF16)
    row_c = _iota2((C, LANES), 0)
    col_c = _iota2((C, LANES), 1) & (C - 1)
    eye_p = jnp.where(row_c == col_c, 1.0, 0.0)
    zeros_cl = jnp.zeros((C, LANES), F32)

    def seg_sum(x):
        return _mm(x, seg01)

    def block_inputs(t0, reverse):
        d = 1 if reverse else 0
        rows = pl.ds(t0, blk)
        r, k, v, lo = r_ref[0, rows, :], k_ref[0, rows, :], v_ref[0, rows, :], lo_ref[0, rows, :]
        zw = _mm(jnp.tanh(lo[:, 0:LANES]), ww_ref[0, d])
        za = _mm(lo[:, LANES:2 * LANES], wa_ref[0, d])
        lw = -_sigmoid(prow(PV_W0_F + d) + zw) * math.exp(-0.5)
        a = _sigmoid(prow(PV_A0_F + d) + za)
        kd = k * (1.0 + (a - 1.0) * prow(PV_KA))
        kk = k * prow(PV_KK)
        kk = kk * lax.rsqrt(seg_sum(kk * kk) + 1e-12)
        bonus = seg_sum(r * kd * prow(PV_RK)) * v
        g = _chunk_cumsum(lw, reverse, 2)
        return dict(r=r, v=v, kd=kd, kk=kk, a=a, lw=lw, g=g, bonus=bonus, lo=lo)

    def bd(x):
        return jnp.concatenate([jnp.where(head_mask[0], x, 0.0), jnp.where(head_mask[1], x, 0.0)], axis=0)

    def block_terms(ctxs):
        tri_s, tri_i, ch, chains = {}, {}, {}, []
        for d, q, order, reverse in ctxs:
            tri_s[d] = (col_c > row_c) if reverse else (col_c < row_c)
            tri_i[d] = (col_c >= row_c) if reverse else (col_c <= row_c)
            for c in order:
                sl = slice(c * C, (c + 1) * C)
                g, lw, kk = q["g"][sl], q["lw"][sl], q["kk"][sl]
                r, v, kd, a = q["r"][sl], q["v"][sl], q["kd"][sl], q["a"][sl]
                g_tot = g[0:1, :] if reverse else g[C - 1:C, :]
                eng = jnp.exp(-g)
                beta = kk * a
                dec = jnp.exp(g_tot - g)
                ch[d, c] = dict(v=v, rt=r * jnp.exp(g), at=-kk * jnp.exp(g - lw), bt=beta * eng, kt=kd * eng,
                                bh=beta * dec, kh=kd * dec, gam=jnp.exp(g_tot))
                chains.append((d, c))
        a_ab, a_ak, a_rbk = {}, {}, {}
        for k in chains:
            e = ch[k]
            gm = _mm_nt(jnp.concatenate([e["at"], e["rt"]], axis=0),
                        jnp.concatenate([bd(e["bt"]), bd(e["kt"])], axis=0))
            a_ab[k] = jnp.where(tri_s[k[0]], gm[:C, :LANES], 0.0)
            a_ak[k] = jnp.where(tri_s[k[0]], gm[:C, LANES:], 0.0)
            a_rbk[k] = jnp.concatenate([jnp.where(tri_i[k[0]], gm[C:, :LANES], 0.0),
                                        jnp.where(tri_i[k[0]], gm[C:, LANES:], 0.0)], axis=1)
        tm = {k: eye_p + jnp.where((row_c >> 1) == (col_c >> 1), a_ab[k], 0.0) for k in chains}
        for lvl in range(1, 6):
            same = (row_c >> (lvl + 1)) == (col_c >> (lvl + 1))
            lower = (((row_c >> lvl) & 1) == 1) & (((col_c >> lvl) & 1) == 0)
            upper = (((row_c >> lvl) & 1) == 0) & (((col_c >> lvl) & 1) == 1)
            off = {d: same & (upper if reverse else lower) for d, _, _, reverse in ctxs}
            left = {k: _mm(tm[k], bd(jnp.where(off[k[0]], a_ab[k], 0.0))) for k in chains}
            tm = {k: tm[k] + _mm(left[k], bd(tm[k])) for k in chains}
        akv = {k: _mm(a_ak[k], bd(ch[k]["v"])) for k in chains}
        wu = {k: _mm(tm[k], jnp.concatenate([bd(ch[k]["at"]), bd(akv[k])], axis=1)) for k in chains}

        def chunk_terms(k):
            e = ch[k]
            rhs = jnp.concatenate(
                [jnp.concatenate([bd(wu[k][:, :LANES]), bd(wu[k][:, LANES:])], axis=1),
                 jnp.concatenate([jnp.zeros((LANES, LANES), F32), bd(e["v"])], axis=1)], axis=0)
            qy = _mm(a_rbk[k], rhs)
            lhs = jnp.concatenate([wu[k], jnp.concatenate([zeros_cl, e["v"]], axis=1)], axis=0)
            pn = _mm_tn(lhs, jnp.concatenate([e["bh"], e["kh"]], axis=0))
            p_off = jnp.where(same_head, pn[:LANES], 0.0).astype(BF16)
            n0t = jnp.where(same_head, pn[LANES:], 0.0)
            return e["rt"] + qy[:, :LANES], qy[:, LANES:], e["gam"], p_off, n0t

        return chunk_terms

    def run_pair(tf, tb):
        qf = block_inputs(tf, False)
        qb = block_inputs(tb, True)
        ctxs = [(0, qf, list(range(nc)), False), (1, qb, list(range(nc - 1, -1, -1)), True)]
        chunk_terms = block_terms(ctxs)
        s = [s_scr[0], s_scr[1]]
        ys = [[None] * nc, [None] * nc]
        terms = {(d, order[0]): chunk_terms((d, order[0])) for d, _, order, _ in ctxs}
        for step in range(nc):
            if step + 1 < nc:
                for d, _, order, _ in ctxs:
                    terms[d, order[step + 1]] = chunk_terms((d, order[step + 1]))
            for d, _, order, _ in ctxs:
                qh, y0, gam, p_off, n0t = terms[d, order[step]]
                sb = s[d].astype(BF16)
                ys[d][order[step]] = _mm_nt(qh, sb) + y0
                s[d] = (s[d] * gam + n0t) + jnp.dot(sb, p_off, preferred_element_type=F32)
        s_scr[0] = s[0]
        s_scr[1] = s[1]
        return qf, jnp.concatenate(ys[0], axis=0), qb, jnp.concatenate(ys[1], axis=0)

    def finish(q, y, bonus, t0):
        mean = seg_sum(y) * (1.0 / RN)
        dy = y - mean
        var = seg_sum(dy * dy) * (1.0 / RN)
        gn = dy * lax.rsqrt(var + GN_EPS) * prow(PV_LNW) + prow(PV_LNB)
        gate = _mm(_sigmoid(q["lo"][:, 2 * LANES:]), wg_ref[0])
        out_ref[0, pl.ds(t0, blk), :] = ((gn + bonus) * gate).astype(out_ref.dtype)

    s_scr[...] = jnp.zeros(s_scr.shape, F32)

    def block_starts(i):
        return pl.multiple_of(i * blk, blk), pl.multiple_of((nblk - 1 - i) * blk, blk)

    def first_half(i, carry):
        tf, tb = block_starts(i)
        qf, yf, qb, yb = run_pair(tf, tb)
        y_scr[pl.ds(tf, blk), :] = yf
        bon_scr[pl.ds(tf, blk), :] = qf["bonus"]
        y_scr[pl.ds(tb, blk), :] = yb
        bon_scr[pl.ds(tb, blk), :] = qb["bonus"]
        return carry

    def second_half(i, carry):
        tf, tb = block_starts(i)
        qf, yf, qb, yb = run_pair(tf, tb)
        finish(qf, yf + y_scr[pl.ds(tf, blk), :], qf["bonus"] + bon_scr[pl.ds(tf, blk), :], tf)
        finish(qb, yb + y_scr[pl.ds(tb, blk), :], qb["bonus"] + bon_scr[pl.ds(tb, blk), :], tb)
        return carry

    lax.fori_loop(0, half, first_half, 0)
    lax.fori_loop(half, nblk, second_half, 0)


def _rwkv(rkv, lora, pvec, ww, wa, wg, blk):
    b, t, _ = rkv.shape
    assert t % (2 * blk) == 0 and blk % CUM_ROWS == 0
    nhp = RW // LANES
    col = lambda off: pl.BlockSpec((1, t, LANES), lambda i, j, off=off: (i, 0, off + j))
    body = functools.partial(_rwkv_body, seq_len=t, blk=blk)
    return pl.pallas_call(
        body,
        grid=(b, nhp),
        in_specs=[col(0), col(nhp), col(2 * nhp),
                  pl.BlockSpec((1, t, LORA_W), lambda i, j: (i, 0, 0)),
                  pl.BlockSpec((PV_ROWS, LANES), lambda i, j: (0, j)),
                  pl.BlockSpec((1, 2, LANES, LANES), lambda i, j: (j, 0, 0, 0)),
                  pl.BlockSpec((1, 2, LANES, LANES), lambda i, j: (j, 0, 0, 0)),
                  pl.BlockSpec((1, 2 * LANES, LANES), lambda i, j: (j, 0, 0))],
        out_specs=pl.BlockSpec((1, t, LANES), lambda i, j: (i, 0, j)),
        out_shape=jax.ShapeDtypeStruct((b, t, RW), BF16),
        scratch_shapes=[pltpu.VMEM((t, LANES), F32), pltpu.VMEM((t, LANES), F32),
                        pltpu.VMEM((2, LANES, LANES), F32)],
        compiler_params=_cparams(("parallel", "parallel")),
        name="rwkv",
    )(rkv, rkv, rkv, lora, pvec, ww, wa, wg)


def _gla_body(q_ref, k_ref, v_ref, gg_ref, gkd_ref, wgk_ref, gkb_ref, nw_ref, out_ref,
              o_scr, s_scr, *, seq_len, blk):
    C = CHUNK
    nc = blk // C
    nblk = seq_len // blk
    half = nblk // 2
    lane = _iota2((1, LANES), 1)
    head_mask = [lane < GK, lane >= GK]
    si = _iota2((2 * GV, LANES), 0)
    sj = _iota2((2 * GV, LANES), 1)
    same_head = (si >> 7) == (sj >> 6)
    row_c = _iota2((C, LANES), 0)
    col_c = _iota2((C, LANES), 1) & (C - 1)
    zeros_cv = jnp.zeros((C, GV), F32)

    def run_pair(tf, tb):
        ctxs = [dict(d=0, rows=pl.ds(tf, blk), reverse=False, order=list(range(nc))),
                dict(d=1, rows=pl.ds(tb, blk), reverse=True, order=list(range(nc - 1, -1, -1)))]
        for cx in ctxs:
            cx["z"] = _mm(gkd_ref[0, cx["rows"], :], wgk_ref[0, cx["d"]]) + gkb_ref[0, cx["d"]]
        for cx in ctxs:
            z = cx["z"]
            lg = (jnp.minimum(z, 0.0) - jnp.log1p(jnp.exp(-jnp.abs(z)))) * (1.0 / GATE_NORM)
            cx["bcum"] = _chunk_cumsum(lg, cx["reverse"], 3)
        ch, chains, tri = {}, [], {}
        for cx in ctxs:
            d, reverse = cx["d"], cx["reverse"]
            tri[d] = (col_c >= row_c) if reverse else (col_c <= row_c)
            q = q_ref[0, cx["rows"], :] * (GK ** -0.5)
            k = k_ref[0, cx["rows"], :]
            v = v_ref[0, cx["rows"], :]
            for c in cx["order"]:
                sl = slice(c * C, (c + 1) * C)
                bc = cx["bcum"][sl]
                b_tot = bc[0:1, :] if reverse else bc[C - 1:C, :]
                ch[d, c] = dict(q_in=q[sl] * jnp.exp(bc), k_in=k[sl] * jnp.exp(-bc),
                                k_dec=k[sl] * jnp.exp(b_tot - bc), v=v[sl], gam=jnp.exp(b_tot))
                chains.append((d, c))
        att, intra, ds = {}, {}, {}
        for kk in chains:
            k_in = ch[kk]["k_in"]
            k_bd = jnp.concatenate([jnp.where(head_mask[0], k_in, 0.0), jnp.where(head_mask[1], k_in, 0.0)], axis=0)
            att[kk] = jnp.where(tri[kk[0]], _mm_nt(ch[kk]["q_in"], k_bd), 0.0)
        for kk in chains:
            v = ch[kk]["v"]
            v_bd = jnp.concatenate([jnp.concatenate([v[:, :GV], zeros_cv], axis=1),
                                    jnp.concatenate([zeros_cv, v[:, GV:]], axis=1)], axis=0)
            intra[kk] = _mm(att[kk], v_bd)
        for kk in chains:
            ds[kk] = jnp.where(same_head, _mm_tn(ch[kk]["v"], ch[kk]["k_dec"]), 0.0)
        s_in = {}
        for cx in ctxs:
            d = cx["d"]
            s = s_scr[d]
            for c in cx["order"]:
                s_in[d, c] = s
                s = s * ch[d, c]["gam"] + ds[d, c]
            s_scr[d] = s
        outs = {kk: _mm_nt(ch[kk]["q_in"], s_in[kk]) + intra[kk] for kk in chains}
        return (jnp.concatenate([outs[0, c] for c in range(nc)], axis=0),
                jnp.concatenate([outs[1, c] for c in range(nc)], axis=0))

    def finish(o, t0):
        gg = gg_ref[0, pl.ds(t0, blk), :]
        res = []
        for h in range(2):
            oh = o[:, h * GV:(h + 1) * GV]
            oh = oh * lax.rsqrt(jnp.mean(oh * oh, axis=-1, keepdims=True) + GLA_EPS) * nw_ref[...]
            gh = gg[:, h * GV:(h + 1) * GV]
            res.append(oh * (gh * _sigmoid(gh)))
        out_ref[0, pl.ds(t0, blk), :] = jnp.concatenate(res, axis=1).astype(out_ref.dtype)

    s_scr[...] = jnp.zeros(s_scr.shape, F32)

    def block_starts(i):
        return pl.multiple_of(i * blk, blk), pl.multiple_of((nblk - 1 - i) * blk, blk)

    def first_half(i, carry):
        tf, tb = block_starts(i)
        of, ob = run_pair(tf, tb)
        o_scr[pl.ds(tf, blk), :] = of
        o_scr[pl.ds(tb, blk), :] = ob
        return carry

    def second_half(i, carry):
        tf, tb = block_starts(i)
        of, ob = run_pair(tf, tb)
        finish(of + o_scr[pl.ds(tf, blk), :], tf)
        finish(ob + o_scr[pl.ds(tb, blk), :], tb)
        return carry

    lax.fori_loop(0, half, first_half, 0)
    lax.fori_loop(half, nblk, second_half, 0)


def _gla(qk, gv, gg, gkd, wgk, gkb, nw, blk):
    b, t, _ = qk.shape
    assert t % (2 * blk) == 0 and blk % CUM_ROWS == 0
    npair = GH // 2
    body = functools.partial(_gla_body, seq_len=t, blk=blk)
    return pl.pallas_call(
        body,
        grid=(b, npair),
        in_specs=[pl.BlockSpec((1, t, LANES), lambda i, j: (i, 0, j)),
                  pl.BlockSpec((1, t, LANES), lambda i, j: (i, 0, npair + j)),
                  pl.BlockSpec((1, t, 2 * GV), lambda i, j: (i, 0, j)),
                  pl.BlockSpec((1, t, 2 * GV), lambda i, j: (i, 0, j)),
                  pl.BlockSpec((1, t, GKD_W), lambda i, j: (i, 0, 0)),
                  pl.BlockSpec((1, 2, GKD_W, LANES), lambda i, j: (j, 0, 0, 0)),
                  pl.BlockSpec((1, 2, 1, LANES), lambda i, j: (j, 0, 0, 0)),
                  pl.BlockSpec((1, GV), lambda i, j: (0, 0))],
        out_specs=pl.BlockSpec((1, t, 2 * GV), lambda i, j: (i, 0, j)),
        out_shape=jax.ShapeDtypeStruct((b, t, GH * GV), BF16),
        scratch_shapes=[pltpu.VMEM((t, 2 * GV), F32), pltpu.VMEM((2, 2 * GV, LANES), F32)],
        compiler_params=_cparams(("parallel", "parallel")),
        name="gla",
    )(qk, qk, gv, gg, gkd, wgk, gkb, nw)


def _kvproj_body(m_ref, g_ref, w_ref, kv_ref):
    h = _rms(m_ref[...], g_ref[...], NORM_EPS).astype(BF16)
    kv_ref[...] = jnp.dot(h, w_ref[...], preferred_element_type=F32).astype(kv_ref.dtype)


def _kvproj(mem2d, g_mem, wkv, tm):
    n = mem2d.shape[0]
    return pl.pallas_call(
        _kvproj_body,
        grid=(n // tm,),
        in_specs=[pl.BlockSpec((tm, D_MODEL), lambda i: (i, 0)),
                  pl.BlockSpec((1, D_MODEL), lambda i: (0, 0)),
                  pl.BlockSpec((D_MODEL, 2 * D_MODEL), lambda i: (0, 0))],
        out_specs=pl.BlockSpec((tm, 2 * D_MODEL), lambda i: (i, 0)),
        out_shape=jax.ShapeDtypeStruct((n, 2 * D_MODEL), BF16),
        compiler_params=_cparams(("parallel",)),
        name="kvproj",
    )(mem2d, g_mem, wkv)


def _xattn_body(x_ref, rw_ref, gl_ref, kv_ref, wout_ref, gmix_ref, gpre_ref, wq_ref, wo_ref, gpost_ref, o_ref):
    mixed = (jnp.dot(rw_ref[0], wout_ref[0:RW, :], preferred_element_type=F32)
             + jnp.dot(gl_ref[0], wout_ref[RW:, :], preferred_element_type=F32))
    x = x_ref[0] + _rms(mixed, gmix_ref[...], NORM_EPS)
    h = _rms(x, gpre_ref[...], NORM_EPS).astype(BF16)
    q = jnp.dot(h, wq_ref[...], preferred_element_type=F32).astype(BF16)
    heads = range(X_HEADS)
    scores = [lax.dot_general(q[:, hd * X_HD:(hd + 1) * X_HD], kv_ref[0, :, hd * X_HD:(hd + 1) * X_HD],
                              (((1,), (1,)), ((), ())), preferred_element_type=F32) * (X_HD ** -0.5)
              for hd in heads]
    probs = []
    for s in scores:
        e = jnp.exp(s - jnp.max(s, axis=-1, keepdims=True))
        probs.append((e * (1.0 / jnp.sum(e, axis=-1, keepdims=True))).astype(BF16))
    outs = [jnp.dot(probs[hd], kv_ref[0, :, D_MODEL + hd * X_HD:D_MODEL + (hd + 1) * X_HD],
                    preferred_element_type=F32).astype(BF16) for hd in heads]
    att = jnp.dot(jnp.concatenate(outs, axis=1), wo_ref[...], preferred_element_type=F32)
    o_ref[0] = x + _rms(att, gpost_ref[...], NORM_EPS)


def _xattn(x3d, rw3d, gl3d, kv3d, w_out, g_mix_post, g_pre, wq, wo, g_post, tm):
    b, t, _ = x3d.shape
    row = lambda w: pl.BlockSpec((1, tm, w), lambda i, j: (i, j, 0))
    const = lambda r, c: pl.BlockSpec((r, c), lambda i, j: (0, 0))
    return pl.pallas_call(
        _xattn_body,
        grid=(b, t // tm),
        in_specs=[row(D_MODEL), row(RW), row(GH * GV),
                  pl.BlockSpec((1, N_MEM, 2 * D_MODEL), lambda i, j: (i, 0, 0)),
                  const(D_MODEL, D_MODEL), const(1, D_MODEL), const(1, D_MODEL),
                  const(D_MODEL, D_MODEL), const(D_MODEL, D_MODEL), const(1, D_MODEL)],
        out_specs=row(D_MODEL),
        out_shape=jax.ShapeDtypeStruct((b, t, D_MODEL), F32),
        compiler_params=_cparams(("parallel", "parallel")),
        name="xattn",
    )(x3d, rw3d, gl3d, kv3d, w_out, g_mix_post, g_pre, wq, wo, g_post)


def _ffn_body(x_ref, gpre_ref, w1_ref, w2_ref, gpost_ref, o_ref, h_scr, acc_scr):
    j = pl.program_id(1)

    @pl.when(j == 0)
    def _():
        h_scr[...] = _rms(x_ref[...], gpre_ref[...], NORM_EPS).astype(BF16)
        acc_scr[...] = jnp.zeros_like(acc_scr)

    a = jnp.dot(h_scr[...], w1_ref[...], preferred_element_type=F32)
    a = jnp.square(jnp.maximum(a, 0.0)).astype(BF16)
    acc_scr[...] += jnp.dot(a, w2_ref[...], preferred_element_type=F32)

    @pl.when(j == pl.num_programs(1) - 1)
    def _():
        o_ref[...] = x_ref[...] + _rms(acc_scr[...], gpost_ref[...], NORM_EPS)


def _ffn(x2d, g_pre, w1, w2, g_post, tm, tf):
    n = x2d.shape[0]
    return pl.pallas_call(
        _ffn_body,
        grid=(n // tm, D_FF // tf),
        in_specs=[pl.BlockSpec((tm, D_MODEL), lambda i, j: (i, 0)),
                  pl.BlockSpec((1, D_MODEL), lambda i, j: (0, 0)),
                  pl.BlockSpec((D_MODEL, tf), lambda i, j: (0, j)),
                  pl.BlockSpec((tf, D_MODEL), lambda i, j: (j, 0)),
                  pl.BlockSpec((1, D_MODEL), lambda i, j: (0, 0))],
        out_specs=pl.BlockSpec((tm, D_MODEL), lambda i, j: (i, 0)),
        out_shape=jax.ShapeDtypeStruct((n, D_MODEL), F32),
        scratch_shapes=[pltpu.VMEM((tm, D_MODEL), BF16), pltpu.VMEM((tm, D_MODEL), F32)],
        compiler_params=_cparams(("parallel", "arbitrary")),
        name="ffn",
    )(x2d, g_pre, w1, w2, g_post)


def _pack_params(p):
    w_in = p["w_in"]
    g0 = R_COLS
    zeros = lambda n: jnp.zeros((D_MODEL, n), w_in.dtype)
    w_perm = jnp.concatenate([
        w_in[:, :R_COLS], zeros(LORA_W - (R_COLS - 3 * RW)),
        w_in[:, g0:g0 + 2 * GH * GK],
        w_in[:, g0 + 2 * GH * GK:g0 + 2 * GH * GK + GH * GV],
        w_in[:, g0 + 2 * GH * GK + GH * GV + GK_LORA:],
        w_in[:, g0 + 2 * GH * GK + GH * GV:g0 + 2 * GH * GK + GH * GV + GK_LORA], zeros(GKD_W - GK_LORA),
    ], axis=1).astype(BF16)
    row = lambda v: v.reshape(1, -1)
    mu_p, mu_n = p["mu_prev"], p["mu_next"]
    pvec = jnp.concatenate([
        row(p["w0_f"]), row(p["w0_b"]), row(p["a0_f"]), row(p["a0_b"]),
        row(p["k_k"]), row(p["k_a"]), row(p["r_k"]), row(p["lnx_w"]), row(p["lnx_b"]),
        jnp.zeros((PV_ROWS - 9, RW), F32)], axis=0)
    pad_mu = lambda v: jnp.pad(v, (0, C_QK - R_COLS))
    mu = jnp.stack([pad_mu(mu_p), pad_mu(mu_n)], axis=0)
    nhp = RW // LANES

    def lora_pair(wf, wb):
        wf = wf.reshape(-1, nhp, LANES).transpose(1, 0, 2)
        wb = wb.reshape(-1, nhp, LANES).transpose(1, 0, 2)
        z = jnp.zeros_like(wf)
        return jnp.stack([jnp.concatenate([wf, z], axis=1), jnp.concatenate([z, wb], axis=1)],
                         axis=1).astype(BF16)

    ww = lora_pair(p["w2_f"], p["w2_b"])
    wa = lora_pair(p["a2_f"], p["a2_b"])
    wg = jnp.pad(p["g2"], ((0, 2 * LANES - GATE_LORA), (0, 0)))
    wg = wg.reshape(2 * LANES, nhp, LANES).transpose(1, 0, 2).astype(BF16)
    npair = GH // 2

    def gk_pair(w):
        w = jnp.pad(w, ((0, GKD_W - GK_LORA), (0, 0)))
        return w.reshape(GKD_W, npair, LANES).transpose(1, 0, 2)

    wgk = jnp.stack([gk_pair(p["gk2_f"]), gk_pair(p["gk2_b"])], axis=1).astype(BF16)
    gkb = jnp.stack([p["gkb_f"].reshape(npair, 1, LANES), p["gkb_b"].reshape(npair, 1, LANES)], axis=1)
    return dict(
        w_perm=w_perm, pvec=pvec, mu=mu, ww=ww, wa=wa, wg=wg, wgk=wgk, gkb=gkb,
        nw=row(p["gla_norm_w"]),
        g_mix_pre=row(p["g_mix_pre"]), g_mix_post=row(p["g_mix_post"]),
        g_x_pre=row(p["g_x_pre"]), g_x_post=row(p["g_x_post"]), g_mem=row(p["g_mem"]),
        g_ffn_pre=row(p["g_ffn_pre"]), g_ffn_post=row(p["g_ffn_post"]),
        w_out=p["w_out"].astype(BF16), wq=p["wq_x"].astype(BF16), wkv=p["wkv_x"].astype(BF16),
        wo=p["wo_x"].astype(BF16), w1=p["w_ff1"].astype(BF16), w2=p["w_ff2"].astype(BF16))


def _pick(n, pref):
    t = pref
    while n % t:
        t //= 2
    return t


def _trunk(x, mem, pk):
    b, t, _ = x.shape
    n = b * t
    tm = _pick(n, 1024)
    x2d = x.reshape(n, D_MODEL)
    rkv, lora, qk, gv, gg, gkd = _inproj(x2d, pk["g_mix_pre"], pk["w_perm"], pk["mu"], _pick(t, 512), t)
    r3 = lambda a: a.reshape(b, t, a.shape[-1])
    blk = _pick(t // 2, 1024)
    rw = _rwkv(r3(rkv), r3(lora), pk["pvec"], pk["ww"], pk["wa"], pk["wg"], blk)
    gl = _gla(r3(qk), r3(gv), r3(gg), r3(gkd), pk["wgk"], pk["gkb"], pk["nw"], blk)
    nm = mem.shape[0] * mem.shape[1]
    kv = _kvproj(mem.reshape(nm, D_MODEL), pk["g_mem"], pk["wkv"], _pick(nm, 512))
    x2 = _xattn(x, rw, gl, kv.reshape(mem.shape[0], mem.shape[1], 2 * D_MODEL), pk["w_out"],
                pk["g_mix_post"], pk["g_x_pre"], pk["wq"], pk["wo"], pk["g_x_post"], _pick(t, 512))
    y = _ffn(x2.reshape(n, D_MODEL), pk["g_ffn_pre"], pk["w1"], pk["w2"], pk["g_ffn_post"],
             tm, 1024)
    return y.reshape(b, t, D_MODEL)


def kernel(x_prompt, x_sample, mem_prompt, mem_sample, g_mix_pre, w_in, mu_prev, mu_next, w0_f, w2_f, w0_b, w2_b, a0_f, a2_f, a0_b, a2_b, g2, k_k, k_a, r_k, lnx_w, lnx_b, gk2_f, gkb_f, gk2_b, gkb_b, gla_norm_w, w_out, g_mix_post, g_x_pre, g_mem, wq_x, wkv_x, wo_x, g_x_post, g_ffn_pre, w_ff1, w_ff2, g_ffn_post):
    params = dict(
        g_mix_pre=g_mix_pre, w_in=w_in, mu_prev=mu_prev, mu_next=mu_next, w0_f=w0_f, w2_f=w2_f,
        w0_b=w0_b, w2_b=w2_b, a0_f=a0_f, a2_f=a2_f, a0_b=a0_b, a2_b=a2_b, g2=g2, k_k=k_k, k_a=k_a,
        r_k=r_k, lnx_w=lnx_w, lnx_b=lnx_b, gk2_f=gk2_f, gkb_f=gkb_f, gk2_b=gk2_b, gkb_b=gkb_b,
        gla_norm_w=gla_norm_w, w_out=w_out, g_mix_post=g_mix_post, g_x_pre=g_x_pre, g_mem=g_mem,
        wq_x=wq_x, wkv_x=wkv_x, wo_x=wo_x, g_x_post=g_x_post, g_ffn_pre=g_ffn_pre, w_ff1=w_ff1,
        w_ff2=w_ff2, g_ffn_post=g_ffn_post)
    assert w_in.shape[0] == 1, "single-layer stack expected"
    pk = _pack_params({name: w[0] for name, w in params.items()})
    return (_trunk(x_prompt, mem_prompt, pk), _trunk(x_sample, mem_sample, pk))
```

```python
import functools
import math

import jax
import jax.numpy as jnp
from jax import lax
from jax.experimental import pallas as pl
from jax.experimental.pallas import tpu as pltpu

F32 = jnp.float32
BF16 = jnp.bfloat16

D_MODEL = 1024
RW = 512
RN = 64
DECAY_LORA = 64
AAA_LORA = 64
GATE_LORA = 160
GN_EPS = 64e-5
GH = 4
GV = 128
GK = 64
GK_LORA = 16
GATE_NORM = 16.0
CHUNK = 64
GLA_EPS = 1e-5
R_COLS = 3 * RW + 2 * DECAY_LORA + 2 * AAA_LORA + GATE_LORA
N_MEM = 256
X_HEADS = 4
X_HD = D_MODEL // X_HEADS
D_FF = 4 * D_MODEL
NORM_EPS = 1e-6

LANES = 128
CUM_ROWS = 2 * CHUNK
HALO = 16
SHIFT_COLS = 512
LORA_W = 512
GKD_W = 128
C_RKV, C_LORA, C_QK, C_GV, C_GG, C_GKD = 0, 1536, 2048, 2560, 3072, 3584
N_PROJ = 3712
VMEM_LIMIT = 56 * 1024 * 1024


def _cparams(sem):
    return pltpu.CompilerParams(dimension_semantics=sem, vmem_limit_bytes=VMEM_LIMIT)


def _mm(a, b):
    return jnp.dot(a.astype(BF16), b.astype(BF16), preferred_element_type=F32)


def _mm_nt(a, b):
    return lax.dot_general(a.astype(BF16), b.astype(BF16), (((1,), (1,)), ((), ())),
                           preferred_element_type=F32)


def _mm_tn(a, b):
    return lax.dot_general(a.astype(BF16), b.astype(BF16), (((0,), (0,)), ((), ())),
                           preferred_element_type=F32)


def _mm_lhs01(m01, x, terms):
    acc = None
    rem = x
    for i in range(terms):
        piece = rem.astype(BF16)
        part = jnp.dot(m01, piece, preferred_element_type=F32)
        acc = part if acc is None else acc + part
        if i + 1 < terms:
            rem = rem - piece.astype(F32)
    return acc


def _rms(x, g, eps):
    return x * lax.rsqrt(jnp.mean(x * x, axis=-1, keepdims=True) + eps) * g


def _sigmoid(x):
    return 1.0 / (1.0 + jnp.exp(-x))


def _iota2(shape, dim):
    return lax.broadcasted_iota(jnp.int32, shape, dim)


def _chunk_tri(n, reverse):
    t = _iota2((n, n), 0)
    s = _iota2((n, n), 1)
    same = (t >> 6) == (s >> 6)
    tri = (s >= t) if reverse else (s <= t)
    return jnp.where(same & tri, 1.0, 0.0).astype(BF16)


def _chunk_cumsum(x, reverse, terms):
    tri = _chunk_tri(CUM_ROWS, reverse)
    parts = [_mm_lhs01(tri, x[i:i + CUM_ROWS], terms) for i in range(0, x.shape[0], CUM_ROWS)]
    return parts[0] if len(parts) == 1 else jnp.concatenate(parts, axis=0)


def _inproj_body(x_ref, xp_ref, xn_ref, g_ref, w_ref, mu_ref, rkv_ref, lora_ref, qk_ref, gv_ref, gg_ref,
                 gkd_ref, *, tiles_per_seq):
    tm = x_ref.shape[0]
    i = pl.program_id(0)
    g = g_ref[...]
    h = _rms(x_ref[...], g, NORM_EPS).astype(BF16)
    h_prev = _rms(xp_ref[...], g, NORM_EPS).astype(BF16)
    h_next = _rms(xn_ref[...], g, NORM_EPS).astype(BF16)
    h_ext = jnp.concatenate([h_prev, h, h_next], axis=0)
    pos = i % tiles_per_seq
    has_prev = pos > 0
    has_next = pos < tiles_per_seq - 1
    rid = _iota2((tm, SHIFT_COLS), 0)

    def proj(lo, hi):
        return jnp.dot(h, w_ref[:, lo:hi], preferred_element_type=F32)

    for lo in range(C_RKV, C_QK, SHIFT_COLS):
        p_ext = jnp.dot(h_ext, w_ref[:, lo:lo + SHIFT_COLS], preferred_element_type=F32)
        p = p_ext[HALO:HALO + tm, :]
        p_before = p_ext[HALO - 1:HALO, :]
        p_after = p_ext[HALO + tm:HALO + tm + 1, :]
        prev = jnp.where(rid == 0, jnp.where(has_prev, p_before, 0.0), pltpu.roll(p, 1, 0))
        nxt = jnp.where(rid == tm - 1, jnp.where(has_next, p_after, 0.0), pltpu.roll(p, tm - 1, 0))
        mu = mu_ref[:, lo:lo + SHIFT_COLS]
        shifted = p + mu[0:1, :] * (prev - p) + mu[1:2, :] * (nxt - p)
        if lo < C_LORA:
            rkv_ref[:, lo:lo + SHIFT_COLS] = shifted
        else:
            lora_ref[:, lo - C_LORA:lo - C_LORA + SHIFT_COLS] = shifted
    qk_ref[...] = proj(C_QK, C_GV)
    gv_ref[...] = proj(C_GV, C_GG)
    gg_ref[...] = proj(C_GG, C_GKD)
    gkd_ref[...] = proj(C_GKD, N_PROJ)


def _inproj(x2d, g, w_perm, mu, tm, seq_len):
    n = x2d.shape[0]
    assert seq_len % tm == 0 and tm % HALO == 0
    widths = (C_LORA - C_RKV, C_QK - C_LORA, C_GV - C_QK, C_GG - C_GV, C_GKD - C_GG, N_PROJ - C_GKD)
    hb = tm // HALO
    last = n // HALO - 1
    body = functools.partial(_inproj_body, tiles_per_seq=seq_len // tm)
    return pl.pallas_call(
        body,
        grid=(n // tm,),
        in_specs=[pl.BlockSpec((tm, D_MODEL), lambda i: (i, 0)),
                  pl.BlockSpec((HALO, D_MODEL), lambda i: (jnp.maximum(i * hb - 1, 0), 0)),
                  pl.BlockSpec((HALO, D_MODEL), lambda i: (jnp.minimum((i + 1) * hb, last), 0)),
                  pl.BlockSpec((1, D_MODEL), lambda i: (0, 0)),
                  pl.BlockSpec((D_MODEL, N_PROJ), lambda i: (0, 0)),
                  pl.BlockSpec((2, C_QK), lambda i: (0, 0))],
        out_specs=[pl.BlockSpec((tm, w), lambda i: (i, 0)) for w in widths],
        out_shape=[jax.ShapeDtypeStruct((n, w), F32) for w in widths],
        compiler_params=_cparams(("parallel",)),
        name="inproj",
    )(x2d, x2d, x2d, g, w_perm, mu)


PV_W0_F, PV_W0_B, PV_A0_F, PV_A0_B, PV_KK, PV_KA, PV_RK, PV_LNW, PV_LNB = range(9)
PV_ROWS = 16


def _rwkv_body(r_ref, k_ref, v_ref, lo_ref, pv_ref, ww_ref, wa_ref, wg_ref, out_ref,
               y_scr, bon_scr, s_scr, *, seq_len, blk):
    C = CHUNK
    nc = blk // C
    nblk = seq_len // blk
    half = nblk // 2
    pv = pv_ref[...]
    prow = lambda i: pv[i:i + 1, :]

    lane = _iota2((1, LANES), 1)
    head_mask = [lane < RN, lane >= RN]
    li = _iota2((LANES, LANES), 0)
    lj = _iota2((LANES, LANES), 1)
    same_head = (li >> 6) == (lj >> 6)
    seg01 = jnp.where(same_head, 1.0, 0.0).astype(BF16)
    row_c = _iota2((C, LANES), 0)
    col_c = _iota2((C, LANES), 1) & (C - 1)
    eye_p = jnp.where(row_c == col_c, 1.0, 0.0)
    zeros_cl = jnp.zeros((C, LANES), BF16)
    zeros_ll = jnp.zeros((LANES, LANES), BF16)

    def seg_sum(x):
        return _mm(x, seg01)

    def block_inputs(t0, reverse):
        d = 1 if reverse else 0
        rows = pl.ds(t0, blk)
        r, k, v, lo = r_ref[0, rows, :], k_ref[0, rows, :], v_ref[0, rows, :], lo_ref[0, rows, :]
        zw = _mm(jnp.tanh(lo[:, 0:LANES]), ww_ref[0, d])
        za = _mm(lo[:, LANES:2 * LANES], wa_ref[0, d])
        lw = -_sigmoid(prow(PV_W0_F + d) + zw) * math.exp(-0.5)
        a = _sigmoid(prow(PV_A0_F + d) + za)
        kd = k * (1.0 + (a - 1.0) * prow(PV_KA))
        kk = k * prow(PV_KK)
        kk = kk * lax.rsqrt(seg_sum(kk * kk) + 1e-12)
        bonus = seg_sum(r * kd * prow(PV_RK)) * v
        g = _chunk_cumsum(lw, reverse, 2)
        return dict(r=r, v=v, kd=kd, kk=kk, a=a, lw=lw, g=g, bonus=bonus, lo=lo)

    def bd(x):
        xb = x.astype(BF16)
        zero = jnp.zeros_like(xb)
        return jnp.concatenate([jnp.where(head_mask[0], xb, zero), jnp.where(head_mask[1], xb, zero)], axis=0)

    def block_terms(ctxs):
        tri_s, tri_i, ch, chains = {}, {}, {}, []
        for d, q, order, reverse in ctxs:
            tri_s[d] = (col_c > row_c) if reverse else (col_c < row_c)
            tri_i[d] = (col_c >= row_c) if reverse else (col_c <= row_c)
            for c in order:
                sl = slice(c * C, (c + 1) * C)
                g, lw, kk = q["g"][sl], q["lw"][sl], q["kk"][sl]
                r, v, kd, a = q["r"][sl], q["v"][sl], q["kd"][sl], q["a"][sl]
                g_tot = g[0:1, :] if reverse else g[C - 1:C, :]
                eng = jnp.exp(-g)
                beta = kk * a
                dec = jnp.exp(g_tot - g)
                mxu = lambda x: x.astype(BF16)
                ch[d, c] = dict(v=mxu(v), rt=r * jnp.exp(g), at=mxu(-kk * jnp.exp(g - lw)), bt=mxu(beta * eng),
                                kt=mxu(kd * eng), bh=mxu(beta * dec), kh=mxu(kd * dec), gam=jnp.exp(g_tot))
                chains.append((d, c))
        a_ab, a_ak, a_rbk = {}, {}, {}
        for k in chains:
            e = ch[k]
            gm = _mm_nt(jnp.concatenate([e["at"], e["rt"].astype(BF16)], axis=0),
                        jnp.concatenate([bd(e["bt"]), bd(e["kt"])], axis=0))
            a_ab[k] = jnp.where(tri_s[k[0]], gm[:C, :LANES], 0.0).astype(BF16)
            a_ak[k] = jnp.where(tri_s[k[0]], gm[:C, LANES:], 0.0).astype(BF16)
            a_rbk[k] = jnp.concatenate([jnp.where(tri_i[k[0]], gm[C:, :LANES], 0.0),
                                        jnp.where(tri_i[k[0]], gm[C:, LANES:], 0.0)], axis=1).astype(BF16)
        tm = {k: eye_p + jnp.where((row_c >> 1) == (col_c >> 1), a_ab[k], 0.0) for k in chains}
        for lvl in range(1, 6):
            same = (row_c >> (lvl + 1)) == (col_c >> (lvl + 1))
            lower = (((row_c >> lvl) & 1) == 1) & (((col_c >> lvl) & 1) == 0)
            upper = (((row_c >> lvl) & 1) == 0) & (((col_c >> lvl) & 1) == 1)
            off = {d: same & (upper if reverse else lower) for d, _, _, reverse in ctxs}
            left = {k: _mm(tm[k], bd(jnp.where(off[k[0]], a_ab[k], zeros_cl))) for k in chains}
            tm = {k: tm[k] + _mm(left[k], bd(tm[k])) for k in chains}
        akv = {k: _mm(a_ak[k], bd(ch[k]["v"])) for k in chains}
        wu = {k: _mm(tm[k], jnp.concatenate([bd(ch[k]["at"]), bd(akv[k])], axis=1)).astype(BF16) for k in chains}

        def chunk_terms(k):
            e = ch[k]
            rhs = jnp.concatenate(
                [jnp.concatenate([bd(wu[k][:, :LANES]), bd(wu[k][:, LANES:])], axis=1),
                 jnp.concatenate([zeros_ll, bd(e["v"])], axis=1)], axis=0)
            qy = _mm(a_rbk[k], rhs)
            lhs = jnp.concatenate([wu[k], jnp.concatenate([zeros_cl, e["v"]], axis=1)], axis=0)
            pn = _mm_tn(lhs, jnp.concatenate([e["bh"], e["kh"]], axis=0))
            p_off = jnp.where(same_head, pn[:LANES], 0.0).astype(BF16)
            n0t = jnp.where(same_head, pn[LANES:], 0.0)
            return e["rt"] + qy[:, :LANES], qy[:, LANES:], e["gam"], p_off, n0t

        return chunk_terms

    def run_pair(tf, tb):
        qf = block_inputs(tf, False)
        qb = block_inputs(tb, True)
        ctxs = [(0, qf, list(range(nc)), False), (1, qb, list(range(nc - 1, -1, -1)), True)]
        chunk_terms = block_terms(ctxs)
        s = [s_scr[0], s_scr[1]]
        ys = [[None] * nc, [None] * nc]
        terms = {(d, order[0]): chunk_terms((d, order[0])) for d, _, order, _ in ctxs}
        for step in range(nc):
            if step + 1 < nc:
                for d, _, order, _ in ctxs:
                    terms[d, order[step + 1]] = chunk_terms((d, order[step + 1]))
            for d, _, order, _ in ctxs:
                qh, y0, gam, p_off, n0t = terms[d, order[step]]
                sb = s[d].astype(BF16)
                ys[d][order[step]] = _mm_nt(qh, sb) + y0
                s[d] = (s[d] * gam + n0t) + jnp.dot(sb, p_off, preferred_element_type=F32)
        s_scr[0] = s[0]
        s_scr[1] = s[1]
        return qf, jnp.concatenate(ys[0], axis=0), qb, jnp.concatenate(ys[1], axis=0)

    def finish(q, y, bonus, t0):
        mean = seg_sum(y) * (1.0 / RN)
        dy = y - mean
        var = seg_sum(dy * dy) * (1.0 / RN)
        gn = dy * lax.rsqrt(var + GN_EPS) * prow(PV_LNW) + prow(PV_LNB)
        gate = _mm(_sigmoid(q["lo"][:, 2 * LANES:]), wg_ref[0])
        out_ref[0, pl.ds(t0, blk), :] = ((gn + bonus) * gate).astype(out_ref.dtype)

    s_scr[...] = jnp.zeros(s_scr.shape, F32)

    def block_starts(i):
        return pl.multiple_of(i * blk, blk), pl.multiple_of((nblk - 1 - i) * blk, blk)

    def first_half(i, carry):
        tf, tb = block_starts(i)
        qf, yf, qb, yb = run_pair(tf, tb)
        y_scr[pl.ds(tf, blk), :] = yf
        bon_scr[pl.ds(tf, blk), :] = qf["bonus"]
        y_scr[pl.ds(tb, blk), :] = yb
        bon_scr[pl.ds(tb, blk), :] = qb["bonus"]
        return carry

    def second_half(i, carry):
        tf, tb = block_starts(i)
        qf, yf, qb, yb = run_pair(tf, tb)
        finish(qf, yf + y_scr[pl.ds(tf, blk), :], qf["bonus"] + bon_scr[pl.ds(tf, blk), :], tf)
        finish(qb, yb + y_scr[pl.ds(tb, blk), :], qb["bonus"] + bon_scr[pl.ds(tb, blk), :], tb)
        return carry

    lax.fori_loop(0, half, first_half, 0)
    lax.fori_loop(half, nblk, second_half, 0)


def _rwkv(rkv, lora, pvec, ww, wa, wg, blk):
    b, t, _ = rkv.shape
    assert t % (2 * blk) == 0 and blk % CUM_ROWS == 0
    nhp = RW // LANES
    col = lambda off: pl.BlockSpec((1, t, LANES), lambda i, j, off=off: (i, 0, off + j))
    body = functools.partial(_rwkv_body, seq_len=t, blk=blk)
    return pl.pallas_call(
        body,
        grid=(b, nhp),
        in_specs=[col(0), col(nhp), col(2 * nhp),
                  pl.BlockSpec((1, t, LORA_W), lambda i, j: (i, 0, 0)),
                  pl.BlockSpec((PV_ROWS, LANES), lambda i, j: (0, j)),
                  pl.BlockSpec((1, 2, LANES, LANES), lambda i, j: (j, 0, 0, 0)),
                  pl.BlockSpec((1, 2, LANES, LANES), lambda i, j: (j, 0, 0, 0)),
                  pl.BlockSpec((1, 2 * LANES, LANES), lambda i, j: (j, 0, 0))],
        out_specs=pl.BlockSpec((1, t, LANES), lambda i, j: (i, 0, j)),
        out_shape=jax.ShapeDtypeStruct((b, t, RW), BF16),
        scratch_shapes=[pltpu.VMEM((t, LANES), F32), pltpu.VMEM((t, LANES), F32),
                        pltpu.VMEM((2, LANES, LANES), F32)],
        compiler_params=_cparams(("parallel", "parallel")),
        name="rwkv",
    )(rkv, rkv, rkv, lora, pvec, ww, wa, wg)


def _gla_body(q_ref, k_ref, v_ref, gg_ref, gkd_ref, wgk_ref, gkb_ref, nw_ref, out_ref,
              o_scr, s_scr, *, seq_len, blk):
    C = CHUNK
    nc = blk // C
    nblk = seq_len // blk
    half = nblk // 2
    lane = _iota2((1, LANES), 1)
    head_mask = [lane < GK, lane >= GK]
    si = _iota2((2 * GV, LANES), 0)
    sj = _iota2((2 * GV, LANES), 1)
    same_head = (si >> 7) == (sj >> 6)
    row_c = _iota2((C, LANES), 0)
    col_c = _iota2((C, LANES), 1) & (C - 1)
    zeros_cv = jnp.zeros((C, GV), F32)

    def run_pair(tf, tb):
        ctxs = [dict(d=0, rows=pl.ds(tf, blk), reverse=False, order=list(range(nc))),
                dict(d=1, rows=pl.ds(tb, blk), reverse=True, order=list(range(nc - 1, -1, -1)))]
        for cx in ctxs:
            cx["z"] = _mm(gkd_ref[0, cx["rows"], :], wgk_ref[0, cx["d"]]) + gkb_ref[0, cx["d"]]
        for cx in ctxs:
            z = cx["z"]
            lg = (jnp.minimum(z, 0.0) - jnp.log1p(jnp.exp(-jnp.abs(z)))) * (1.0 / GATE_NORM)
            cx["bcum"] = _chunk_cumsum(lg, cx["reverse"], 3)
        ch, chains, tri = {}, [], {}
        for cx in ctxs:
            d, reverse = cx["d"], cx["reverse"]
            tri[d] = (col_c >= row_c) if reverse else (col_c <= row_c)
            q = q_ref[0, cx["rows"], :] * (GK ** -0.5)
            k = k_ref[0, cx["rows"], :]
            v = v_ref[0, cx["rows"], :]
            for c in cx["order"]:
                sl = slice(c * C, (c + 1) * C)
                bc = cx["bcum"][sl]
                b_tot = bc[0:1, :] if reverse else bc[C - 1:C, :]
                ch[d, c] = dict(q_in=q[sl] * jnp.exp(bc), k_in=k[sl] * jnp.exp(-bc),
                                k_dec=k[sl] * jnp.exp(b_tot - bc), v=v[sl], gam=jnp.exp(b_tot))
                chains.append((d, c))
        att, intra, ds = {}, {}, {}
        for kk in chains:
            k_in = ch[kk]["k_in"]
            k_bd = jnp.concatenate([jnp.where(head_mask[0], k_in, 0.0), jnp.where(head_mask[1], k_in, 0.0)], axis=0)
            att[kk] = jnp.where(tri[kk[0]], _mm_nt(ch[kk]["q_in"], k_bd), 0.0)
        for kk in chains:
            v = ch[kk]["v"]
            v_bd = jnp.concatenate([jnp.concatenate([v[:, :GV], zeros_cv], axis=1),
                                    jnp.concatenate([zeros_cv, v[:, GV:]], axis=1)], axis=0)
            intra[kk] = _mm(att[kk], v_bd)
        for kk in chains:
            ds[kk] = jnp.where(same_head, _mm_tn(ch[kk]["v"], ch[kk]["k_dec"]), 0.0)
        s_in = {}
        for cx in ctxs:
            d = cx["d"]
            s = s_scr[d]
            for c in cx["order"]:
                s_in[d, c] = s
                s = s * ch[d, c]["gam"] + ds[d, c]
            s_scr[d] = s
        outs = {kk: _mm_nt(ch[kk]["q_in"], s_in[kk]) + intra[kk] for kk in chains}
        return (jnp.concatenate([outs[0, c] for c in range(nc)], axis=0),
                jnp.concatenate([outs[1, c] for c in range(nc)], axis=0))

    def finish(o, t0):
        gg = gg_ref[0, pl.ds(t0, blk), :]
        res = []
        for h in range(2):
            oh = o[:, h * GV:(h + 1) * GV]
            oh = oh * lax.rsqrt(jnp.mean(oh * oh, axis=-1, keepdims=True) + GLA_EPS) * nw_ref[...]
            gh = gg[:, h * GV:(h + 1) * GV]
            res.append(oh * (gh * _sigmoid(gh)))
        out_ref[0, pl.ds(t0, blk), :] = jnp.concatenate(res, axis=1).astype(out_ref.dtype)

    s_scr[...] = jnp.zeros(s_scr.shape, F32)

    def block_starts(i):
        return pl.multiple_of(i * blk, blk), pl.multiple_of((nblk - 1 - i) * blk, blk)

    def first_half(i, carry):
        tf, tb = block_starts(i)
        of, ob = run_pair(tf, tb)
        o_scr[pl.ds(tf, blk), :] = of
        o_scr[pl.ds(tb, blk), :] = ob
        return carry

    def second_half(i, carry):
        tf, tb = block_starts(i)
        of, ob = run_pair(tf, tb)
        finish(of + o_scr[pl.ds(tf, blk), :], tf)
        finish(ob + o_scr[pl.ds(tb, blk), :], tb)
        return carry

    lax.fori_loop(0, half, first_half, 0)
    lax.fori_loop(half, nblk, second_half, 0)


def _gla(qk, gv, gg, gkd, wgk, gkb, nw, blk):
    b, t, _ = qk.shape
    assert t % (2 * blk) == 0 and blk % CUM_ROWS == 0
    npair = GH // 2
    body = functools.partial(_gla_body, seq_len=t, blk=blk)
    return pl.pallas_call(
        body,
        grid=(b, npair),
        in_specs=[pl.BlockSpec((1, t, LANES), lambda i, j: (i, 0, j)),
                  pl.BlockSpec((1, t, LANES), lambda i, j: (i, 0, npair + j)),
                  pl.BlockSpec((1, t, 2 * GV), lambda i, j: (i, 0, j)),
                  pl.BlockSpec((1, t, 2 * GV), lambda i, j: (i, 0, j)),
                  pl.BlockSpec((1, t, GKD_W), lambda i, j: (i, 0, 0)),
                  pl.BlockSpec((1, 2, GKD_W, LANES), lambda i, j: (j, 0, 0, 0)),
                  pl.BlockSpec((1, 2, 1, LANES), lambda i, j: (j, 0, 0, 0)),
                  pl.BlockSpec((1, GV), lambda i, j: (0, 0))],
        out_specs=pl.BlockSpec((1, t, 2 * GV), lambda i, j: (i, 0, j)),
        out_shape=jax.ShapeDtypeStruct((b, t, GH * GV), BF16),
        scratch_shapes=[pltpu.VMEM((t, 2 * GV), F32), pltpu.VMEM((2, 2 * GV, LANES), F32)],
        compiler_params=_cparams(("parallel", "parallel")),
        name="gla",
    )(qk, qk, gv, gg, gkd, wgk, gkb, nw)


def _kvproj_body(m_ref, g_ref, w_ref, kv_ref):
    h = _rms(m_ref[...], g_ref[...], NORM_EPS).astype(BF16)
    kv_ref[...] = jnp.dot(h, w_ref[...], preferred_element_type=F32).astype(kv_ref.dtype)


def _kvproj(mem2d, g_mem, wkv, tm):
    n = mem2d.shape[0]
    return pl.pallas_call(
        _kvproj_body,
        grid=(n // tm,),
        in_specs=[pl.BlockSpec((tm, D_MODEL), lambda i: (i, 0)),
                  pl.BlockSpec((1, D_MODEL), lambda i: (0, 0)),
                  pl.BlockSpec((D_MODEL, 2 * D_MODEL), lambda i: (0, 0))],
        out_specs=pl.BlockSpec((tm, 2 * D_MODEL), lambda i: (i, 0)),
        out_shape=jax.ShapeDtypeStruct((n, 2 * D_MODEL), BF16),
        compiler_params=_cparams(("parallel",)),
        name="kvproj",
    )(mem2d, g_mem, wkv)


def _xattn_body(x_ref, rw_ref, gl_ref, kv_ref, wout_ref, gmix_ref, gpre_ref, wq_ref, wo_ref, gpost_ref, o_ref):
    mixed = (jnp.dot(rw_ref[0], wout_ref[0:RW, :], preferred_element_type=F32)
             + jnp.dot(gl_ref[0], wout_ref[RW:, :], preferred_element_type=F32))
    x = x_ref[0] + _rms(mixed, gmix_ref[...], NORM_EPS)
    h = _rms(x, gpre_ref[...], NORM_EPS).astype(BF16)
    q = jnp.dot(h, wq_ref[...], preferred_element_type=F32).astype(BF16)
    heads = range(X_HEADS)
    scores = [lax.dot_general(q[:, hd * X_HD:(hd + 1) * X_HD], kv_ref[0, :, hd * X_HD:(hd + 1) * X_HD],
                              (((1,), (1,)), ((), ())), preferred_element_type=F32) * (X_HD ** -0.5)
              for hd in heads]
    probs = []
    for s in scores:
        e = jnp.exp(s - jnp.max(s, axis=-1, keepdims=True))
        probs.append((e * (1.0 / jnp.sum(e, axis=-1, keepdims=True))).astype(BF16))
    outs = [jnp.dot(probs[hd], kv_ref[0, :, D_MODEL + hd * X_HD:D_MODEL + (hd + 1) * X_HD],
                    preferred_element_type=F32).astype(BF16) for hd in heads]
    att = jnp.dot(jnp.concatenate(outs, axis=1), wo_ref[...], preferred_element_type=F32)
    o_ref[0] = x + _rms(att, gpost_ref[...], NORM_EPS)


def _xattn(x3d, rw3d, gl3d, kv3d, w_out, g_mix_post, g_pre, wq, wo, g_post, tm):
    b, t, _ = x3d.shape
    row = lambda w: pl.BlockSpec((1, tm, w), lambda i, j: (i, j, 0))
    const = lambda r, c: pl.BlockSpec((r, c), lambda i, j: (0, 0))
    return pl.pallas_call(
        _xattn_body,
        grid=(b, t // tm),
        in_specs=[row(D_MODEL), row(RW), row(GH * GV),
                  pl.BlockSpec((1, N_MEM, 2 * D_MODEL), lambda i, j: (i, 0, 0)),
                  const(D_MODEL, D_MODEL), const(1, D_MODEL), const(1, D_MODEL),
                  const(D_MODEL, D_MODEL), const(D_MODEL, D_MODEL), const(1, D_MODEL)],
        out_specs=row(D_MODEL),
        out_shape=jax.ShapeDtypeStruct((b, t, D_MODEL), F32),
        compiler_params=_cparams(("parallel", "parallel")),
        name="xattn",
    )(x3d, rw3d, gl3d, kv3d, w_out, g_mix_post, g_pre, wq, wo, g_post)


def _ffn_body(x_ref, gpre_ref, w1_ref, w2_ref, gpost_ref, o_ref, h_scr, acc_scr):
    j = pl.program_id(1)

    @pl.when(j == 0)
    def _():
        h_scr[...] = _rms(x_ref[...], gpre_ref[...], NORM_EPS).astype(BF16)
        acc_scr[...] = jnp.zeros_like(acc_scr)

    a = jnp.dot(h_scr[...], w1_ref[...], preferred_element_type=F32)
    a = jnp.square(jnp.maximum(a, 0.0)).astype(BF16)
    acc_scr[...] += jnp.dot(a, w2_ref[...], preferred_element_type=F32)

    @pl.when(j == pl.num_programs(1) - 1)
    def _():
        o_ref[...] = x_ref[...] + _rms(acc_scr[...], gpost_ref[...], NORM_EPS)


def _ffn(x2d, g_pre, w1, w2, g_post, tm, tf):
    n = x2d.shape[0]
    return pl.pallas_call(
        _ffn_body,
        grid=(n // tm, D_FF // tf),
        in_specs=[pl.BlockSpec((tm, D_MODEL), lambda i, j: (i, 0)),
                  pl.BlockSpec((1, D_MODEL), lambda i, j: (0, 0)),
                  pl.BlockSpec((D_MODEL, tf), lambda i, j: (0, j)),
                  pl.BlockSpec((tf, D_MODEL), lambda i, j: (j, 0)),
                  pl.BlockSpec((1, D_MODEL), lambda i, j: (0, 0))],
        out_specs=pl.BlockSpec((tm, D_MODEL), lambda i, j: (i, 0)),
        out_shape=jax.ShapeDtypeStruct((n, D_MODEL), F32),
        scratch_shapes=[pltpu.VMEM((tm, D_MODEL), BF16), pltpu.VMEM((tm, D_MODEL), F32)],
        compiler_params=_cparams(("parallel", "arbitrary")),
        name="ffn",
    )(x2d, g_pre, w1, w2, g_post)


def _pack_params(p):
    w_in = p["w_in"]
    g0 = R_COLS
    zeros = lambda n: jnp.zeros((D_MODEL, n), w_in.dtype)
    w_perm = jnp.concatenate([
        w_in[:, :R_COLS], zeros(LORA_W - (R_COLS - 3 * RW)),
        w_in[:, g0:g0 + 2 * GH * GK],
        w_in[:, g0 + 2 * GH * GK:g0 + 2 * GH * GK + GH * GV],
        w_in[:, g0 + 2 * GH * GK + GH * GV + GK_LORA:],
        w_in[:, g0 + 2 * GH * GK + GH * GV:g0 + 2 * GH * GK + GH * GV + GK_LORA], zeros(GKD_W - GK_LORA),
    ], axis=1).astype(BF16)
    row = lambda v: v.reshape(1, -1)
    mu_p, mu_n = p["mu_prev"], p["mu_next"]
    pvec = jnp.concatenate([
        row(p["w0_f"]), row(p["w0_b"]), row(p["a0_f"]), row(p["a0_b"]),
        row(p["k_k"]), row(p["k_a"]), row(p["r_k"]), row(p["lnx_w"]), row(p["lnx_b"]),
        jnp.zeros((PV_ROWS - 9, RW), F32)], axis=0)
    pad_mu = lambda v: jnp.pad(v, (0, C_QK - R_COLS))
    mu = jnp.stack([pad_mu(mu_p), pad_mu(mu_n)], axis=0)
    nhp = RW // LANES

    def lora_pair(wf, wb):
        wf = wf.reshape(-1, nhp, LANES).transpose(1, 0, 2)
        wb = wb.reshape(-1, nhp, LANES).transpose(1, 0, 2)
        z = jnp.zeros_like(wf)
        return jnp.stack([jnp.concatenate([wf, z], axis=1), jnp.concatenate([z, wb], axis=1)],
                         axis=1).astype(BF16)

    ww = lora_pair(p["w2_f"], p["w2_b"])
    wa = lora_pair(p["a2_f"], p["a2_b"])
    wg = jnp.pad(p["g2"], ((0, 2 * LANES - GATE_LORA), (0, 0)))
    wg = wg.reshape(2 * LANES, nhp, LANES).transpose(1, 0, 2).astype(BF16)
    npair = GH // 2

    def gk_pair(w):
        w = jnp.pad(w, ((0, GKD_W - GK_LORA), (0, 0)))
        return w.reshape(GKD_W, npair, LANES).transpose(1, 0, 2)

    wgk = jnp.stack([gk_pair(p["gk2_f"]), gk_pair(p["gk2_b"])], axis=1).astype(BF16)
    gkb = jnp.stack([p["gkb_f"].reshape(npair, 1, LANES), p["gkb_b"].reshape(npair, 1, LANES)], axis=1)
    return dict(
        w_perm=w_perm, pvec=pvec, mu=mu, ww=ww, wa=wa, wg=wg, wgk=wgk, gkb=gkb,
        nw=row(p["gla_norm_w"]),
        g_mix_pre=row(p["g_mix_pre"]), g_mix_post=row(p["g_mix_post"]),
        g_x_pre=row(p["g_x_pre"]), g_x_post=row(p["g_x_post"]), g_mem=row(p["g_mem"]),
        g_ffn_pre=row(p["g_ffn_pre"]), g_ffn_post=row(p["g_ffn_post"]),
        w_out=p["w_out"].astype(BF16), wq=p["wq_x"].astype(BF16), wkv=p["wkv_x"].astype(BF16),
        wo=p["wo_x"].astype(BF16), w1=p["w_ff1"].astype(BF16), w2=p["w_ff2"].astype(BF16))


def _pick(n, pref):
    t = pref
    while n % t:
        t //= 2
    return t


def _trunk(x, mem, pk):
    b, t, _ = x.shape
    n = b * t
    tm = _pick(n, 1024)
    x2d = x.reshape(n, D_MODEL)
    rkv, lora, qk, gv, gg, gkd = _inproj(x2d, pk["g_mix_pre"], pk["w_perm"], pk["mu"], _pick(t, 512), t)
    r3 = lambda a: a.reshape(b, t, a.shape[-1])
    blk = _pick(t // 2, 1024)
    rw = _rwkv(r3(rkv), r3(lora), pk["pvec"], pk["ww"], pk["wa"], pk["wg"], blk)
    gl = _gla(r3(qk), r3(gv), r3(gg), r3(gkd), pk["wgk"], pk["gkb"], pk["nw"], blk)
    nm = mem.shape[0] * mem.shape[1]
    kv = _kvproj(mem.reshape(nm, D_MODEL), pk["g_mem"], pk["wkv"], _pick(nm, 512))
    x2 = _xattn(x, rw, gl, kv.reshape(mem.shape[0], mem.shape[1], 2 * D_MODEL), pk["w_out"],
                pk["g_mix_post"], pk["g_x_pre"], pk["wq"], pk["wo"], pk["g_x_post"], _pick(t, 1024))
    y = _ffn(x2.reshape(n, D_MODEL), pk["g_ffn_pre"], pk["w1"], pk["w2"], pk["g_ffn_post"],
             tm, 1024)
    return y.reshape(b, t, D_MODEL)


def kernel(x_prompt, x_sample, mem_prompt, mem_sample, g_mix_pre, w_in, mu_prev, mu_next, w0_f, w2_f, w0_b, w2_b, a0_f, a2_f, a0_b, a2_b, g2, k_k, k_a, r_k, lnx_w, lnx_b, gk2_f, gkb_f, gk2_b, gkb_b, gla_norm_w, w_out, g_mix_post, g_x_pre, g_mem, wq_x, wkv_x, wo_x, g_x_post, g_ffn_pre, w_ff1, w_ff2, g_ffn_post):
    params = dict(
        g_mix_pre=g_mix_pre, w_in=w_in, mu_prev=mu_prev, mu_next=mu_next, w0_f=w0_f, w2_f=w2_f,
        w0_b=w0_b, w2_b=w2_b, a0_f=a0_f, a2_f=a2_f, a0_b=a0_b, a2_b=a2_b, g2=g2, k_k=k_k, k_a=k_a,
        r_k=r_k, lnx_w=lnx_w, lnx_b=lnx_b, gk2_f=gk2_f, gkb_f=gkb_f, gk2_b=gk2_b, gkb_b=gkb_b,
        gla_norm_w=gla_norm_w, w_out=w_out, g_mix_post=g_mix_post, g_x_pre=g_x_pre, g_mem=g_mem,
        wq_x=wq_x, wkv_x=wkv_x, wo_x=wo_x, g_x_post=g_x_post, g_ffn_pre=g_ffn_pre, w_ff1=w_ff1,
        w_ff2=w_ff2, g_ffn_post=g_ffn_post)
    assert w_in.shape[0] == 1, "single-layer stack expected"
    pk = _pack_params({name: w[0] for name, w in params.items()})
    return (_trunk(x_prompt, mem_prompt, pk), _trunk(x_sample, mem_sample, pk))
```

```python
import functools
import math

import jax
import jax.numpy as jnp
from jax import lax
from jax.experimental import pallas as pl
from jax.experimental.pallas import tpu as pltpu

F32 = jnp.float32
BF16 = jnp.bfloat16

D_MODEL = 1024
RW = 512
RN = 64
DECAY_LORA = 64
AAA_LORA = 64
GATE_LORA = 160
GN_EPS = 64e-5
GH = 4
GV = 128
GK = 64
GK_LORA = 16
GATE_NORM = 16.0
CHUNK = 64
GLA_EPS = 1e-5
R_COLS = 3 * RW + 2 * DECAY_LORA + 2 * AAA_LORA + GATE_LORA
N_MEM = 256
X_HEADS = 4
X_HD = D_MODEL // X_HEADS
D_FF = 4 * D_MODEL
NORM_EPS = 1e-6
CHUNK_SHIFT, RN_SHIFT, GV_SHIFT, GK_SHIFT = (n.bit_length() - 1 for n in (CHUNK, RN, GV, GK))
assert (1 << CHUNK_SHIFT, 1 << RN_SHIFT, 1 << GV_SHIFT, 1 << GK_SHIFT) == (CHUNK, RN, GV, GK)

LANES = 128
CUM_ROWS = 2 * CHUNK
HALO = 16
SHIFT_COLS = 512
LORA_W = 512
GKD_W = 128
C_RKV, C_LORA, C_QK, C_GV, C_GG, C_GKD = 0, 1536, 2048, 2560, 3072, 3584
N_PROJ = 3712
VMEM_LIMIT = 56 * 1024 * 1024


def _cparams(sem):
    return pltpu.CompilerParams(dimension_semantics=sem, vmem_limit_bytes=VMEM_LIMIT)


def _mm(a, b):
    return jnp.dot(a.astype(BF16), b.astype(BF16), preferred_element_type=F32)


def _mm_nt(a, b):
    return lax.dot_general(a.astype(BF16), b.astype(BF16), (((1,), (1,)), ((), ())),
                           preferred_element_type=F32)


def _mm_tn(a, b):
    return lax.dot_general(a.astype(BF16), b.astype(BF16), (((0,), (0,)), ((), ())),
                           preferred_element_type=F32)


def _mm_lhs01(m01, x, terms):
    acc = None
    rem = x
    for i in range(terms):
        piece = rem.astype(BF16)
        part = jnp.dot(m01, piece, preferred_element_type=F32)
        acc = part if acc is None else acc + part
        if i + 1 < terms:
            rem = rem - piece.astype(F32)
    return acc


def _rms(x, g, eps):
    return x * lax.rsqrt(jnp.mean(x * x, axis=-1, keepdims=True) + eps) * g


def _sigmoid(x):
    return 1.0 / (1.0 + jnp.exp(-x))


def _iota2(shape, dim):
    return lax.broadcasted_iota(jnp.int32, shape, dim)


def _chunk_tri(n, reverse):
    t = _iota2((n, n), 0)
    s = _iota2((n, n), 1)
    same = (t >> CHUNK_SHIFT) == (s >> CHUNK_SHIFT)
    tri = (s >= t) if reverse else (s <= t)
    return jnp.where(same & tri, 1.0, 0.0).astype(BF16)


def _chunk_cumsum(x, reverse, terms):
    tri = _chunk_tri(CUM_ROWS, reverse)
    parts = [_mm_lhs01(tri, x[i:i + CUM_ROWS], terms) for i in range(0, x.shape[0], CUM_ROWS)]
    return parts[0] if len(parts) == 1 else jnp.concatenate(parts, axis=0)


def _inproj_body(x_ref, xp_ref, xn_ref, g_ref, w_ref, mu_ref, rkv_ref, lora_ref, qk_ref, gv_ref, gg_ref,
                 gkd_ref, *, tiles_per_seq):
    tm = x_ref.shape[0]
    i = pl.program_id(0)
    g = g_ref[...]
    h = _rms(x_ref[...], g, NORM_EPS).astype(BF16)
    h_prev = _rms(xp_ref[...], g, NORM_EPS).astype(BF16)
    h_next = _rms(xn_ref[...], g, NORM_EPS).astype(BF16)
    h_ext = jnp.concatenate([h_prev, h, h_next], axis=0)
    pos = i % tiles_per_seq
    has_prev = pos > 0
    has_next = pos < tiles_per_seq - 1
    rid = _iota2((tm, SHIFT_COLS), 0)

    def proj(lo, hi):
        return jnp.dot(h, w_ref[:, lo:hi], preferred_element_type=F32)

    for lo in range(C_RKV, C_QK, SHIFT_COLS):
        p_ext = jnp.dot(h_ext, w_ref[:, lo:lo + SHIFT_COLS], preferred_element_type=F32)
        p = p_ext[HALO:HALO + tm, :]
        p_before = p_ext[HALO - 1:HALO, :]
        p_after = p_ext[HALO + tm:HALO + tm + 1, :]
        prev = jnp.where(rid == 0, jnp.where(has_prev, p_before, 0.0), pltpu.roll(p, 1, 0))
        nxt = jnp.where(rid == tm - 1, jnp.where(has_next, p_after, 0.0), pltpu.roll(p, tm - 1, 0))
        mu = mu_ref[:, lo:lo + SHIFT_COLS]
        shifted = p + mu[0:1, :] * (prev - p) + mu[1:2, :] * (nxt - p)
        if lo < C_LORA:
            rkv_ref[:, lo:lo + SHIFT_COLS] = shifted
        else:
            lora_ref[:, lo - C_LORA:lo - C_LORA + SHIFT_COLS] = shifted
    qk_ref[...] = proj(C_QK, C_GV)
    gv_ref[...] = proj(C_GV, C_GG)
    gg_ref[...] = proj(C_GG, C_GKD)
    gkd_ref[...] = proj(C_GKD, N_PROJ)


def _inproj(x2d, g, w_perm, mu, tm, seq_len):
    n = x2d.shape[0]
    assert seq_len % tm == 0 and tm % HALO == 0
    widths = (C_LORA - C_RKV, C_QK - C_LORA, C_GV - C_QK, C_GG - C_GV, C_GKD - C_GG, N_PROJ - C_GKD)
    hb = tm // HALO
    last = n // HALO - 1
    body = functools.partial(_inproj_body, tiles_per_seq=seq_len // tm)
    return pl.pallas_call(
        body,
        grid=(n // tm,),
        in_specs=[pl.BlockSpec((tm, D_MODEL), lambda i: (i, 0)),
                  pl.BlockSpec((HALO, D_MODEL), lambda i: (jnp.maximum(i * hb - 1, 0), 0)),
                  pl.BlockSpec((HALO, D_MODEL), lambda i: (jnp.minimum((i + 1) * hb, last), 0)),
                  pl.BlockSpec((1, D_MODEL), lambda i: (0, 0)),
                  pl.BlockSpec((D_MODEL, N_PROJ), lambda i: (0, 0)),
                  pl.BlockSpec((2, C_QK), lambda i: (0, 0))],
        out_specs=[pl.BlockSpec((tm, w), lambda i: (i, 0)) for w in widths],
        out_shape=[jax.ShapeDtypeStruct((n, w), F32) for w in widths],
        compiler_params=_cparams(("parallel",)),
        name="inproj",
    )(x2d, x2d, x2d, g, w_perm, mu)


PV_W0_F, PV_W0_B, PV_A0_F, PV_A0_B, PV_KK, PV_KA, PV_RK, PV_LNW, PV_LNB = range(9)
PV_ROWS = 16


def _rwkv_body(r_ref, k_ref, v_ref, lo_ref, pv_ref, ww_ref, wa_ref, wg_ref, out_ref,
               y_scr, bon_scr, s_scr, *, seq_len, blk):
    C = CHUNK
    nc = blk // C
    nblk = seq_len // blk
    half = nblk // 2
    pv = pv_ref[...]
    prow = lambda i: pv[i:i + 1, :]

    lane = _iota2((1, LANES), 1)
    head_mask = [lane < RN, lane >= RN]
    li = _iota2((LANES, LANES), 0)
    lj = _iota2((LANES, LANES), 1)
    same_head = (li >> RN_SHIFT) == (lj >> RN_SHIFT)
    seg01 = jnp.where(same_head, 1.0, 0.0).astype(BF16)
    row_c = _iota2((C, LANES), 0)
    col_c = _iota2((C, LANES), 1) & (C - 1)
    eye_p = jnp.where(row_c == col_c, 1.0, 0.0)
    zeros_cl = jnp.zeros((C, LANES), BF16)
    zeros_ll = jnp.zeros((LANES, LANES), BF16)

    def seg_sum(x):
        return _mm(x, seg01)

    def block_inputs(t0, reverse):
        d = 1 if reverse else 0
        rows = pl.ds(t0, blk)
        r, k, v, lo = r_ref[0, rows, :], k_ref[0, rows, :], v_ref[0, rows, :], lo_ref[0, rows, :]
        zw = _mm(jnp.tanh(lo[:, 0:LANES]), ww_ref[0, d])
        za = _mm(lo[:, LANES:2 * LANES], wa_ref[0, d])
        lw = -_sigmoid(prow(PV_W0_F + d) + zw) * math.exp(-0.5)
        a = _sigmoid(prow(PV_A0_F + d) + za)
        kd = k * (1.0 + (a - 1.0) * prow(PV_KA))
        kk = k * prow(PV_KK)
        kk = kk * lax.rsqrt(seg_sum(kk * kk) + 1e-12)
        bonus = seg_sum(r * kd * prow(PV_RK)) * v
        g = _chunk_cumsum(lw, reverse, 2)
        return dict(r=r, v=v, kd=kd, kk=kk, a=a, lw=lw, g=g, bonus=bonus, lo=lo)

    def bd(x):
        xb = x.astype(BF16)
        zero = jnp.zeros_like(xb)
        return jnp.concatenate([jnp.where(head_mask[0], xb, zero), jnp.where(head_mask[1], xb, zero)], axis=0)

    def block_terms(ctxs):
        tri_s, tri_i, ch, chains = {}, {}, {}, []
        for d, q, order, reverse in ctxs:
            tri_s[d] = (col_c > row_c) if reverse else (col_c < row_c)
            tri_i[d] = (col_c >= row_c) if reverse else (col_c <= row_c)
            for c in order:
                sl = slice(c * C, (c + 1) * C)
                g, lw, kk = q["g"][sl], q["lw"][sl], q["kk"][sl]
                r, v, kd, a = q["r"][sl], q["v"][sl], q["kd"][sl], q["a"][sl]
                g_tot = g[0:1, :] if reverse else g[C - 1:C, :]
                eng = jnp.exp(-g)
                beta = kk * a
                dec = jnp.exp(g_tot - g)
                mxu = lambda x: x.astype(BF16)
                ch[d, c] = dict(v=mxu(v), rt=r * jnp.exp(g), at=mxu(-kk * jnp.exp(g - lw)), bt=mxu(beta * eng),
                                kt=mxu(kd * eng), bh=mxu(beta * dec), kh=mxu(kd * dec), gam=jnp.exp(g_tot))
                chains.append((d, c))
        a_ab, a_ak, a_rbk = {}, {}, {}
        for k in chains:
            e = ch[k]
            gm = _mm_nt(jnp.concatenate([e["at"], e["rt"].astype(BF16)], axis=0),
                        jnp.concatenate([bd(e["bt"]), bd(e["kt"])], axis=0))
            a_ab[k] = jnp.where(tri_s[k[0]], gm[:C, :LANES], 0.0).astype(BF16)
            a_ak[k] = jnp.where(tri_s[k[0]], gm[:C, LANES:], 0.0).astype(BF16)
            a_rbk[k] = jnp.concatenate([jnp.where(tri_i[k[0]], gm[C:, :LANES], 0.0),
                                        jnp.where(tri_i[k[0]], gm[C:, LANES:], 0.0)], axis=1).astype(BF16)
        tm = {k: eye_p + jnp.where((row_c >> 1) == (col_c >> 1), a_ab[k], 0.0) for k in chains}
        for lvl in range(1, 6):
            same = (row_c >> (lvl + 1)) == (col_c >> (lvl + 1))
            lower = (((row_c >> lvl) & 1) == 1) & (((col_c >> lvl) & 1) == 0)
            upper = (((row_c >> lvl) & 1) == 0) & (((col_c >> lvl) & 1) == 1)
            off = {d: same & (upper if reverse else lower) for d, _, _, reverse in ctxs}
            left = {k: _mm(tm[k], bd(jnp.where(off[k[0]], a_ab[k], zeros_cl))) for k in chains}
            tm = {k: tm[k] + _mm(left[k], bd(tm[k])) for k in chains}
        akv = {k: _mm(a_ak[k], bd(ch[k]["v"])) for k in chains}
        wu = {k: _mm(tm[k], jnp.concatenate([bd(ch[k]["at"]), bd(akv[k])], axis=1)).astype(BF16) for k in chains}

        def chunk_terms(k):
            e = ch[k]
            rhs = jnp.concatenate(
                [jnp.concatenate([bd(wu[k][:, :LANES]), bd(wu[k][:, LANES:])], axis=1),
                 jnp.concatenate([zeros_ll, bd(e["v"])], axis=1)], axis=0)
            qy = _mm(a_rbk[k], rhs)
            lhs = jnp.concatenate([wu[k], jnp.concatenate([zeros_cl, e["v"]], axis=1)], axis=0)
            pn = _mm_tn(lhs, jnp.concatenate([e["bh"], e["kh"]], axis=0))
            p_off = jnp.where(same_head, pn[:LANES], 0.0).astype(BF16)
            n0t = jnp.where(same_head, pn[LANES:], 0.0)
            return e["rt"] + qy[:, :LANES], qy[:, LANES:], e["gam"], p_off, n0t

        return chunk_terms

    def run_pair(tf, tb):
        qf = block_inputs(tf, False)
        qb = block_inputs(tb, True)
        ctxs = [(0, qf, list(range(nc)), False), (1, qb, list(range(nc - 1, -1, -1)), True)]
        chunk_terms = block_terms(ctxs)
        s = [s_scr[0], s_scr[1]]
        ys = [[None] * nc, [None] * nc]
        terms = {(d, order[0]): chunk_terms((d, order[0])) for d, _, order, _ in ctxs}
        for step in range(nc):
            if step + 1 < nc:
                for d, _, order, _ in ctxs:
                    terms[d, order[step + 1]] = chunk_terms((d, order[step + 1]))
            for d, _, order, _ in ctxs:
                qh, y0, gam, p_off, n0t = terms[d, order[step]]
                sb = s[d].astype(BF16)
                ys[d][order[step]] = _mm_nt(qh, sb) + y0
                s[d] = (s[d] * gam + n0t) + jnp.dot(sb, p_off, preferred_element_type=F32)
        s_scr[0] = s[0]
        s_scr[1] = s[1]
        return qf, jnp.concatenate(ys[0], axis=0), qb, jnp.concatenate(ys[1], axis=0)

    def finish(q, y, bonus, t0):
        mean = seg_sum(y) * (1.0 / RN)
        dy = y - mean
        var = seg_sum(dy * dy) * (1.0 / RN)
        gn = dy * lax.rsqrt(var + GN_EPS) * prow(PV_LNW) + prow(PV_LNB)
        gate = _mm(_sigmoid(q["lo"][:, 2 * LANES:]), wg_ref[0])
        out_ref[0, pl.ds(t0, blk), :] = ((gn + bonus) * gate).astype(out_ref.dtype)

    s_scr[...] = jnp.zeros(s_scr.shape, F32)

    def block_starts(i):
        return pl.multiple_of(i * blk, blk), pl.multiple_of((nblk - 1 - i) * blk, blk)

    def first_half(i, carry):
        tf, tb = block_starts(i)
        qf, yf, qb, yb = run_pair(tf, tb)
        y_scr[pl.ds(tf, blk), :] = yf
        bon_scr[pl.ds(tf, blk), :] = qf["bonus"]
        y_scr[pl.ds(tb, blk), :] = yb
        bon_scr[pl.ds(tb, blk), :] = qb["bonus"]
        return carry

    def second_half(i, carry):
        tf, tb = block_starts(i)
        qf, yf, qb, yb = run_pair(tf, tb)
        finish(qf, yf + y_scr[pl.ds(tf, blk), :], qf["bonus"] + bon_scr[pl.ds(tf, blk), :], tf)
        finish(qb, yb + y_scr[pl.ds(tb, blk), :], qb["bonus"] + bon_scr[pl.ds(tb, blk), :], tb)
        return carry

    lax.fori_loop(0, half, first_half, 0)
    lax.fori_loop(half, nblk, second_half, 0)


def _rwkv(rkv, lora, pvec, ww, wa, wg, blk):
    b, t, _ = rkv.shape
    assert t % (2 * blk) == 0 and blk % CUM_ROWS == 0
    nhp = RW // LANES
    col = lambda off: pl.BlockSpec((1, t, LANES), lambda i, j, off=off: (i, 0, off + j))
    body = functools.partial(_rwkv_body, seq_len=t, blk=blk)
    return pl.pallas_call(
        body,
        grid=(b, nhp),
        in_specs=[col(0), col(nhp), col(2 * nhp),
                  pl.BlockSpec((1, t, LORA_W), lambda i, j: (i, 0, 0)),
                  pl.BlockSpec((PV_ROWS, LANES), lambda i, j: (0, j)),
                  pl.BlockSpec((1, 2, LANES, LANES), lambda i, j: (j, 0, 0, 0)),
                  pl.BlockSpec((1, 2, LANES, LANES), lambda i, j: (j, 0, 0, 0)),
                  pl.BlockSpec((1, 2 * LANES, LANES), lambda i, j: (j, 0, 0))],
        out_specs=pl.BlockSpec((1, t, LANES), lambda i, j: (i, 0, j)),
        out_shape=jax.ShapeDtypeStruct((b, t, RW), BF16),
        scratch_shapes=[pltpu.VMEM((t, LANES), F32), pltpu.VMEM((t, LANES), F32),
                        pltpu.VMEM((2, LANES, LANES), F32)],
        compiler_params=_cparams(("parallel", "parallel")),
        name="rwkv",
    )(rkv, rkv, rkv, lora, pvec, ww, wa, wg)


def _gla_body(q_ref, k_ref, v_ref, gg_ref, gkd_ref, wgk_ref, gkb_ref, nw_ref, out_ref,
              o_scr, s_scr, *, seq_len, blk):
    C = CHUNK
    nc = blk // C
    nblk = seq_len // blk
    half = nblk // 2
    lane = _iota2((1, LANES), 1)
    head_mask = [lane < GK, lane >= GK]
    si = _iota2((2 * GV, LANES), 0)
    sj = _iota2((2 * GV, LANES), 1)
    same_head = (si >> GV_SHIFT) == (sj >> GK_SHIFT)
    row_c = _iota2((C, LANES), 0)
    col_c = _iota2((C, LANES), 1) & (C - 1)
    zeros_cv = jnp.zeros((C, GV), BF16)

    def run_pair(tf, tb):
        ctxs = [dict(d=0, rows=pl.ds(tf, blk), reverse=False, order=list(range(nc))),
                dict(d=1, rows=pl.ds(tb, blk), reverse=True, order=list(range(nc - 1, -1, -1)))]
        for cx in ctxs:
            cx["z"] = _mm(gkd_ref[0, cx["rows"], :], wgk_ref[0, cx["d"]]) + gkb_ref[0, cx["d"]]
        for cx in ctxs:
            z = cx["z"]
            lg = (jnp.minimum(z, 0.0) - jnp.log1p(jnp.exp(-jnp.abs(z)))) * (1.0 / GATE_NORM)
            cx["bcum"] = _chunk_cumsum(lg, cx["reverse"], 3)
        ch, chains, tri = {}, [], {}
        for cx in ctxs:
            d, reverse = cx["d"], cx["reverse"]
            tri[d] = (col_c >= row_c) if reverse else (col_c <= row_c)
            q = q_ref[0, cx["rows"], :] * (GK ** -0.5)
            k = k_ref[0, cx["rows"], :]
            v = v_ref[0, cx["rows"], :]
            for c in cx["order"]:
                sl = slice(c * C, (c + 1) * C)
                bc = cx["bcum"][sl]
                b_tot = bc[0:1, :] if reverse else bc[C - 1:C, :]
                mxu = lambda x: x.astype(BF16)
                ch[d, c] = dict(q_in=mxu(q[sl] * jnp.exp(bc)), k_in=mxu(k[sl] * jnp.exp(-bc)),
                                k_dec=mxu(k[sl] * jnp.exp(b_tot - bc)), v=mxu(v[sl]), gam=jnp.exp(b_tot))
                chains.append((d, c))
        att, intra, ds = {}, {}, {}
        for kk in chains:
            k_in = ch[kk]["k_in"]
            zero = jnp.zeros_like(k_in)
            k_bd = jnp.concatenate([jnp.where(head_mask[0], k_in, zero), jnp.where(head_mask[1], k_in, zero)], axis=0)
            att[kk] = jnp.where(tri[kk[0]], _mm_nt(ch[kk]["q_in"], k_bd), 0.0).astype(BF16)
        for kk in chains:
            v = ch[kk]["v"]
            v_bd = jnp.concatenate([jnp.concatenate([v[:, :GV], zeros_cv], axis=1),
                                    jnp.concatenate([zeros_cv, v[:, GV:]], axis=1)], axis=0)
            intra[kk] = _mm(att[kk], v_bd)
        for kk in chains:
            ds[kk] = jnp.where(same_head, _mm_tn(ch[kk]["v"], ch[kk]["k_dec"]), 0.0)
        s_in = {}
        for cx in ctxs:
            d = cx["d"]
            s = s_scr[d]
            for c in cx["order"]:
                s_in[d, c] = s
                s = s * ch[d, c]["gam"] + ds[d, c]
            s_scr[d] = s
        outs = {kk: _mm_nt(ch[kk]["q_in"], s_in[kk]) + intra[kk] for kk in chains}
        return (jnp.concatenate([outs[0, c] for c in range(nc)], axis=0),
                jnp.concatenate([outs[1, c] for c in range(nc)], axis=0))

    def finish(o, t0):
        gg = gg_ref[0, pl.ds(t0, blk), :]
        res = []
        for h in range(2):
            oh = o[:, h * GV:(h + 1) * GV]
            oh = oh * lax.rsqrt(jnp.mean(oh * oh, axis=-1, keepdims=True) + GLA_EPS) * nw_ref[...]
            gh = gg[:, h * GV:(h + 1) * GV]
            res.append(oh * (gh * _sigmoid(gh)))
        out_ref[0, pl.ds(t0, blk), :] = jnp.concatenate(res, axis=1).astype(out_ref.dtype)

    s_scr[...] = jnp.zeros(s_scr.shape, F32)

    def block_starts(i):
        return pl.multiple_of(i * blk, blk), pl.multiple_of((nblk - 1 - i) * blk, blk)

    def first_half(i, carry):
        tf, tb = block_starts(i)
        of, ob = run_pair(tf, tb)
        o_scr[pl.ds(tf, blk), :] = of
        o_scr[pl.ds(tb, blk), :] = ob
        return carry

    def second_half(i, carry):
        tf, tb = block_starts(i)
        of, ob = run_pair(tf, tb)
        finish(of + o_scr[pl.ds(tf, blk), :], tf)
        finish(ob + o_scr[pl.ds(tb, blk), :], tb)
        return carry

    lax.fori_loop(0, half, first_half, 0)
    lax.fori_loop(half, nblk, second_half, 0)


def _gla(qk, gv, gg, gkd, wgk, gkb, nw, blk):
    b, t, _ = qk.shape
    assert t % (2 * blk) == 0 and blk % CUM_ROWS == 0
    npair = GH // 2
    body = functools.partial(_gla_body, seq_len=t, blk=blk)
    return pl.pallas_call(
        body,
        grid=(b, npair),
        in_specs=[pl.BlockSpec((1, t, LANES), lambda i, j: (i, 0, j)),
                  pl.BlockSpec((1, t, LANES), lambda i, j: (i, 0, npair + j)),
                  pl.BlockSpec((1, t, 2 * GV), lambda i, j: (i, 0, j)),
                  pl.BlockSpec((1, t, 2 * GV), lambda i, j: (i, 0, j)),
                  pl.BlockSpec((1, t, GKD_W), lambda i, j: (i, 0, 0)),
                  pl.BlockSpec((1, 2, GKD_W, LANES), lambda i, j: (j, 0, 0, 0)),
                  pl.BlockSpec((1, 2, 1, LANES), lambda i, j: (j, 0, 0, 0)),
                  pl.BlockSpec((1, GV), lambda i, j: (0, 0))],
        out_specs=pl.BlockSpec((1, t, 2 * GV), lambda i, j: (i, 0, j)),
        out_shape=jax.ShapeDtypeStruct((b, t, GH * GV), BF16),
        scratch_shapes=[pltpu.VMEM((t, 2 * GV), F32), pltpu.VMEM((2, 2 * GV, LANES), F32)],
        compiler_params=_cparams(("parallel", "parallel")),
        name="gla",
    )(qk, qk, gv, gg, gkd, wgk, gkb, nw)


def _kvproj_body(m_ref, g_ref, w_ref, kv_ref):
    h = _rms(m_ref[...], g_ref[...], NORM_EPS).astype(BF16)
    kv_ref[...] = jnp.dot(h, w_ref[...], preferred_element_type=F32).astype(kv_ref.dtype)


def _kvproj(mem2d, g_mem, wkv, tm):
    n = mem2d.shape[0]
    return pl.pallas_call(
        _kvproj_body,
        grid=(n // tm,),
        in_specs=[pl.BlockSpec((tm, D_MODEL), lambda i: (i, 0)),
                  pl.BlockSpec((1, D_MODEL), lambda i: (0, 0)),
                  pl.BlockSpec((D_MODEL, 2 * D_MODEL), lambda i: (0, 0))],
        out_specs=pl.BlockSpec((tm, 2 * D_MODEL), lambda i: (i, 0)),
        out_shape=jax.ShapeDtypeStruct((n, 2 * D_MODEL), BF16),
        compiler_params=_cparams(("parallel",)),
        name="kvproj",
    )(mem2d, g_mem, wkv)


def _xattn_body(x_ref, rw_ref, gl_ref, kv_ref, wout_ref, gmix_ref, gpre_ref, wq_ref, wo_ref, gpost_ref, o_ref):
    mixed = (jnp.dot(rw_ref[0], wout_ref[0:RW, :], preferred_element_type=F32)
             + jnp.dot(gl_ref[0], wout_ref[RW:, :], preferred_element_type=F32))
    x = x_ref[0] + _rms(mixed, gmix_ref[...], NORM_EPS)
    h = _rms(x, gpre_ref[...], NORM_EPS).astype(BF16)
    q = jnp.dot(h, wq_ref[...], preferred_element_type=F32).astype(BF16)
    heads = range(X_HEADS)
    scores = [lax.dot_general(q[:, hd * X_HD:(hd + 1) * X_HD], kv_ref[0, :, hd * X_HD:(hd + 1) * X_HD],
                              (((1,), (1,)), ((), ())), preferred_element_type=F32) * (X_HD ** -0.5)
              for hd in heads]
    probs = []
    for s in scores:
        e = jnp.exp(s - jnp.max(s, axis=-1, keepdims=True))
        probs.append((e * (1.0 / jnp.sum(e, axis=-1, keepdims=True))).astype(BF16))
    outs = [jnp.dot(probs[hd], kv_ref[0, :, D_MODEL + hd * X_HD:D_MODEL + (hd + 1) * X_HD],
                    preferred_element_type=F32).astype(BF16) for hd in heads]
    att = jnp.dot(jnp.concatenate(outs, axis=1), wo_ref[...], preferred_element_type=F32)
    o_ref[0] = x + _rms(att, gpost_ref[...], NORM_EPS)


def _xattn(x3d, rw3d, gl3d, kv3d, w_out, g_mix_post, g_pre, wq, wo, g_post, tm):
    b, t, _ = x3d.shape
    row = lambda w: pl.BlockSpec((1, tm, w), lambda i, j: (i, j, 0))
    const = lambda r, c: pl.BlockSpec((r, c), lambda i, j: (0, 0))
    return pl.pallas_call(
        _xattn_body,
        grid=(b, t // tm),
        in_specs=[row(D_MODEL), row(RW), row(GH * GV),
                  pl.BlockSpec((1, N_MEM, 2 * D_MODEL), lambda i, j: (i, 0, 0)),
                  const(D_MODEL, D_MODEL), const(1, D_MODEL), const(1, D_MODEL),
                  const(D_MODEL, D_MODEL), const(D_MODEL, D_MODEL), const(1, D_MODEL)],
        out_specs=row(D_MODEL),
        out_shape=jax.ShapeDtypeStruct((b, t, D_MODEL), F32),
        compiler_params=_cparams(("parallel", "parallel")),
        name="xattn",
    )(x3d, rw3d, gl3d, kv3d, w_out, g_mix_post, g_pre, wq, wo, g_post)


def _ffn_body(x_ref, gpre_ref, w1_ref, w2_ref, gpost_ref, o_ref, h_scr, acc_scr):
    j = pl.program_id(1)

    @pl.when(j == 0)
    def _():
        h_scr[...] = _rms(x_ref[...], gpre_ref[...], NORM_EPS).astype(BF16)
        acc_scr[...] = jnp.zeros_like(acc_scr)

    a = jnp.dot(h_scr[...], w1_ref[...], preferred_element_type=F32)
    a = jnp.square(jnp.maximum(a, 0.0)).astype(BF16)
    acc_scr[...] += jnp.dot(a, w2_ref[...], preferred_element_type=F32)

    @pl.when(j == pl.num_programs(1) - 1)
    def _():
        o_ref[...] = x_ref[...] + _rms(acc_scr[...], gpost_ref[...], NORM_EPS)


def _ffn(x2d, g_pre, w1, w2, g_post, tm, tf):
    n = x2d.shape[0]
    return pl.pallas_call(
        _ffn_body,
        grid=(n // tm, D_FF // tf),
        in_specs=[pl.BlockSpec((tm, D_MODEL), lambda i, j: (i, 0)),
                  pl.BlockSpec((1, D_MODEL), lambda i, j: (0, 0)),
                  pl.BlockSpec((D_MODEL, tf), lambda i, j: (0, j)),
                  pl.BlockSpec((tf, D_MODEL), lambda i, j: (j, 0)),
                  pl.BlockSpec((1, D_MODEL), lambda i, j: (0, 0))],
        out_specs=pl.BlockSpec((tm, D_MODEL), lambda i, j: (i, 0)),
        out_shape=jax.ShapeDtypeStruct((n, D_MODEL), F32),
        scratch_shapes=[pltpu.VMEM((tm, D_MODEL), BF16), pltpu.VMEM((tm, D_MODEL), F32)],
        compiler_params=_cparams(("parallel", "arbitrary")),
        name="ffn",
    )(x2d, g_pre, w1, w2, g_post)


def _pack_params(p):
    w_in = p["w_in"]
    g0 = R_COLS
    zeros = lambda n: jnp.zeros((D_MODEL, n), w_in.dtype)
    w_perm = jnp.concatenate([
        w_in[:, :R_COLS], zeros(LORA_W - (R_COLS - 3 * RW)),
        w_in[:, g0:g0 + 2 * GH * GK],
        w_in[:, g0 + 2 * GH * GK:g0 + 2 * GH * GK + GH * GV],
        w_in[:, g0 + 2 * GH * GK + GH * GV + GK_LORA:],
        w_in[:, g0 + 2 * GH * GK + GH * GV:g0 + 2 * GH * GK + GH * GV + GK_LORA], zeros(GKD_W - GK_LORA),
    ], axis=1).astype(BF16)
    row = lambda v: v.reshape(1, -1)
    mu_p, mu_n = p["mu_prev"], p["mu_next"]
    pvec = jnp.concatenate([
        row(p["w0_f"]), row(p["w0_b"]), row(p["a0_f"]), row(p["a0_b"]),
        row(p["k_k"]), row(p["k_a"]), row(p["r_k"]), row(p["lnx_w"]), row(p["lnx_b"]),
        jnp.zeros((PV_ROWS - 9, RW), F32)], axis=0)
    pad_mu = lambda v: jnp.pad(v, (0, C_QK - R_COLS))
    mu = jnp.stack([pad_mu(mu_p), pad_mu(mu_n)], axis=0)
    nhp = RW // LANES

    def lora_pair(wf, wb):
        wf = wf.reshape(-1, nhp, LANES).transpose(1, 0, 2)
        wb = wb.reshape(-1, nhp, LANES).transpose(1, 0, 2)
        z = jnp.zeros_like(wf)
        return jnp.stack([jnp.concatenate([wf, z], axis=1), jnp.concatenate([z, wb], axis=1)],
                         axis=1).astype(BF16)

    ww = lora_pair(p["w2_f"], p["w2_b"])
    wa = lora_pair(p["a2_f"], p["a2_b"])
    wg = jnp.pad(p["g2"], ((0, 2 * LANES - GATE_LORA), (0, 0)))
    wg = wg.reshape(2 * LANES, nhp, LANES).transpose(1, 0, 2).astype(BF16)
    npair = GH // 2

    def gk_pair(w):
        w = jnp.pad(w, ((0, GKD_W - GK_LORA), (0, 0)))
        return w.reshape(GKD_W, npair, LANES).transpose(1, 0, 2)

    wgk = jnp.stack([gk_pair(p["gk2_f"]), gk_pair(p["gk2_b"])], axis=1).astype(BF16)
    gkb = jnp.stack([p["gkb_f"].reshape(npair, 1, LANES), p["gkb_b"].reshape(npair, 1, LANES)], axis=1)
    return dict(
        w_perm=w_perm, pvec=pvec, mu=mu, ww=ww, wa=wa, wg=wg, wgk=wgk, gkb=gkb,
        nw=row(p["gla_norm_w"]),
        g_mix_pre=row(p["g_mix_pre"]), g_mix_post=row(p["g_mix_post"]),
        g_x_pre=row(p["g_x_pre"]), g_x_post=row(p["g_x_post"]), g_mem=row(p["g_mem"]),
        g_ffn_pre=row(p["g_ffn_pre"]), g_ffn_post=row(p["g_ffn_post"]),
        w_out=p["w_out"].astype(BF16), wq=p["wq_x"].astype(BF16), wkv=p["wkv_x"].astype(BF16),
        wo=p["wo_x"].astype(BF16), w1=p["w_ff1"].astype(BF16), w2=p["w_ff2"].astype(BF16))


def _pick(n, pref):
    t = pref
    while n % t:
        t //= 2
    return t


def _trunk(x, mem, pk):
    b, t, _ = x.shape
    n = b * t
    tm = _pick(n, 1024)
    x2d = x.reshape(n, D_MODEL)
    rkv, lora, qk, gv, gg, gkd = _inproj(x2d, pk["g_mix_pre"], pk["w_perm"], pk["mu"], _pick(t, 512), t)
    r3 = lambda a: a.reshape(b, t, a.shape[-1])
    blk = _pick(t // 2, 1024)
    rw = _rwkv(r3(rkv), r3(lora), pk["pvec"], pk["ww"], pk["wa"], pk["wg"], blk)
    gl = _gla(r3(qk), r3(gv), r3(gg), r3(gkd), pk["wgk"], pk["gkb"], pk["nw"], blk)
    nm = mem.shape[0] * mem.shape[1]
    kv = _kvproj(mem.reshape(nm, D_MODEL), pk["g_mem"], pk["wkv"], _pick(nm, 512))
    x2 = _xattn(x, rw, gl, kv.reshape(mem.shape[0], mem.shape[1], 2 * D_MODEL), pk["w_out"],
                pk["g_mix_post"], pk["g_x_pre"], pk["wq"], pk["wo"], pk["g_x_post"], _pick(t, 1024))
    y = _ffn(x2.reshape(n, D_MODEL), pk["g_ffn_pre"], pk["w1"], pk["w2"], pk["g_ffn_post"],
             tm, 1024)
    return y.reshape(b, t, D_MODEL)


def kernel(x_prompt, x_sample, mem_prompt, mem_sample, g_mix_pre, w_in, mu_prev, mu_next, w0_f, w2_f, w0_b, w2_b, a0_f, a2_f, a0_b, a2_b, g2, k_k, k_a, r_k, lnx_w, lnx_b, gk2_f, gkb_f, gk2_b, gkb_b, gla_norm_w, w_out, g_mix_post, g_x_pre, g_mem, wq_x, wkv_x, wo_x, g_x_post, g_ffn_pre, w_ff1, w_ff2, g_ffn_post):
    params = dict(
        g_mix_pre=g_mix_pre, w_in=w_in, mu_prev=mu_prev, mu_next=mu_next, w0_f=w0_f, w2_f=w2_f,
        w0_b=w0_b, w2_b=w2_b, a0_f=a0_f, a2_f=a2_f, a0_b=a0_b, a2_b=a2_b, g2=g2, k_k=k_k, k_a=k_a,
        r_k=r_k, lnx_w=lnx_w, lnx_b=lnx_b, gk2_f=gk2_f, gkb_f=gkb_f, gk2_b=gk2_b, gkb_b=gkb_b,
        gla_norm_w=gla_norm_w, w_out=w_out, g_mix_post=g_mix_post, g_x_pre=g_x_pre, g_mem=g_mem,
        wq_x=wq_x, wkv_x=wkv_x, wo_x=wo_x, g_x_post=g_x_post, g_ffn_pre=g_ffn_pre, w_ff1=w_ff1,
        w_ff2=w_ff2, g_ffn_post=g_ffn_post)
    assert w_in.shape[0] == 1, "single-layer stack expected"
    pk = _pack_params({name: w[0] for name, w in params.items()})
    return (_trunk(x_prompt, mem_prompt, pk), _trunk(x_sample, mem_sample, pk))
```

```python
import functools
import math

import jax
import jax.numpy as jnp
from jax import lax
from jax.experimental import pallas as pl
from jax.experimental.pallas import tpu as pltpu

F32 = jnp.float32
BF16 = jnp.bfloat16

D_MODEL = 1024
RW = 512
RN = 64
DECAY_LORA = 64
AAA_LORA = 64
GATE_LORA = 160
GN_EPS = 64e-5
GH = 4
GV = 128
GK = 64
GK_LORA = 16
GATE_NORM = 16.0
CHUNK = 64
GLA_EPS = 1e-5
R_COLS = 3 * RW + 2 * DECAY_LORA + 2 * AAA_LORA + GATE_LORA
N_MEM = 256
X_HEADS = 4
X_HD = D_MODEL // X_HEADS
D_FF = 4 * D_MODEL
NORM_EPS = 1e-6
CHUNK_SHIFT, RN_SHIFT, GV_SHIFT, GK_SHIFT = (n.bit_length() - 1 for n in (CHUNK, RN, GV, GK))
assert (1 << CHUNK_SHIFT, 1 << RN_SHIFT, 1 << GV_SHIFT, 1 << GK_SHIFT) == (CHUNK, RN, GV, GK)

LANES = 128
CUM_ROWS = 2 * CHUNK
HALO = 16
SHIFT_COLS = 512
LORA_W = 512
GKD_W = 128
C_RKV, C_LORA, C_QK, C_GV, C_GG, C_GKD = 0, 1536, 2048, 2560, 3072, 3584
N_PROJ = 3712
VMEM_LIMIT = 56 * 1024 * 1024


def _cparams(sem):
    return pltpu.CompilerParams(dimension_semantics=sem, vmem_limit_bytes=VMEM_LIMIT)


def _mm(a, b):
    return jnp.dot(a.astype(BF16), b.astype(BF16), preferred_element_type=F32)


def _mm_nt(a, b):
    return lax.dot_general(a.astype(BF16), b.astype(BF16), (((1,), (1,)), ((), ())),
                           preferred_element_type=F32)


def _mm_tn(a, b):
    return lax.dot_general(a.astype(BF16), b.astype(BF16), (((0,), (0,)), ((), ())),
                           preferred_element_type=F32)


def _mm_lhs01(m01, x, terms):
    acc = None
    rem = x
    for i in range(terms):
        piece = rem.astype(BF16)
        part = jnp.dot(m01, piece, preferred_element_type=F32)
        acc = part if acc is None else acc + part
        if i + 1 < terms:
            rem = rem - piece.astype(F32)
    return acc


def _rms(x, g, eps):
    return x * lax.rsqrt(jnp.mean(x * x, axis=-1, keepdims=True) + eps) * g


def _sigmoid(x):
    return 1.0 / (1.0 + jnp.exp(-x))


def _iota2(shape, dim):
    return lax.broadcasted_iota(jnp.int32, shape, dim)


def _chunk_tri(n, reverse):
    t = _iota2((n, n), 0)
    s = _iota2((n, n), 1)
    same = (t >> CHUNK_SHIFT) == (s >> CHUNK_SHIFT)
    tri = (s >= t) if reverse else (s <= t)
    return jnp.where(same & tri, 1.0, 0.0).astype(BF16)


def _chunk_cumsum(x, reverse, terms):
    tri = _chunk_tri(CUM_ROWS, reverse)
    parts = [_mm_lhs01(tri, x[i:i + CUM_ROWS], terms) for i in range(0, x.shape[0], CUM_ROWS)]
    return parts[0] if len(parts) == 1 else jnp.concatenate(parts, axis=0)


def _inproj_body(x_ref, xp_ref, xn_ref, g_ref, w_ref, mu_ref, rkv_ref, lora_ref, qk_ref, gv_ref, gg_ref,
                 gkd_ref, *, tiles_per_seq):
    tm = x_ref.shape[0]
    i = pl.program_id(0)
    g = g_ref[...]
    h = _rms(x_ref[...], g, NORM_EPS).astype(BF16)
    h_prev = _rms(xp_ref[...], g, NORM_EPS).astype(BF16)
    h_next = _rms(xn_ref[...], g, NORM_EPS).astype(BF16)
    h_ext = jnp.concatenate([h_prev, h, h_next], axis=0)
    pos = i % tiles_per_seq
    has_prev = pos > 0
    has_next = pos < tiles_per_seq - 1
    rid = _iota2((tm, SHIFT_COLS), 0)

    def proj(lo, hi):
        return jnp.dot(h, w_ref[:, lo:hi], preferred_element_type=F32)

    for lo in range(C_RKV, C_QK, SHIFT_COLS):
        p_ext = jnp.dot(h_ext, w_ref[:, lo:lo + SHIFT_COLS], preferred_element_type=F32)
        p = p_ext[HALO:HALO + tm, :]
        p_before = p_ext[HALO - 1:HALO, :]
        p_after = p_ext[HALO + tm:HALO + tm + 1, :]
        prev = jnp.where(rid == 0, jnp.where(has_prev, p_before, 0.0), pltpu.roll(p, 1, 0))
        nxt = jnp.where(rid == tm - 1, jnp.where(has_next, p_after, 0.0), pltpu.roll(p, tm - 1, 0))
        mu = mu_ref[:, lo:lo + SHIFT_COLS]
        shifted = p + mu[0:1, :] * (prev - p) + mu[1:2, :] * (nxt - p)
        if lo < C_LORA:
            rkv_ref[:, lo:lo + SHIFT_COLS] = shifted
        else:
            lora_ref[:, lo - C_LORA:lo - C_LORA + SHIFT_COLS] = shifted
    qk_ref[...] = proj(C_QK, C_GV)
    gv_ref[...] = proj(C_GV, C_GG)
    gg_ref[...] = proj(C_GG, C_GKD)
    gkd_ref[...] = proj(C_GKD, N_PROJ)


def _inproj(x2d, g, w_perm, mu, tm, seq_len):
    n = x2d.shape[0]
    assert seq_len % tm == 0 and tm % HALO == 0
    widths = (C_LORA - C_RKV, C_QK - C_LORA, C_GV - C_QK, C_GG - C_GV, C_GKD - C_GG, N_PROJ - C_GKD)
    hb = tm // HALO
    last = n // HALO - 1
    body = functools.partial(_inproj_body, tiles_per_seq=seq_len // tm)
    return pl.pallas_call(
        body,
        grid=(n // tm,),
        in_specs=[pl.BlockSpec((tm, D_MODEL), lambda i: (i, 0)),
                  pl.BlockSpec((HALO, D_MODEL), lambda i: (jnp.maximum(i * hb - 1, 0), 0)),
                  pl.BlockSpec((HALO, D_MODEL), lambda i: (jnp.minimum((i + 1) * hb, last), 0)),
                  pl.BlockSpec((1, D_MODEL), lambda i: (0, 0)),
                  pl.BlockSpec((D_MODEL, N_PROJ), lambda i: (0, 0)),
                  pl.BlockSpec((2, C_QK), lambda i: (0, 0))],
        out_specs=[pl.BlockSpec((tm, w), lambda i: (i, 0)) for w in widths],
        out_shape=[jax.ShapeDtypeStruct((n, w), F32) for w in widths],
        compiler_params=_cparams(("parallel",)),
        name="inproj",
    )(x2d, x2d, x2d, g, w_perm, mu)


PV_W0_F, PV_W0_B, PV_A0_F, PV_A0_B, PV_KK, PV_KA, PV_RK, PV_LNW, PV_LNB = range(9)
PV_ROWS = 16


def _rwkv_body(r_ref, k_ref, v_ref, lo_ref, pv_ref, ww_ref, wa_ref, wg_ref, out_ref,
               y_scr, bon_scr, s_scr, *, seq_len, blk):
    C = CHUNK
    nc = blk // C
    nblk = seq_len // blk
    half = nblk // 2
    pv = pv_ref[...]
    prow = lambda i: pv[i:i + 1, :]

    lane = _iota2((1, LANES), 1)
    head_mask = [lane < RN, lane >= RN]
    li = _iota2((LANES, LANES), 0)
    lj = _iota2((LANES, LANES), 1)
    same_head = (li >> RN_SHIFT) == (lj >> RN_SHIFT)
    seg01 = jnp.where(same_head, 1.0, 0.0).astype(BF16)
    row_c = _iota2((C, LANES), 0)
    col_c = _iota2((C, LANES), 1) & (C - 1)
    eye_p = jnp.where(row_c == col_c, 1.0, 0.0)
    zeros_cl = jnp.zeros((C, LANES), BF16)
    zeros_ll = jnp.zeros((LANES, LANES), BF16)

    def seg_sum(x):
        return _mm(x, seg01)

    def block_inputs(t0, reverse):
        d = 1 if reverse else 0
        rows = pl.ds(t0, blk)
        r, k, v, lo = r_ref[0, rows, :], k_ref[0, rows, :], v_ref[0, rows, :], lo_ref[0, rows, :]
        zw = _mm(jnp.tanh(lo[:, 0:LANES]), ww_ref[0, d])
        za = _mm(lo[:, LANES:2 * LANES], wa_ref[0, d])
        lw = -_sigmoid(prow(PV_W0_F + d) + zw) * math.exp(-0.5)
        a = _sigmoid(prow(PV_A0_F + d) + za)
        kd = k * (1.0 + (a - 1.0) * prow(PV_KA))
        kk = k * prow(PV_KK)
        kk = kk * lax.rsqrt(seg_sum(kk * kk) + 1e-12)
        bonus = seg_sum(r * kd * prow(PV_RK)) * v
        g = _chunk_cumsum(lw, reverse, 2)
        return dict(r=r, v=v, kd=kd, kk=kk, a=a, lw=lw, g=g, bonus=bonus, lo=lo)

    def bd(x):
        xb = x.astype(BF16)
        zero = jnp.zeros_like(xb)
        return jnp.concatenate([jnp.where(head_mask[0], xb, zero), jnp.where(head_mask[1], xb, zero)], axis=0)

    def block_terms(ctxs):
        tri_s, tri_i, ch, chains = {}, {}, {}, []
        for d, q, order, reverse in ctxs:
            tri_s[d] = (col_c > row_c) if reverse else (col_c < row_c)
            tri_i[d] = (col_c >= row_c) if reverse else (col_c <= row_c)
            for c in order:
                sl = slice(c * C, (c + 1) * C)
                g, lw, kk = q["g"][sl], q["lw"][sl], q["kk"][sl]
                r, v, kd, a = q["r"][sl], q["v"][sl], q["kd"][sl], q["a"][sl]
                g_tot = g[0:1, :] if reverse else g[C - 1:C, :]
                eng = jnp.exp(-g)
                beta = kk * a
                dec = jnp.exp(g_tot - g)
                mxu = lambda x: x.astype(BF16)
                ch[d, c] = dict(v=mxu(v), rt=r * jnp.exp(g), at=mxu(-kk * jnp.exp(g - lw)), bt=mxu(beta * eng),
                                kt=mxu(kd * eng), bh=mxu(beta * dec), kh=mxu(kd * dec), gam=jnp.exp(g_tot))
                chains.append((d, c))
        a_ab, a_ak, a_rbk = {}, {}, {}
        for k in chains:
            e = ch[k]
            gm = _mm_nt(jnp.concatenate([e["at"], e["rt"].astype(BF16)], axis=0),
                        jnp.concatenate([bd(e["bt"]), bd(e["kt"])], axis=0))
            a_ab[k] = jnp.where(tri_s[k[0]], gm[:C, :LANES], 0.0).astype(BF16)
            a_ak[k] = jnp.where(tri_s[k[0]], gm[:C, LANES:], 0.0).astype(BF16)
            a_rbk[k] = jnp.concatenate([jnp.where(tri_i[k[0]], gm[C:, :LANES], 0.0),
                                        jnp.where(tri_i[k[0]], gm[C:, LANES:], 0.0)], axis=1).astype(BF16)
        tm = {k: eye_p + jnp.where((row_c >> 1) == (col_c >> 1), a_ab[k], 0.0) for k in chains}
        for lvl in range(1, 6):
            same = (row_c >> (lvl + 1)) == (col_c >> (lvl + 1))
            lower = (((row_c >> lvl) & 1) == 1) & (((col_c >> lvl) & 1) == 0)
            upper = (((row_c >> lvl) & 1) == 0) & (((col_c >> lvl) & 1) == 1)
            off = {d: same & (upper if reverse else lower) for d, _, _, reverse in ctxs}
            left = {k: _mm(tm[k], bd(jnp.where(off[k[0]], a_ab[k], zeros_cl))) for k in chains}
            tm = {k: tm[k] + _mm(left[k], bd(tm[k])) for k in chains}
        akv = {k: _mm(a_ak[k], bd(ch[k]["v"])) for k in chains}
        wu = {k: _mm(tm[k], jnp.concatenate([bd(ch[k]["at"]), bd(akv[k])], axis=1)).astype(BF16) for k in chains}

        def chunk_terms(k):
            e = ch[k]
            rhs = jnp.concatenate(
                [jnp.concatenate([bd(wu[k][:, :LANES]), bd(wu[k][:, LANES:])], axis=1),
                 jnp.concatenate([zeros_ll, bd(e["v"])], axis=1)], axis=0)
            qy = _mm(a_rbk[k], rhs)
            lhs = jnp.concatenate([wu[k], jnp.concatenate([zeros_cl, e["v"]], axis=1)], axis=0)
            pn = _mm_tn(lhs, jnp.concatenate([e["bh"], e["kh"]], axis=0))
            p_off = jnp.where(same_head, pn[:LANES], 0.0).astype(BF16)
            n0t = jnp.where(same_head, pn[LANES:], 0.0)
            return e["rt"] + qy[:, :LANES], qy[:, LANES:], e["gam"], p_off, n0t

        return chunk_terms

    def run_pair(tf, tb):
        qf = block_inputs(tf, False)
        qb = block_inputs(tb, True)
        ctxs = [(0, qf, list(range(nc)), False), (1, qb, list(range(nc - 1, -1, -1)), True)]
        chunk_terms = block_terms(ctxs)
        s = [s_scr[0], s_scr[1]]
        ys = [[None] * nc, [None] * nc]
        terms = {(d, order[0]): chunk_terms((d, order[0])) for d, _, order, _ in ctxs}
        for step in range(nc):
            if step + 1 < nc:
                for d, _, order, _ in ctxs:
                    terms[d, order[step + 1]] = chunk_terms((d, order[step + 1]))
            for d, _, order, _ in ctxs:
                qh, y0, gam, p_off, n0t = terms[d, order[step]]
                sb = s[d].astype(BF16)
                ys[d][order[step]] = _mm_nt(qh, sb) + y0
                s[d] = (s[d] * gam + n0t) + jnp.dot(sb, p_off, preferred_element_type=F32)
        s_scr[0] = s[0]
        s_scr[1] = s[1]
        return qf, jnp.concatenate(ys[0], axis=0), qb, jnp.concatenate(ys[1], axis=0)

    def finish(q, y, bonus, t0):
        mean = seg_sum(y) * (1.0 / RN)
        dy = y - mean
        var = seg_sum(dy * dy) * (1.0 / RN)
        gn = dy * lax.rsqrt(var + GN_EPS) * prow(PV_LNW) + prow(PV_LNB)
        gate = _mm(_sigmoid(q["lo"][:, 2 * LANES:]), wg_ref[0])
        out_ref[0, pl.ds(t0, blk), :] = ((gn + bonus) * gate).astype(out_ref.dtype)

    s_scr[...] = jnp.zeros(s_scr.shape, F32)

    def block_starts(i):
        return pl.multiple_of(i * blk, blk), pl.multiple_of((nblk - 1 - i) * blk, blk)

    def first_half(i, carry):
        tf, tb = block_starts(i)
        qf, yf, qb, yb = run_pair(tf, tb)
        y_scr[pl.ds(tf, blk), :] = yf
        bon_scr[pl.ds(tf, blk), :] = qf["bonus"]
        y_scr[pl.ds(tb, blk), :] = yb
        bon_scr[pl.ds(tb, blk), :] = qb["bonus"]
        return carry

    def second_half(i, carry):
        tf, tb = block_starts(i)
        qf, yf, qb, yb = run_pair(tf, tb)
        finish(qf, yf + y_scr[pl.ds(tf, blk), :], qf["bonus"] + bon_scr[pl.ds(tf, blk), :], tf)
        finish(qb, yb + y_scr[pl.ds(tb, blk), :], qb["bonus"] + bon_scr[pl.ds(tb, blk), :], tb)
        return carry

    lax.fori_loop(0, half, first_half, 0)
    lax.fori_loop(half, nblk, second_half, 0)


def _rwkv(rkv, lora, pvec, ww, wa, wg, blk):
    b, t, _ = rkv.shape
    assert t % (2 * blk) == 0 and blk % CUM_ROWS == 0
    nhp = RW // LANES
    col = lambda off: pl.BlockSpec((1, t, LANES), lambda i, j, off=off: (i, 0, off + j))
    body = functools.partial(_rwkv_body, seq_len=t, blk=blk)
    return pl.pallas_call(
        body,
        grid=(b, nhp),
        in_specs=[col(0), col(nhp), col(2 * nhp),
                  pl.BlockSpec((1, t, LORA_W), lambda i, j: (i, 0, 0)),
                  pl.BlockSpec((PV_ROWS, LANES), lambda i, j: (0, j)),
                  pl.BlockSpec((1, 2, LANES, LANES), lambda i, j: (j, 0, 0, 0)),
                  pl.BlockSpec((1, 2, LANES, LANES), lambda i, j: (j, 0, 0, 0)),
                  pl.BlockSpec((1, 2 * LANES, LANES), lambda i, j: (j, 0, 0))],
        out_specs=pl.BlockSpec((1, t, LANES), lambda i, j: (i, 0, j)),
        out_shape=jax.ShapeDtypeStruct((b, t, RW), BF16),
        scratch_shapes=[pltpu.VMEM((t, LANES), F32), pltpu.VMEM((t, LANES), F32),
                        pltpu.VMEM((2, LANES, LANES), F32)],
        compiler_params=_cparams(("parallel", "parallel")),
        name="rwkv",
    )(rkv, rkv, rkv, lora, pvec, ww, wa, wg)


def _gla_body(q_ref, k_ref, v_ref, gg_ref, gkd_ref, wgk_ref, gkb_ref, nw_ref, out_ref,
              o_scr, s_scr, *, seq_len, blk):
    C = CHUNK
    nc = blk // C
    nblk = seq_len // blk
    half = nblk // 2
    lane = _iota2((1, LANES), 1)
    head_mask = [lane < GK, lane >= GK]
    si = _iota2((2 * GV, LANES), 0)
    sj = _iota2((2 * GV, LANES), 1)
    same_head = (si >> GV_SHIFT) == (sj >> GK_SHIFT)
    row_c = _iota2((C, LANES), 0)
    col_c = _iota2((C, LANES), 1) & (C - 1)
    zeros_cv = jnp.zeros((C, GV), BF16)

    def run_pair(tf, tb):
        ctxs = [dict(d=0, rows=pl.ds(tf, blk), reverse=False, order=list(range(nc))),
                dict(d=1, rows=pl.ds(tb, blk), reverse=True, order=list(range(nc - 1, -1, -1)))]
        for cx in ctxs:
            cx["z"] = _mm(gkd_ref[0, cx["rows"], :], wgk_ref[0, cx["d"]]) + gkb_ref[0, cx["d"]]
        for cx in ctxs:
            z = cx["z"]
            lg = (jnp.minimum(z, 0.0) - jnp.log1p(jnp.exp(-jnp.abs(z)))) * (1.0 / GATE_NORM)
            cx["bcum"] = _chunk_cumsum(lg, cx["reverse"], 3)
        ch, chains, tri = {}, [], {}
        for cx in ctxs:
            d, reverse = cx["d"], cx["reverse"]
            tri[d] = (col_c >= row_c) if reverse else (col_c <= row_c)
            q = q_ref[0, cx["rows"], :] * (GK ** -0.5)
            k = k_ref[0, cx["rows"], :]
            v = v_ref[0, cx["rows"], :]
            for c in cx["order"]:
                sl = slice(c * C, (c + 1) * C)
                bc = cx["bcum"][sl]
                b_tot = bc[0:1, :] if reverse else bc[C - 1:C, :]
                mxu = lambda x: x.astype(BF16)
                ch[d, c] = dict(q_in=mxu(q[sl] * jnp.exp(bc)), k_in=mxu(k[sl] * jnp.exp(-bc)),
                                k_dec=mxu(k[sl] * jnp.exp(b_tot - bc)), v=mxu(v[sl]), gam=jnp.exp(b_tot))
                chains.append((d, c))
        att, intra, ds = {}, {}, {}
        for kk in chains:
            k_in = ch[kk]["k_in"]
            zero = jnp.zeros_like(k_in)
            k_bd = jnp.concatenate([jnp.where(head_mask[0], k_in, zero), jnp.where(head_mask[1], k_in, zero)], axis=0)
            att[kk] = jnp.where(tri[kk[0]], _mm_nt(ch[kk]["q_in"], k_bd), 0.0).astype(BF16)
        for kk in chains:
            v = ch[kk]["v"]
            v_bd = jnp.concatenate([jnp.concatenate([v[:, :GV], zeros_cv], axis=1),
                                    jnp.concatenate([zeros_cv, v[:, GV:]], axis=1)], axis=0)
            intra[kk] = _mm(att[kk], v_bd)
        for kk in chains:
            ds[kk] = jnp.where(same_head, _mm_tn(ch[kk]["v"], ch[kk]["k_dec"]), 0.0)
        s_in = {}
        for cx in ctxs:
            d = cx["d"]
            s = s_scr[d]
            for c in cx["order"]:
                s_in[d, c] = s
                s = s * ch[d, c]["gam"] + ds[d, c]
            s_scr[d] = s
        outs = {kk: _mm_nt(ch[kk]["q_in"], s_in[kk]) + intra[kk] for kk in chains}
        return (jnp.concatenate([outs[0, c] for c in range(nc)], axis=0),
                jnp.concatenate([outs[1, c] for c in range(nc)], axis=0))

    def finish(o, t0):
        gg = gg_ref[0, pl.ds(t0, blk), :]
        res = []
        for h in range(2):
            oh = o[:, h * GV:(h + 1) * GV]
            oh = oh * lax.rsqrt(jnp.mean(oh * oh, axis=-1, keepdims=True) + GLA_EPS) * nw_ref[...]
            gh = gg[:, h * GV:(h + 1) * GV]
            res.append(oh * (gh * _sigmoid(gh)))
        out_ref[0, pl.ds(t0, blk), :] = jnp.concatenate(res, axis=1).astype(out_ref.dtype)

    s_scr[...] = jnp.zeros(s_scr.shape, F32)

    def block_starts(i):
        return pl.multiple_of(i * blk, blk), pl.multiple_of((nblk - 1 - i) * blk, blk)

    def first_half(i, carry):
        tf, tb = block_starts(i)
        of, ob = run_pair(tf, tb)
        o_scr[pl.ds(tf, blk), :] = of
        o_scr[pl.ds(tb, blk), :] = ob
        return carry

    def second_half(i, carry):
        tf, tb = block_starts(i)
        of, ob = run_pair(tf, tb)
        finish(of + o_scr[pl.ds(tf, blk), :], tf)
        finish(ob + o_scr[pl.ds(tb, blk), :], tb)
        return carry

    lax.fori_loop(0, half, first_half, 0)
    lax.fori_loop(half, nblk, second_half, 0)


def _gla(qk, gv, gg, gkd, wgk, gkb, nw, blk):
    b, t, _ = qk.shape
    assert t % (2 * blk) == 0 and blk % CUM_ROWS == 0
    npair = GH // 2
    body = functools.partial(_gla_body, seq_len=t, blk=blk)
    return pl.pallas_call(
        body,
        grid=(b, npair),
        in_specs=[pl.BlockSpec((1, t, LANES), lambda i, j: (i, 0, j)),
                  pl.BlockSpec((1, t, LANES), lambda i, j: (i, 0, npair + j)),
                  pl.BlockSpec((1, t, 2 * GV), lambda i, j: (i, 0, j)),
                  pl.BlockSpec((1, t, 2 * GV), lambda i, j: (i, 0, j)),
                  pl.BlockSpec((1, t, GKD_W), lambda i, j: (i, 0, 0)),
                  pl.BlockSpec((1, 2, GKD_W, LANES), lambda i, j: (j, 0, 0, 0)),
                  pl.BlockSpec((1, 2, 1, LANES), lambda i, j: (j, 0, 0, 0)),
                  pl.BlockSpec((1, GV), lambda i, j: (0, 0))],
        out_specs=pl.BlockSpec((1, t, 2 * GV), lambda i, j: (i, 0, j)),
        out_shape=jax.ShapeDtypeStruct((b, t, GH * GV), BF16),
        scratch_shapes=[pltpu.VMEM((t, 2 * GV), F32), pltpu.VMEM((2, 2 * GV, LANES), F32)],
        compiler_params=_cparams(("parallel", "parallel")),
        name="gla",
    )(qk, qk, gv, gg, gkd, wgk, gkb, nw)


def _kvproj_body(m_ref, g_ref, w_ref, kv_ref):
    h = _rms(m_ref[...], g_ref[...], NORM_EPS).astype(BF16)
    kv_ref[...] = jnp.dot(h, w_ref[...], preferred_element_type=F32).astype(kv_ref.dtype)


def _kvproj(mem2d, g_mem, wkv, tm):
    n = mem2d.shape[0]
    return pl.pallas_call(
        _kvproj_body,
        grid=(n // tm,),
        in_specs=[pl.BlockSpec((tm, D_MODEL), lambda i: (i, 0)),
                  pl.BlockSpec((1, D_MODEL), lambda i: (0, 0)),
                  pl.BlockSpec((D_MODEL, 2 * D_MODEL), lambda i: (0, 0))],
        out_specs=pl.BlockSpec((tm, 2 * D_MODEL), lambda i: (i, 0)),
        out_shape=jax.ShapeDtypeStruct((n, 2 * D_MODEL), BF16),
        compiler_params=_cparams(("parallel",)),
        name="kvproj",
    )(mem2d, g_mem, wkv)


def _xattn_body(x_ref, rw_ref, gl_ref, kv_ref, wout_ref, gmix_ref, gpre_ref, wq_ref, wo_ref, gpost_ref, o_ref):
    tm = x_ref.shape[1]
    cols = lambda hd: slice(hd * X_HD, (hd + 1) * X_HD)
    heads = range(X_HEADS)

    def half_tile(rows):
        mixed = (jnp.dot(rw_ref[0, rows, :], wout_ref[0:RW, :], preferred_element_type=F32)
                 + jnp.dot(gl_ref[0, rows, :], wout_ref[RW:, :], preferred_element_type=F32))
        yield
        x = x_ref[0, rows, :] + _rms(mixed, gmix_ref[...], NORM_EPS)
        h = _rms(x, gpre_ref[...], NORM_EPS).astype(BF16)
        yield
        q = [jnp.dot(h, wq_ref[:, cols(hd)], preferred_element_type=F32).astype(BF16) for hd in heads]
        scores = [lax.dot_general(q[hd], kv_ref[0, :, cols(hd)], (((1,), (1,)), ((), ())),
                                  preferred_element_type=F32) * (X_HD ** -0.5) for hd in heads]
        yield
        probs = []
        for s in scores:
            e = jnp.exp(s - jnp.max(s, axis=-1, keepdims=True))
            probs.append((e * (1.0 / jnp.sum(e, axis=-1, keepdims=True))).astype(BF16))
        yield
        outs = [jnp.dot(probs[hd], kv_ref[0, :, D_MODEL + hd * X_HD:D_MODEL + (hd + 1) * X_HD],
                        preferred_element_type=F32).astype(BF16) for hd in heads]
        att = jnp.dot(jnp.concatenate(outs, axis=1), wo_ref[...], preferred_element_type=F32)
        yield
        o_ref[0, rows, :] = x + _rms(att, gpost_ref[...], NORM_EPS)

    first, second = half_tile(pl.ds(0, tm // 2)), half_tile(pl.ds(tm // 2, tm // 2))
    next(first)
    running = [second, first]
    while running:
        for gen in list(running):
            if next(gen, "done") == "done":
                running.remove(gen)


def _xattn(x3d, rw3d, gl3d, kv3d, w_out, g_mix_post, g_pre, wq, wo, g_post, tm):
    b, t, _ = x3d.shape
    row = lambda w: pl.BlockSpec((1, tm, w), lambda i, j: (i, j, 0))
    const = lambda r, c: pl.BlockSpec((r, c), lambda i, j: (0, 0))
    return pl.pallas_call(
        _xattn_body,
        grid=(b, t // tm),
        in_specs=[row(D_MODEL), row(RW), row(GH * GV),
                  pl.BlockSpec((1, N_MEM, 2 * D_MODEL), lambda i, j: (i, 0, 0)),
                  const(D_MODEL, D_MODEL), const(1, D_MODEL), const(1, D_MODEL),
                  const(D_MODEL, D_MODEL), const(D_MODEL, D_MODEL), const(1, D_MODEL)],
        out_specs=row(D_MODEL),
        out_shape=jax.ShapeDtypeStruct((b, t, D_MODEL), F32),
        compiler_params=_cparams(("parallel", "parallel")),
        name="xattn",
    )(x3d, rw3d, gl3d, kv3d, w_out, g_mix_post, g_pre, wq, wo, g_post)


def _ffn_body(x_ref, gpre_ref, w1_ref, w2_ref, gpost_ref, o_ref, h_scr, acc_scr):
    j = pl.program_id(1)

    @pl.when(j == 0)
    def _():
        h_scr[...] = _rms(x_ref[...], gpre_ref[...], NORM_EPS).astype(BF16)
        acc_scr[...] = jnp.zeros_like(acc_scr)

    a = jnp.dot(h_scr[...], w1_ref[...], preferred_element_type=F32)
    a = jnp.square(jnp.maximum(a, 0.0)).astype(BF16)
    acc_scr[...] += jnp.dot(a, w2_ref[...], preferred_element_type=F32)

    @pl.when(j == pl.num_programs(1) - 1)
    def _():
        o_ref[...] = x_ref[...] + _rms(acc_scr[...], gpost_ref[...], NORM_EPS)


def _ffn(x2d, g_pre, w1, w2, g_post, tm, tf):
    n = x2d.shape[0]
    return pl.pallas_call(
        _ffn_body,
        grid=(n // tm, D_FF // tf),
        in_specs=[pl.BlockSpec((tm, D_MODEL), lambda i, j: (i, 0)),
                  pl.BlockSpec((1, D_MODEL), lambda i, j: (0, 0)),
                  pl.BlockSpec((D_MODEL, tf), lambda i, j: (0, j)),
                  pl.BlockSpec((tf, D_MODEL), lambda i, j: (j, 0)),
                  pl.BlockSpec((1, D_MODEL), lambda i, j: (0, 0))],
        out_specs=pl.BlockSpec((tm, D_MODEL), lambda i, j: (i, 0)),
        out_shape=jax.ShapeDtypeStruct((n, D_MODEL), F32),
        scratch_shapes=[pltpu.VMEM((tm, D_MODEL), BF16), pltpu.VMEM((tm, D_MODEL), F32)],
        compiler_params=_cparams(("parallel", "arbitrary")),
        name="ffn",
    )(x2d, g_pre, w1, w2, g_post)


def _pack_params(p):
    w_in = p["w_in"]
    g0 = R_COLS
    zeros = lambda n: jnp.zeros((D_MODEL, n), w_in.dtype)
    w_perm = jnp.concatenate([
        w_in[:, :R_COLS], zeros(LORA_W - (R_COLS - 3 * RW)),
        w_in[:, g0:g0 + 2 * GH * GK],
        w_in[:, g0 + 2 * GH * GK:g0 + 2 * GH * GK + GH * GV],
        w_in[:, g0 + 2 * GH * GK + GH * GV + GK_LORA:],
        w_in[:, g0 + 2 * GH * GK + GH * GV:g0 + 2 * GH * GK + GH * GV + GK_LORA], zeros(GKD_W - GK_LORA),
    ], axis=1).astype(BF16)
    row = lambda v: v.reshape(1, -1)
    mu_p, mu_n = p["mu_prev"], p["mu_next"]
    pvec = jnp.concatenate([
        row(p["w0_f"]), row(p["w0_b"]), row(p["a0_f"]), row(p["a0_b"]),
        row(p["k_k"]), row(p["k_a"]), row(p["r_k"]), row(p["lnx_w"]), row(p["lnx_b"]),
        jnp.zeros((PV_ROWS - 9, RW), F32)], axis=0)
    pad_mu = lambda v: jnp.pad(v, (0, C_QK - R_COLS))
    mu = jnp.stack([pad_mu(mu_p), pad_mu(mu_n)], axis=0)
    nhp = RW // LANES

    def lora_pair(wf, wb):
        wf = wf.reshape(-1, nhp, LANES).transpose(1, 0, 2)
        wb = wb.reshape(-1, nhp, LANES).transpose(1, 0, 2)
        z = jnp.zeros_like(wf)
        return jnp.stack([jnp.concatenate([wf, z], axis=1), jnp.concatenate([z, wb], axis=1)],
                         axis=1).astype(BF16)

    ww = lora_pair(p["w2_f"], p["w2_b"])
    wa = lora_pair(p["a2_f"], p["a2_b"])
    wg = jnp.pad(p["g2"], ((0, 2 * LANES - GATE_LORA), (0, 0)))
    wg = wg.reshape(2 * LANES, nhp, LANES).transpose(1, 0, 2).astype(BF16)
    npair = GH // 2

    def gk_pair(w):
        w = jnp.pad(w, ((0, GKD_W - GK_LORA), (0, 0)))
        return w.reshape(GKD_W, npair, LANES).transpose(1, 0, 2)

    wgk = jnp.stack([gk_pair(p["gk2_f"]), gk_pair(p["gk2_b"])], axis=1).astype(BF16)
    gkb = jnp.stack([p["gkb_f"].reshape(npair, 1, LANES), p["gkb_b"].reshape(npair, 1, LANES)], axis=1)
    return dict(
        w_perm=w_perm, pvec=pvec, mu=mu, ww=ww, wa=wa, wg=wg, wgk=wgk, gkb=gkb,
        nw=row(p["gla_norm_w"]),
        g_mix_pre=row(p["g_mix_pre"]), g_mix_post=row(p["g_mix_post"]),
        g_x_pre=row(p["g_x_pre"]), g_x_post=row(p["g_x_post"]), g_mem=row(p["g_mem"]),
        g_ffn_pre=row(p["g_ffn_pre"]), g_ffn_post=row(p["g_ffn_post"]),
        w_out=p["w_out"].astype(BF16), wq=p["wq_x"].astype(BF16), wkv=p["wkv_x"].astype(BF16),
        wo=p["wo_x"].astype(BF16), w1=p["w_ff1"].astype(BF16), w2=p["w_ff2"].astype(BF16))


def _pick(n, pref):
    t = pref
    while n % t:
        t //= 2
    return t


def _trunk(x, mem, pk):
    b, t, _ = x.shape
    n = b * t
    tm = _pick(n, 1024)
    x2d = x.reshape(n, D_MODEL)
    rkv, lora, qk, gv, gg, gkd = _inproj(x2d, pk["g_mix_pre"], pk["w_perm"], pk["mu"], _pick(t, 512), t)
    r3 = lambda a: a.reshape(b, t, a.shape[-1])
    blk = _pick(t // 2, 1024)
    rw = _rwkv(r3(rkv), r3(lora), pk["pvec"], pk["ww"], pk["wa"], pk["wg"], blk)
    gl = _gla(r3(qk), r3(gv), r3(gg), r3(gkd), pk["wgk"], pk["gkb"], pk["nw"], blk)
    nm = mem.shape[0] * mem.shape[1]
    kv = _kvproj(mem.reshape(nm, D_MODEL), pk["g_mem"], pk["wkv"], _pick(nm, 512))
    x2 = _xattn(x, rw, gl, kv.reshape(mem.shape[0], mem.shape[1], 2 * D_MODEL), pk["w_out"],
                pk["g_mix_post"], pk["g_x_pre"], pk["wq"], pk["wo"], pk["g_x_post"], _pick(t, 1024))
    y = _ffn(x2.reshape(n, D_MODEL), pk["g_ffn_pre"], pk["w1"], pk["w2"], pk["g_ffn_post"],
             tm, 1024)
    return y.reshape(b, t, D_MODEL)


def kernel(x_prompt, x_sample, mem_prompt, mem_sample, g_mix_pre, w_in, mu_prev, mu_next, w0_f, w2_f, w0_b, w2_b, a0_f, a2_f, a0_b, a2_b, g2, k_k, k_a, r_k, lnx_w, lnx_b, gk2_f, gkb_f, gk2_b, gkb_b, gla_norm_w, w_out, g_mix_post, g_x_pre, g_mem, wq_x, wkv_x, wo_x, g_x_post, g_ffn_pre, w_ff1, w_ff2, g_ffn_post):
    params = dict(
        g_mix_pre=g_mix_pre, w_in=w_in, mu_prev=mu_prev, mu_next=mu_next, w0_f=w0_f, w2_f=w2_f,
        w0_b=w0_b, w2_b=w2_b, a0_f=a0_f, a2_f=a2_f, a0_b=a0_b, a2_b=a2_b, g2=g2, k_k=k_k, k_a=k_a,
        r_k=r_k, lnx_w=lnx_w, lnx_b=lnx_b, gk2_f=gk2_f, gkb_f=gkb_f, gk2_b=gk2_b, gkb_b=gkb_b,
        gla_norm_w=gla_norm_w, w_out=w_out, g_mix_post=g_mix_post, g_x_pre=g_x_pre, g_mem=g_mem,
        wq_x=wq_x, wkv_x=wkv_x, wo_x=wo_x, g_x_post=g_x_post, g_ffn_pre=g_ffn_pre, w_ff1=w_ff1,
        w_ff2=w_ff2, g_ffn_post=g_ffn_post)
    assert w_in.shape[0] == 1, "single-layer stack expected"
    pk = _pack_params({name: w[0] for name, w in params.items()})
    return (_trunk(x_prompt, mem_prompt, pk), _trunk(x_sample, mem_sample, pk))
```

```python
import functools
import math

import jax
import jax.numpy as jnp
from jax import lax
from jax.experimental import pallas as pl
from jax.experimental.pallas import tpu as pltpu

F32 = jnp.float32
BF16 = jnp.bfloat16

D_MODEL = 1024
RW = 512
RN = 64
DECAY_LORA = 64
AAA_LORA = 64
GATE_LORA = 160
GN_EPS = 64e-5
GH = 4
GV = 128
GK = 64
GK_LORA = 16
GATE_NORM = 16.0
CHUNK = 64
GLA_EPS = 1e-5
R_COLS = 3 * RW + 2 * DECAY_LORA + 2 * AAA_LORA + GATE_LORA
N_MEM = 256
X_HEADS = 4
X_HD = D_MODEL // X_HEADS
D_FF = 4 * D_MODEL
NORM_EPS = 1e-6
CHUNK_SHIFT, RN_SHIFT, GV_SHIFT, GK_SHIFT = (n.bit_length() - 1 for n in (CHUNK, RN, GV, GK))
assert (1 << CHUNK_SHIFT, 1 << RN_SHIFT, 1 << GV_SHIFT, 1 << GK_SHIFT) == (CHUNK, RN, GV, GK)

LANES = 128
CUM_ROWS = 2 * CHUNK
HALO = 16
SHIFT_COLS = 512
LORA_W = 512
GKD_W = 128
C_RKV, C_LORA, C_QK, C_GV, C_GG, C_GKD = 0, 1536, 2048, 2560, 3072, 3584
N_PROJ = 3712
VMEM_LIMIT = 56 * 1024 * 1024


def _cparams(sem):
    return pltpu.CompilerParams(dimension_semantics=sem, vmem_limit_bytes=VMEM_LIMIT)


def _mm(a, b):
    return jnp.dot(a.astype(BF16), b.astype(BF16), preferred_element_type=F32)


def _mm_nt(a, b):
    return lax.dot_general(a.astype(BF16), b.astype(BF16), (((1,), (1,)), ((), ())),
                           preferred_element_type=F32)


def _mm_tn(a, b):
    return lax.dot_general(a.astype(BF16), b.astype(BF16), (((0,), (0,)), ((), ())),
                           preferred_element_type=F32)


def _mm_lhs01(m01, x, terms):
    acc = None
    rem = x
    for i in range(terms):
        piece = rem.astype(BF16)
        part = jnp.dot(m01, piece, preferred_element_type=F32)
        acc = part if acc is None else acc + part
        if i + 1 < terms:
            rem = rem - piece.astype(F32)
    return acc


def _rms(x, g, eps):
    return x * lax.rsqrt(jnp.mean(x * x, axis=-1, keepdims=True) + eps) * g


def _sigmoid(x):
    return 1.0 / (1.0 + jnp.exp(-x))


def _iota2(shape, dim):
    return lax.broadcasted_iota(jnp.int32, shape, dim)


def _chunk_tri(n, reverse):
    t = _iota2((n, n), 0)
    s = _iota2((n, n), 1)
    same = (t >> CHUNK_SHIFT) == (s >> CHUNK_SHIFT)
    tri = (s >= t) if reverse else (s <= t)
    return jnp.where(same & tri, 1.0, 0.0).astype(BF16)


def _chunk_cumsum(x, reverse, terms):
    tri = _chunk_tri(CUM_ROWS, reverse)
    parts = [_mm_lhs01(tri, x[i:i + CUM_ROWS], terms) for i in range(0, x.shape[0], CUM_ROWS)]
    return parts[0] if len(parts) == 1 else jnp.concatenate(parts, axis=0)


def _inproj_body(x_ref, xp_ref, xn_ref, g_ref, w_ref, mu_ref, rkv_ref, lora_ref, qk_ref, gv_ref, gg_ref,
                 gkd_ref, *, tiles_per_seq):
    tm = x_ref.shape[0]
    i = pl.program_id(0)
    g = g_ref[...]
    h = _rms(x_ref[...], g, NORM_EPS).astype(BF16)
    h_prev = _rms(xp_ref[...], g, NORM_EPS).astype(BF16)
    h_next = _rms(xn_ref[...], g, NORM_EPS).astype(BF16)
    h_ext = jnp.concatenate([h_prev, h, h_next], axis=0)
    pos = i % tiles_per_seq
    has_prev = pos > 0
    has_next = pos < tiles_per_seq - 1
    rid = _iota2((tm, SHIFT_COLS), 0)

    def proj(lo, hi):
        return jnp.dot(h, w_ref[:, lo:hi], preferred_element_type=F32)

    for lo in range(C_RKV, C_QK, SHIFT_COLS):
        p_ext = jnp.dot(h_ext, w_ref[:, lo:lo + SHIFT_COLS], preferred_element_type=F32)
        p = p_ext[HALO:HALO + tm, :]
        p_before = p_ext[HALO - 1:HALO, :]
        p_after = p_ext[HALO + tm:HALO + tm + 1, :]
        prev = jnp.where(rid == 0, jnp.where(has_prev, p_before, 0.0), pltpu.roll(p, 1, 0))
        nxt = jnp.where(rid == tm - 1, jnp.where(has_next, p_after, 0.0), pltpu.roll(p, tm - 1, 0))
        mu = mu_ref[:, lo:lo + SHIFT_COLS]
        shifted = p + mu[0:1, :] * (prev - p) + mu[1:2, :] * (nxt - p)
        if lo < C_LORA:
            rkv_ref[:, lo:lo + SHIFT_COLS] = shifted
        else:
            lora_ref[:, lo - C_LORA:lo - C_LORA + SHIFT_COLS] = shifted
    qk_ref[...] = proj(C_QK, C_GV)
    gv_ref[...] = proj(C_GV, C_GG)
    gg_ref[...] = proj(C_GG, C_GKD)
    gkd_ref[...] = proj(C_GKD, N_PROJ)


def _inproj(x2d, g, w_perm, mu, tm, seq_len):
    n = x2d.shape[0]
    assert seq_len % tm == 0 and tm % HALO == 0
    widths = (C_LORA - C_RKV, C_QK - C_LORA, C_GV - C_QK, C_GG - C_GV, C_GKD - C_GG, N_PROJ - C_GKD)
    hb = tm // HALO
    last = n // HALO - 1
    body = functools.partial(_inproj_body, tiles_per_seq=seq_len // tm)
    return pl.pallas_call(
        body,
        grid=(n // tm,),
        in_specs=[pl.BlockSpec((tm, D_MODEL), lambda i: (i, 0)),
                  pl.BlockSpec((HALO, D_MODEL), lambda i: (jnp.maximum(i * hb - 1, 0), 0)),
                  pl.BlockSpec((HALO, D_MODEL), lambda i: (jnp.minimum((i + 1) * hb, last), 0)),
                  pl.BlockSpec((1, D_MODEL), lambda i: (0, 0)),
                  pl.BlockSpec((D_MODEL, N_PROJ), lambda i: (0, 0)),
                  pl.BlockSpec((2, C_QK), lambda i: (0, 0))],
        out_specs=[pl.BlockSpec((tm, w), lambda i: (i, 0)) for w in widths],
        out_shape=[jax.ShapeDtypeStruct((n, w), F32) for w in widths],
        compiler_params=_cparams(("parallel",)),
        name="inproj",
    )(x2d, x2d, x2d, g, w_perm, mu)


PV_W0_F, PV_W0_B, PV_A0_F, PV_A0_B, PV_KK, PV_KA, PV_RK, PV_LNW, PV_LNB = range(9)
PV_ROWS = 16


def _rwkv_body(r_ref, k_ref, v_ref, lo_ref, pv_ref, ww_ref, wa_ref, wg_ref, out_ref,
               y_scr, bon_scr, s_scr, *, seq_len, blk):
    C = CHUNK
    nc = blk // C
    nblk = seq_len // blk
    half = nblk // 2
    pv = pv_ref[...]
    prow = lambda i: pv[i:i + 1, :]

    lane = _iota2((1, LANES), 1)
    head_mask = [lane < RN, lane >= RN]
    li = _iota2((LANES, LANES), 0)
    lj = _iota2((LANES, LANES), 1)
    same_head = (li >> RN_SHIFT) == (lj >> RN_SHIFT)
    seg01 = jnp.where(same_head, 1.0, 0.0).astype(BF16)
    row_c = _iota2((C, LANES), 0)
    col_c = _iota2((C, LANES), 1) & (C - 1)
    eye_p = jnp.where(row_c == col_c, 1.0, 0.0)
    zeros_cl = jnp.zeros((C, LANES), BF16)
    zeros_ll = jnp.zeros((LANES, LANES), BF16)

    def seg_sum(x):
        return _mm(x, seg01)

    def block_inputs(t0, reverse):
        d = 1 if reverse else 0
        rows = pl.ds(t0, blk)
        r, k, v, lo = r_ref[0, rows, :], k_ref[0, rows, :], v_ref[0, rows, :], lo_ref[0, rows, :]
        zw = _mm(jnp.tanh(lo[:, 0:LANES]), ww_ref[0, d])
        za = _mm(lo[:, LANES:2 * LANES], wa_ref[0, d])
        lw = -_sigmoid(prow(PV_W0_F + d) + zw) * math.exp(-0.5)
        a = _sigmoid(prow(PV_A0_F + d) + za)
        kd = k * (1.0 + (a - 1.0) * prow(PV_KA))
        kk = k * prow(PV_KK)
        kk = kk * lax.rsqrt(seg_sum(kk * kk) + 1e-12)
        bonus = seg_sum(r * kd * prow(PV_RK)) * v
        g = _chunk_cumsum(lw, reverse, 2)
        return dict(r=r, v=v, kd=kd, kk=kk, a=a, lw=lw, g=g, bonus=bonus, lo=lo)

    def bd(x):
        xb = x.astype(BF16)
        zero = jnp.zeros_like(xb)
        return jnp.concatenate([jnp.where(head_mask[0], xb, zero), jnp.where(head_mask[1], xb, zero)], axis=0)

    def block_terms(ctxs):
        tri_s, tri_i, ch, chains = {}, {}, {}, []
        for d, q, order, reverse in ctxs:
            tri_s[d] = (col_c > row_c) if reverse else (col_c < row_c)
            tri_i[d] = (col_c >= row_c) if reverse else (col_c <= row_c)
            for c in order:
                sl = slice(c * C, (c + 1) * C)
                g, lw, kk = q["g"][sl], q["lw"][sl], q["kk"][sl]
                r, v, kd, a = q["r"][sl], q["v"][sl], q["kd"][sl], q["a"][sl]
                g_tot = g[0:1, :] if reverse else g[C - 1:C, :]
                eng = jnp.exp(-g)
                beta = kk * a
                dec = jnp.exp(g_tot - g)
                mxu = lambda x: x.astype(BF16)
                ch[d, c] = dict(v=mxu(v), rt=r * jnp.exp(g), at=mxu(-kk * jnp.exp(g - lw)), bt=mxu(beta * eng),
                                kt=mxu(kd * eng), bh=mxu(beta * dec), kh=mxu(kd * dec), gam=jnp.exp(g_tot))
                chains.append((d, c))
        a_ab, a_ak, a_rbk = {}, {}, {}
        for k in chains:
            e = ch[k]
            gm = _mm_nt(jnp.concatenate([e["at"], e["rt"].astype(BF16)], axis=0),
                        jnp.concatenate([bd(e["bt"]), bd(e["kt"])], axis=0))
            a_ab[k] = jnp.where(tri_s[k[0]], gm[:C, :LANES], 0.0).astype(BF16)
            a_ak[k] = jnp.where(tri_s[k[0]], gm[:C, LANES:], 0.0).astype(BF16)
            a_rbk[k] = jnp.concatenate([jnp.where(tri_i[k[0]], gm[C:, :LANES], 0.0),
                                        jnp.where(tri_i[k[0]], gm[C:, LANES:], 0.0)], axis=1).astype(BF16)
        tm = {k: eye_p + jnp.where((row_c >> 1) == (col_c >> 1), a_ab[k], 0.0) for k in chains}
        for lvl in range(1, 6):
            same = (row_c >> (lvl + 1)) == (col_c >> (lvl + 1))
            lower = (((row_c >> lvl) & 1) == 1) & (((col_c >> lvl) & 1) == 0)
            upper = (((row_c >> lvl) & 1) == 0) & (((col_c >> lvl) & 1) == 1)
            off = {d: same & (upper if reverse else lower) for d, _, _, reverse in ctxs}
            left = {k: _mm(tm[k], bd(jnp.where(off[k[0]], a_ab[k], zeros_cl))) for k in chains}
            tm = {k: tm[k] + _mm(left[k], bd(tm[k])) for k in chains}
        akv = {k: _mm(a_ak[k], bd(ch[k]["v"])) for k in chains}
        wu = {k: _mm(tm[k], jnp.concatenate([bd(ch[k]["at"]), bd(akv[k])], axis=1)).astype(BF16) for k in chains}

        def chunk_terms(k):
            e = ch[k]
            rhs = jnp.concatenate(
                [jnp.concatenate([bd(wu[k][:, :LANES]), bd(wu[k][:, LANES:])], axis=1),
                 jnp.concatenate([zeros_ll, bd(e["v"])], axis=1)], axis=0)
            qy = _mm(a_rbk[k], rhs)
            lhs = jnp.concatenate([wu[k], jnp.concatenate([zeros_cl, e["v"]], axis=1)], axis=0)
            pn = _mm_tn(lhs, jnp.concatenate([e["bh"], e["kh"]], axis=0))
            p_off = jnp.where(same_head, pn[:LANES], 0.0).astype(BF16)
            n0t = jnp.where(same_head, pn[LANES:], 0.0)
            return e["rt"] + qy[:, :LANES], qy[:, LANES:], e["gam"], p_off, n0t

        return chunk_terms

    def run_pair(tf, tb):
        qf = block_inputs(tf, False)
        qb = block_inputs(tb, True)
        ctxs = [(0, qf, list(range(nc)), False), (1, qb, list(range(nc - 1, -1, -1)), True)]
        chunk_terms = block_terms(ctxs)
        s = [s_scr[0], s_scr[1]]
        ys = [[None] * nc, [None] * nc]
        terms = {(d, order[0]): chunk_terms((d, order[0])) for d, _, order, _ in ctxs}
        for step in range(nc):
            if step + 1 < nc:
                for d, _, order, _ in ctxs:
                    terms[d, order[step + 1]] = chunk_terms((d, order[step + 1]))
            for d, _, order, _ in ctxs:
                qh, y0, gam, p_off, n0t = terms[d, order[step]]
                sb = s[d].astype(BF16)
                ys[d][order[step]] = _mm_nt(qh, sb) + y0
                s[d] = (s[d] * gam + n0t) + jnp.dot(sb, p_off, preferred_element_type=F32)
        s_scr[0] = s[0]
        s_scr[1] = s[1]
        return qf, jnp.concatenate(ys[0], axis=0), qb, jnp.concatenate(ys[1], axis=0)

    def finish(q, y, bonus, t0):
        mean = seg_sum(y) * (1.0 / RN)
        dy = y - mean
        var = seg_sum(dy * dy) * (1.0 / RN)
        gn = dy * lax.rsqrt(var + GN_EPS) * prow(PV_LNW) + prow(PV_LNB)
        gate = _mm(_sigmoid(q["lo"][:, 2 * LANES:]), wg_ref[0])
        out_ref[0, pl.ds(t0, blk), :] = ((gn + bonus) * gate).astype(out_ref.dtype)

    s_scr[...] = jnp.zeros(s_scr.shape, F32)

    def block_starts(i):
        return pl.multiple_of(i * blk, blk), pl.multiple_of((nblk - 1 - i) * blk, blk)

    def first_half(i, carry):
        tf, tb = block_starts(i)
        qf, yf, qb, yb = run_pair(tf, tb)
        y_scr[pl.ds(tf, blk), :] = yf
        bon_scr[pl.ds(tf, blk), :] = qf["bonus"]
        y_scr[pl.ds(tb, blk), :] = yb
        bon_scr[pl.ds(tb, blk), :] = qb["bonus"]
        return carry

    def second_half(i, carry):
        tf, tb = block_starts(i)
        qf, yf, qb, yb = run_pair(tf, tb)
        finish(qf, yf + y_scr[pl.ds(tf, blk), :], qf["bonus"] + bon_scr[pl.ds(tf, blk), :], tf)
        finish(qb, yb + y_scr[pl.ds(tb, blk), :], qb["bonus"] + bon_scr[pl.ds(tb, blk), :], tb)
        return carry

    lax.fori_loop(0, half, first_half, 0)
    lax.fori_loop(half, nblk, second_half, 0)


def _rwkv(rkv, lora, pvec, ww, wa, wg, blk):
    b, t, _ = rkv.shape
    assert t % (2 * blk) == 0 and blk % CUM_ROWS == 0
    nhp = RW // LANES
    col = lambda off: pl.BlockSpec((1, t, LANES), lambda i, j, off=off: (i, 0, off + j))
    body = functools.partial(_rwkv_body, seq_len=t, blk=blk)
    return pl.pallas_call(
        body,
        grid=(b, nhp),
        in_specs=[col(0), col(nhp), col(2 * nhp),
                  pl.BlockSpec((1, t, LORA_W), lambda i, j: (i, 0, 0)),
                  pl.BlockSpec((PV_ROWS, LANES), lambda i, j: (0, j)),
                  pl.BlockSpec((1, 2, LANES, LANES), lambda i, j: (j, 0, 0, 0)),
                  pl.BlockSpec((1, 2, LANES, LANES), lambda i, j: (j, 0, 0, 0)),
                  pl.BlockSpec((1, 2 * LANES, LANES), lambda i, j: (j, 0, 0))],
        out_specs=pl.BlockSpec((1, t, LANES), lambda i, j: (i, 0, j)),
        out_shape=jax.ShapeDtypeStruct((b, t, RW), BF16),
        scratch_shapes=[pltpu.VMEM((t, LANES), F32), pltpu.VMEM((t, LANES), F32),
                        pltpu.VMEM((2, LANES, LANES), F32)],
        compiler_params=_cparams(("parallel", "parallel")),
        name="rwkv",
    )(rkv, rkv, rkv, lora, pvec, ww, wa, wg)


def _gla_body(q_ref, k_ref, v_ref, gg_ref, gkd_ref, wgk_ref, gkb_ref, nw_ref, out_ref,
              o_scr, s_scr, *, seq_len, blk):
    C = CHUNK
    nc = blk // C
    nblk = seq_len // blk
    half = nblk // 2
    lane = _iota2((1, LANES), 1)
    head_mask = [lane < GK, lane >= GK]
    si = _iota2((2 * GV, LANES), 0)
    sj = _iota2((2 * GV, LANES), 1)
    same_head = (si >> GV_SHIFT) == (sj >> GK_SHIFT)
    row_c = _iota2((C, LANES), 0)
    col_c = _iota2((C, LANES), 1) & (C - 1)
    zeros_cv = jnp.zeros((C, GV), BF16)

    def run_pair(tf, tb):
        ctxs = [dict(d=0, rows=pl.ds(tf, blk), reverse=False, order=list(range(nc))),
                dict(d=1, rows=pl.ds(tb, blk), reverse=True, order=list(range(nc - 1, -1, -1)))]
        for cx in ctxs:
            cx["z"] = _mm(gkd_ref[0, cx["rows"], :], wgk_ref[0, cx["d"]]) + gkb_ref[0, cx["d"]]
        for cx in ctxs:
            z = cx["z"]
            lg = (jnp.minimum(z, 0.0) - jnp.log1p(jnp.exp(-jnp.abs(z)))) * (1.0 / GATE_NORM)
            cx["bcum"] = _chunk_cumsum(lg, cx["reverse"], 3)
        ch, chains, tri = {}, [], {}
        for cx in ctxs:
            d, reverse = cx["d"], cx["reverse"]
            tri[d] = (col_c >= row_c) if reverse else (col_c <= row_c)
            q = q_ref[0, cx["rows"], :] * (GK ** -0.5)
            k = k_ref[0, cx["rows"], :]
            v = v_ref[0, cx["rows"], :]
            for c in cx["order"]:
                sl = slice(c * C, (c + 1) * C)
                bc = cx["bcum"][sl]
                b_tot = bc[0:1, :] if reverse else bc[C - 1:C, :]
                mxu = lambda x: x.astype(BF16)
                ch[d, c] = dict(q_in=mxu(q[sl] * jnp.exp(bc)), k_in=mxu(k[sl] * jnp.exp(-bc)),
                                k_dec=mxu(k[sl] * jnp.exp(b_tot - bc)), v=mxu(v[sl]), gam=jnp.exp(b_tot))
                chains.append((d, c))
        att, intra, ds = {}, {}, {}
        for kk in chains:
            k_in = ch[kk]["k_in"]
            zero = jnp.zeros_like(k_in)
            k_bd = jnp.concatenate([jnp.where(head_mask[0], k_in, zero), jnp.where(head_mask[1], k_in, zero)], axis=0)
            att[kk] = jnp.where(tri[kk[0]], _mm_nt(ch[kk]["q_in"], k_bd), 0.0).astype(BF16)
        for kk in chains:
            v = ch[kk]["v"]
            v_bd = jnp.concatenate([jnp.concatenate([v[:, :GV], zeros_cv], axis=1),
                                    jnp.concatenate([zeros_cv, v[:, GV:]], axis=1)], axis=0)
            intra[kk] = _mm(att[kk], v_bd)
        for kk in chains:
            ds[kk] = jnp.where(same_head, _mm_tn(ch[kk]["v"], ch[kk]["k_dec"]), 0.0)
        s_in = {}
        for cx in ctxs:
            d = cx["d"]
            s = s_scr[d]
            for c in cx["order"]:
                s_in[d, c] = s
                s = s * ch[d, c]["gam"] + ds[d, c]
            s_scr[d] = s
        outs = {kk: _mm_nt(ch[kk]["q_in"], s_in[kk]) + intra[kk] for kk in chains}
        return (jnp.concatenate([outs[0, c] for c in range(nc)], axis=0),
                jnp.concatenate([outs[1, c] for c in range(nc)], axis=0))

    def finish(o, t0):
        gg = gg_ref[0, pl.ds(t0, blk), :]
        res = []
        for h in range(2):
            oh = o[:, h * GV:(h + 1) * GV]
            oh = oh * lax.rsqrt(jnp.mean(oh * oh, axis=-1, keepdims=True) + GLA_EPS) * nw_ref[...]
            gh = gg[:, h * GV:(h + 1) * GV]
            res.append(oh * (gh * _sigmoid(gh)))
        out_ref[0, pl.ds(t0, blk), :] = jnp.concatenate(res, axis=1).astype(out_ref.dtype)

    s_scr[...] = jnp.zeros(s_scr.shape, F32)

    def block_starts(i):
        return pl.multiple_of(i * blk, blk), pl.multiple_of((nblk - 1 - i) * blk, blk)

    def first_half(i, carry):
        tf, tb = block_starts(i)
        of, ob = run_pair(tf, tb)
        o_scr[pl.ds(tf, blk), :] = of
        o_scr[pl.ds(tb, blk), :] = ob
        return carry

    def second_half(i, carry):
        tf, tb = block_starts(i)
        of, ob = run_pair(tf, tb)
        finish(of + o_scr[pl.ds(tf, blk), :], tf)
        finish(ob + o_scr[pl.ds(tb, blk), :], tb)
        return carry

    lax.fori_loop(0, half, first_half, 0)
    lax.fori_loop(half, nblk, second_half, 0)


def _gla(qk, gv, gg, gkd, wgk, gkb, nw, blk):
    b, t, _ = qk.shape
    assert t % (2 * blk) == 0 and blk % CUM_ROWS == 0
    npair = GH // 2
    body = functools.partial(_gla_body, seq_len=t, blk=blk)
    return pl.pallas_call(
        body,
        grid=(b, npair),
        in_specs=[pl.BlockSpec((1, t, LANES), lambda i, j: (i, 0, j)),
                  pl.BlockSpec((1, t, LANES), lambda i, j: (i, 0, npair + j)),
                  pl.BlockSpec((1, t, 2 * GV), lambda i, j: (i, 0, j)),
                  pl.BlockSpec((1, t, 2 * GV), lambda i, j: (i, 0, j)),
                  pl.BlockSpec((1, t, GKD_W), lambda i, j: (i, 0, 0)),
                  pl.BlockSpec((1, 2, GKD_W, LANES), lambda i, j: (j, 0, 0, 0)),
                  pl.BlockSpec((1, 2, 1, LANES), lambda i, j: (j, 0, 0, 0)),
                  pl.BlockSpec((1, GV), lambda i, j: (0, 0))],
        out_specs=pl.BlockSpec((1, t, 2 * GV), lambda i, j: (i, 0, j)),
        out_shape=jax.ShapeDtypeStruct((b, t, GH * GV), BF16),
        scratch_shapes=[pltpu.VMEM((t, 2 * GV), F32), pltpu.VMEM((2, 2 * GV, LANES), F32)],
        compiler_params=_cparams(("parallel", "parallel")),
        name="gla",
    )(qk, qk, gv, gg, gkd, wgk, gkb, nw)


def _kvproj_body(m_ref, g_ref, w_ref, kv_ref):
    h = _rms(m_ref[...], g_ref[...], NORM_EPS).astype(BF16)
    kv_ref[...] = jnp.dot(h, w_ref[...], preferred_element_type=F32).astype(kv_ref.dtype)


def _kvproj(mem2d, g_mem, wkv, tm):
    n = mem2d.shape[0]
    return pl.pallas_call(
        _kvproj_body,
        grid=(n // tm,),
        in_specs=[pl.BlockSpec((tm, D_MODEL), lambda i: (i, 0)),
                  pl.BlockSpec((1, D_MODEL), lambda i: (0, 0)),
                  pl.BlockSpec((D_MODEL, 2 * D_MODEL), lambda i: (0, 0))],
        out_specs=pl.BlockSpec((tm, 2 * D_MODEL), lambda i: (i, 0)),
        out_shape=jax.ShapeDtypeStruct((n, 2 * D_MODEL), BF16),
        compiler_params=_cparams(("parallel",)),
        name="kvproj",
    )(mem2d, g_mem, wkv)


def _xattn_body(x_ref, rw_ref, gl_ref, kv_ref, wout_ref, gmix_ref, gpre_ref, wq_ref, wo_ref, gpost_ref, o_ref):
    tm = x_ref.shape[1]
    cols = lambda hd: slice(hd * X_HD, (hd + 1) * X_HD)
    heads = range(X_HEADS)

    def half_tile(rows):
        mixed = (jnp.dot(rw_ref[0, rows, :], wout_ref[0:RW, :], preferred_element_type=F32)
                 + jnp.dot(gl_ref[0, rows, :], wout_ref[RW:, :], preferred_element_type=F32))
        yield
        x = x_ref[0, rows, :] + _rms(mixed, gmix_ref[...], NORM_EPS)
        h = _rms(x, gpre_ref[...], NORM_EPS).astype(BF16)
        yield
        q = [jnp.dot(h, wq_ref[:, cols(hd)], preferred_element_type=F32).astype(BF16) for hd in heads]
        scores = [lax.dot_general(q[hd], kv_ref[0, :, cols(hd)], (((1,), (1,)), ((), ())),
                                  preferred_element_type=F32) * (X_HD ** -0.5) for hd in heads]
        yield
        probs = []
        for s in scores:
            e = jnp.exp(s - jnp.max(s, axis=-1, keepdims=True))
            probs.append((e * (1.0 / jnp.sum(e, axis=-1, keepdims=True))).astype(BF16))
        yield
        outs = [jnp.dot(probs[hd], kv_ref[0, :, D_MODEL + hd * X_HD:D_MODEL + (hd + 1) * X_HD],
                        preferred_element_type=F32).astype(BF16) for hd in heads]
        att = jnp.dot(jnp.concatenate(outs, axis=1), wo_ref[...], preferred_element_type=F32)
        yield
        o_ref[0, rows, :] = x + _rms(att, gpost_ref[...], NORM_EPS)

    first, second = half_tile(pl.ds(0, tm // 2)), half_tile(pl.ds(tm // 2, tm // 2))
    next(first)
    running = [second, first]
    while running:
        for gen in list(running):
            if next(gen, "done") == "done":
                running.remove(gen)


def _xattn(x3d, rw3d, gl3d, kv3d, w_out, g_mix_post, g_pre, wq, wo, g_post, tm):
    b, t, _ = x3d.shape
    row = lambda w: pl.BlockSpec((1, tm, w), lambda i, j: (i, j, 0))
    const = lambda r, c: pl.BlockSpec((r, c), lambda i, j: (0, 0))
    return pl.pallas_call(
        _xattn_body,
        grid=(b, t // tm),
        in_specs=[row(D_MODEL), row(RW), row(GH * GV),
                  pl.BlockSpec((1, N_MEM, 2 * D_MODEL), lambda i, j: (i, 0, 0)),
                  const(D_MODEL, D_MODEL), const(1, D_MODEL), const(1, D_MODEL),
                  const(D_MODEL, D_MODEL), const(D_MODEL, D_MODEL), const(1, D_MODEL)],
        out_specs=row(D_MODEL),
        out_shape=jax.ShapeDtypeStruct((b, t, D_MODEL), F32),
        compiler_params=_cparams(("parallel", "parallel")),
        name="xattn",
    )(x3d, rw3d, gl3d, kv3d, w_out, g_mix_post, g_pre, wq, wo, g_post)


def _ffn_body(x_ref, gpre_ref, w1_ref, w2_ref, gpost_ref, o_ref, *, tf):
    tm = x_ref.shape[0]

    def half_tile(rows):
        x = x_ref[rows, :]
        h = _rms(x, gpre_ref[...], NORM_EPS).astype(BF16)
        yield
        acc = None
        for lo in range(0, D_FF, tf):
            a = jnp.dot(h, w1_ref[:, lo:lo + tf], preferred_element_type=F32)
            yield
            a = jnp.square(jnp.maximum(a, 0.0)).astype(BF16)
            part = jnp.dot(a, w2_ref[lo:lo + tf, :], preferred_element_type=F32)
            acc = part if acc is None else acc + part
            yield
        o_ref[rows, :] = x + _rms(acc, gpost_ref[...], NORM_EPS)

    first, second = half_tile(pl.ds(0, tm // 2)), half_tile(pl.ds(tm // 2, tm // 2))
    next(first)
    running = [second, first]
    while running:
        for gen in list(running):
            if next(gen, "done") == "done":
                running.remove(gen)


def _ffn(x2d, g_pre, w1, w2, g_post, tm, tf):
    n = x2d.shape[0]
    const = lambda r, c: pl.BlockSpec((r, c), lambda i: (0, 0))
    return pl.pallas_call(
        functools.partial(_ffn_body, tf=tf),
        grid=(n // tm,),
        in_specs=[pl.BlockSpec((tm, D_MODEL), lambda i: (i, 0)), const(1, D_MODEL),
                  const(D_MODEL, D_FF), const(D_FF, D_MODEL), const(1, D_MODEL)],
        out_specs=pl.BlockSpec((tm, D_MODEL), lambda i: (i, 0)),
        out_shape=jax.ShapeDtypeStruct((n, D_MODEL), F32),
        compiler_params=_cparams(("parallel",)),
        name="ffn",
    )(x2d, g_pre, w1, w2, g_post)


def _pack_params(p):
    w_in = p["w_in"]
    g0 = R_COLS
    zeros = lambda n: jnp.zeros((D_MODEL, n), w_in.dtype)
    w_perm = jnp.concatenate([
        w_in[:, :R_COLS], zeros(LORA_W - (R_COLS - 3 * RW)),
        w_in[:, g0:g0 + 2 * GH * GK],
        w_in[:, g0 + 2 * GH * GK:g0 + 2 * GH * GK + GH * GV],
        w_in[:, g0 + 2 * GH * GK + GH * GV + GK_LORA:],
        w_in[:, g0 + 2 * GH * GK + GH * GV:g0 + 2 * GH * GK + GH * GV + GK_LORA], zeros(GKD_W - GK_LORA),
    ], axis=1).astype(BF16)
    row = lambda v: v.reshape(1, -1)
    mu_p, mu_n = p["mu_prev"], p["mu_next"]
    pvec = jnp.concatenate([
        row(p["w0_f"]), row(p["w0_b"]), row(p["a0_f"]), row(p["a0_b"]),
        row(p["k_k"]), row(p["k_a"]), row(p["r_k"]), row(p["lnx_w"]), row(p["lnx_b"]),
        jnp.zeros((PV_ROWS - 9, RW), F32)], axis=0)
    pad_mu = lambda v: jnp.pad(v, (0, C_QK - R_COLS))
    mu = jnp.stack([pad_mu(mu_p), pad_mu(mu_n)], axis=0)
    nhp = RW // LANES

    def lora_pair(wf, wb):
        wf = wf.reshape(-1, nhp, LANES).transpose(1, 0, 2)
        wb = wb.reshape(-1, nhp, LANES).transpose(1, 0, 2)
        z = jnp.zeros_like(wf)
        return jnp.stack([jnp.concatenate([wf, z], axis=1), jnp.concatenate([z, wb], axis=1)],
                         axis=1).astype(BF16)

    ww = lora_pair(p["w2_f"], p["w2_b"])
    wa = lora_pair(p["a2_f"], p["a2_b"])
    wg = jnp.pad(p["g2"], ((0, 2 * LANES - GATE_LORA), (0, 0)))
    wg = wg.reshape(2 * LANES, nhp, LANES).transpose(1, 0, 2).astype(BF16)
    npair = GH // 2

    def gk_pair(w):
        w = jnp.pad(w, ((0, GKD_W - GK_LORA), (0, 0)))
        return w.reshape(GKD_W, npair, LANES).transpose(1, 0, 2)

    wgk = jnp.stack([gk_pair(p["gk2_f"]), gk_pair(p["gk2_b"])], axis=1).astype(BF16)
    gkb = jnp.stack([p["gkb_f"].reshape(npair, 1, LANES), p["gkb_b"].reshape(npair, 1, LANES)], axis=1)
    return dict(
        w_perm=w_perm, pvec=pvec, mu=mu, ww=ww, wa=wa, wg=wg, wgk=wgk, gkb=gkb,
        nw=row(p["gla_norm_w"]),
        g_mix_pre=row(p["g_mix_pre"]), g_mix_post=row(p["g_mix_post"]),
        g_x_pre=row(p["g_x_pre"]), g_x_post=row(p["g_x_post"]), g_mem=row(p["g_mem"]),
        g_ffn_pre=row(p["g_ffn_pre"]), g_ffn_post=row(p["g_ffn_post"]),
        w_out=p["w_out"].astype(BF16), wq=p["wq_x"].astype(BF16), wkv=p["wkv_x"].astype(BF16),
        wo=p["wo_x"].astype(BF16), w1=p["w_ff1"].astype(BF16), w2=p["w_ff2"].astype(BF16))


def _pick(n, pref):
    t = pref
    while n % t:
        t //= 2
    return t


def _trunk(x, mem, pk):
    b, t, _ = x.shape
    n = b * t
    tm = _pick(n, 1024)
    x2d = x.reshape(n, D_MODEL)
    rkv, lora, qk, gv, gg, gkd = _inproj(x2d, pk["g_mix_pre"], pk["w_perm"], pk["mu"], _pick(t, 512), t)
    r3 = lambda a: a.reshape(b, t, a.shape[-1])
    blk = _pick(t // 2, 1024)
    rw = _rwkv(r3(rkv), r3(lora), pk["pvec"], pk["ww"], pk["wa"], pk["wg"], blk)
    gl = _gla(r3(qk), r3(gv), r3(gg), r3(gkd), pk["wgk"], pk["gkb"], pk["nw"], blk)
    nm = mem.shape[0] * mem.shape[1]
    kv = _kvproj(mem.reshape(nm, D_MODEL), pk["g_mem"], pk["wkv"], _pick(nm, 512))
    x2 = _xattn(x, rw, gl, kv.reshape(mem.shape[0], mem.shape[1], 2 * D_MODEL), pk["w_out"],
                pk["g_mix_post"], pk["g_x_pre"], pk["wq"], pk["wo"], pk["g_x_post"], _pick(t, 1024))
    y = _ffn(x2.reshape(n, D_MODEL), pk["g_ffn_pre"], pk["w1"], pk["w2"], pk["g_ffn_post"],
             _pick(n, 1024), 1024)
    return y.reshape(b, t, D_MODEL)


def kernel(x_prompt, x_sample, mem_prompt, mem_sample, g_mix_pre, w_in, mu_prev, mu_next, w0_f, w2_f, w0_b, w2_b, a0_f, a2_f, a0_b, a2_b, g2, k_k, k_a, r_k, lnx_w, lnx_b, gk2_f, gkb_f, gk2_b, gkb_b, gla_norm_w, w_out, g_mix_post, g_x_pre, g_mem, wq_x, wkv_x, wo_x, g_x_post, g_ffn_pre, w_ff1, w_ff2, g_ffn_post):
    params = dict(
        g_mix_pre=g_mix_pre, w_in=w_in, mu_prev=mu_prev, mu_next=mu_next, w0_f=w0_f, w2_f=w2_f,
        w0_b=w0_b, w2_b=w2_b, a0_f=a0_f, a2_f=a2_f, a0_b=a0_b, a2_b=a2_b, g2=g2, k_k=k_k, k_a=k_a,
        r_k=r_k, lnx_w=lnx_w, lnx_b=lnx_b, gk2_f=gk2_f, gkb_f=gkb_f, gk2_b=gk2_b, gkb_b=gkb_b,
        gla_norm_w=gla_norm_w, w_out=w_out, g_mix_post=g_mix_post, g_x_pre=g_x_pre, g_mem=g_mem,
        wq_x=wq_x, wkv_x=wkv_x, wo_x=wo_x, g_x_post=g_x_post, g_ffn_pre=g_ffn_pre, w_ff1=w_ff1,
        w_ff2=w_ff2, g_ffn_post=g_ffn_post)
    assert w_in.shape[0] == 1, "single-layer stack expected"
    pk = _pack_params({name: w[0] for name, w in params.items()})
    return (_trunk(x_prompt, mem_prompt, pk), _trunk(x_sample, mem_sample, pk))
```

```python
import functools
import math

import jax
import jax.numpy as jnp
from jax import lax
from jax.experimental import pallas as pl
from jax.experimental.pallas import tpu as pltpu

F32 = jnp.float32
BF16 = jnp.bfloat16

D_MODEL = 1024
RW = 512
RN = 64
DECAY_LORA = 64
AAA_LORA = 64
GATE_LORA = 160
GN_EPS = 64e-5
GH = 4
GV = 128
GK = 64
GK_LORA = 16
GATE_NORM = 16.0
CHUNK = 64
GLA_EPS = 1e-5
R_COLS = 3 * RW + 2 * DECAY_LORA + 2 * AAA_LORA + GATE_LORA
N_MEM = 256
X_HEADS = 4
X_HD = D_MODEL // X_HEADS
D_FF = 4 * D_MODEL
NORM_EPS = 1e-6
CHUNK_SHIFT, RN_SHIFT, GV_SHIFT, GK_SHIFT = (n.bit_length() - 1 for n in (CHUNK, RN, GV, GK))
assert (1 << CHUNK_SHIFT, 1 << RN_SHIFT, 1 << GV_SHIFT, 1 << GK_SHIFT) == (CHUNK, RN, GV, GK)

LANES = 128
CUM_ROWS = 2 * CHUNK
HALO = 16
SHIFT_COLS = 512
LORA_W = 512
GKD_W = 128
C_RKV, C_LORA, C_QK, C_GV, C_GG, C_GKD = 0, 1536, 2048, 2560, 3072, 3584
N_PROJ = 3712
VMEM_LIMIT = 56 * 1024 * 1024


def _cparams(sem):
    return pltpu.CompilerParams(dimension_semantics=sem, vmem_limit_bytes=VMEM_LIMIT)


def _mm(a, b):
    return jnp.dot(a.astype(BF16), b.astype(BF16), preferred_element_type=F32)


def _mm_nt(a, b):
    return lax.dot_general(a.astype(BF16), b.astype(BF16), (((1,), (1,)), ((), ())),
                           preferred_element_type=F32)


def _mm_tn(a, b):
    return lax.dot_general(a.astype(BF16), b.astype(BF16), (((0,), (0,)), ((), ())),
                           preferred_element_type=F32)


def _mm_lhs01(m01, x, terms):
    acc = None
    rem = x
    for i in range(terms):
        piece = rem.astype(BF16)
        part = jnp.dot(m01, piece, preferred_element_type=F32)
        acc = part if acc is None else acc + part
        if i + 1 < terms:
            rem = rem - piece.astype(F32)
    return acc


def _rms(x, g, eps):
    return x * lax.rsqrt(jnp.mean(x * x, axis=-1, keepdims=True) + eps) * g


def _sigmoid(x):
    return 1.0 / (1.0 + jnp.exp(-x))


def _iota2(shape, dim):
    return lax.broadcasted_iota(jnp.int32, shape, dim)


def _chunk_tri(n, reverse):
    t = _iota2((n, n), 0)
    s = _iota2((n, n), 1)
    same = (t >> CHUNK_SHIFT) == (s >> CHUNK_SHIFT)
    tri = (s >= t) if reverse else (s <= t)
    return jnp.where(same & tri, 1.0, 0.0).astype(BF16)


def _chunk_cumsum(x, reverse, terms):
    tri = _chunk_tri(CUM_ROWS, reverse)
    parts = [_mm_lhs01(tri, x[i:i + CUM_ROWS], terms) for i in range(0, x.shape[0], CUM_ROWS)]
    return parts[0] if len(parts) == 1 else jnp.concatenate(parts, axis=0)


def _inproj_body(x_ref, xp_ref, xn_ref, g_ref, w_ref, mu_ref, rkv_ref, lora_ref, qk_ref, gv_ref, gg_ref,
                 gkd_ref, *, tiles_per_seq):
    tm = x_ref.shape[0]
    i = pl.program_id(0)
    g = g_ref[...]
    h = _rms(x_ref[...], g, NORM_EPS).astype(BF16)
    h_prev = _rms(xp_ref[...], g, NORM_EPS).astype(BF16)
    h_next = _rms(xn_ref[...], g, NORM_EPS).astype(BF16)
    h_ext = jnp.concatenate([h_prev, h, h_next], axis=0)
    pos = i % tiles_per_seq
    has_prev = pos > 0
    has_next = pos < tiles_per_seq - 1
    rid = _iota2((tm, SHIFT_COLS), 0)

    def proj(lo, hi):
        return jnp.dot(h, w_ref[:, lo:hi], preferred_element_type=F32)

    for lo in range(C_RKV, C_QK, SHIFT_COLS):
        p_ext = jnp.dot(h_ext, w_ref[:, lo:lo + SHIFT_COLS], preferred_element_type=F32)
        p = p_ext[HALO:HALO + tm, :]
        p_before = p_ext[HALO - 1:HALO, :]
        p_after = p_ext[HALO + tm:HALO + tm + 1, :]
        prev = jnp.where(rid == 0, jnp.where(has_prev, p_before, 0.0), pltpu.roll(p, 1, 0))
        nxt = jnp.where(rid == tm - 1, jnp.where(has_next, p_after, 0.0), pltpu.roll(p, tm - 1, 0))
        mu = mu_ref[:, lo:lo + SHIFT_COLS]
        shifted = p + mu[0:1, :] * (prev - p) + mu[1:2, :] * (nxt - p)
        if lo < C_LORA:
            rkv_ref[:, lo:lo + SHIFT_COLS] = shifted
        else:
            lora_ref[:, lo - C_LORA:lo - C_LORA + SHIFT_COLS] = shifted
    qk_ref[...] = proj(C_QK, C_GV)
    gv_ref[...] = proj(C_GV, C_GG)
    gg_ref[...] = proj(C_GG, C_GKD)
    gkd_ref[...] = proj(C_GKD, N_PROJ)


def _inproj(x2d, g, w_perm, mu, tm, seq_len):
    n = x2d.shape[0]
    assert seq_len % tm == 0 and tm % HALO == 0
    widths = (C_LORA - C_RKV, C_QK - C_LORA, C_GV - C_QK, C_GG - C_GV, C_GKD - C_GG, N_PROJ - C_GKD)
    hb = tm // HALO
    last = n // HALO - 1
    body = functools.partial(_inproj_body, tiles_per_seq=seq_len // tm)
    return pl.pallas_call(
        body,
        grid=(n // tm,),
        in_specs=[pl.BlockSpec((tm, D_MODEL), lambda i: (i, 0)),
                  pl.BlockSpec((HALO, D_MODEL), lambda i: (jnp.maximum(i * hb - 1, 0), 0)),
                  pl.BlockSpec((HALO, D_MODEL), lambda i: (jnp.minimum((i + 1) * hb, last), 0)),
                  pl.BlockSpec((1, D_MODEL), lambda i: (0, 0)),
                  pl.BlockSpec((D_MODEL, N_PROJ), lambda i: (0, 0)),
                  pl.BlockSpec((2, C_QK), lambda i: (0, 0))],
        out_specs=[pl.BlockSpec((tm, w), lambda i: (i, 0)) for w in widths],
        out_shape=[jax.ShapeDtypeStruct((n, w), F32) for w in widths],
        compiler_params=_cparams(("parallel",)),
        name="inproj",
    )(x2d, x2d, x2d, g, w_perm, mu)


PV_W0_F, PV_W0_B, PV_A0_F, PV_A0_B, PV_KK, PV_KA, PV_RK, PV_LNW, PV_LNB = range(9)
PV_ROWS = 16


def _rwkv_body(r_ref, k_ref, v_ref, lo_ref, pv_ref, ww_ref, wa_ref, wg_ref, out_ref,
               y_scr, bon_scr, s_scr, *, seq_len, blk):
    C = CHUNK
    nc = blk // C
    nblk = seq_len // blk
    half = nblk // 2
    pv = pv_ref[...]
    prow = lambda i: pv[i:i + 1, :]

    lane = _iota2((1, LANES), 1)
    head_mask = [lane < RN, lane >= RN]
    li = _iota2((LANES, LANES), 0)
    lj = _iota2((LANES, LANES), 1)
    same_head = (li >> RN_SHIFT) == (lj >> RN_SHIFT)
    seg01 = jnp.where(same_head, 1.0, 0.0).astype(BF16)
    row_c = _iota2((C, LANES), 0)
    col_c = _iota2((C, LANES), 1) & (C - 1)
    eye_p = jnp.where(row_c == col_c, 1.0, 0.0)
    zeros_cl = jnp.zeros((C, LANES), BF16)
    zeros_ll = jnp.zeros((LANES, LANES), BF16)

    def seg_sum(x):
        return _mm(x, seg01)

    def block_inputs(t0, reverse):
        d = 1 if reverse else 0
        rows = pl.ds(t0, blk)
        r, k, v, lo = r_ref[0, rows, :], k_ref[0, rows, :], v_ref[0, rows, :], lo_ref[0, rows, :]
        zw = _mm(jnp.tanh(lo[:, 0:LANES]), ww_ref[0, d])
        za = _mm(lo[:, LANES:2 * LANES], wa_ref[0, d])
        lw = -_sigmoid(prow(PV_W0_F + d) + zw) * math.exp(-0.5)
        a = _sigmoid(prow(PV_A0_F + d) + za)
        kd = k * (1.0 + (a - 1.0) * prow(PV_KA))
        kk = k * prow(PV_KK)
        kk = kk * lax.rsqrt(seg_sum(kk * kk) + 1e-12)
        bonus = seg_sum(r * kd * prow(PV_RK)) * v
        g = _chunk_cumsum(lw, reverse, 2)
        return dict(r=r, v=v, kd=kd, kk=kk, a=a, lw=lw, g=g, bonus=bonus, lo=lo)

    def bd(x):
        xb = x.astype(BF16)
        zero = jnp.zeros_like(xb)
        return jnp.concatenate([jnp.where(head_mask[0], xb, zero), jnp.where(head_mask[1], xb, zero)], axis=0)

    def block_terms(ctxs):
        tri_s, tri_i, ch, chains = {}, {}, {}, []
        for d, q, order, reverse in ctxs:
            tri_s[d] = (col_c > row_c) if reverse else (col_c < row_c)
            tri_i[d] = (col_c >= row_c) if reverse else (col_c <= row_c)
            for c in order:
                sl = slice(c * C, (c + 1) * C)
                g, lw, kk = q["g"][sl], q["lw"][sl], q["kk"][sl]
                r, v, kd, a = q["r"][sl], q["v"][sl], q["kd"][sl], q["a"][sl]
                g_tot = g[0:1, :] if reverse else g[C - 1:C, :]
                eng = jnp.exp(-g)
                beta = kk * a
                dec = jnp.exp(g_tot - g)
                mxu = lambda x: x.astype(BF16)
                ch[d, c] = dict(v=mxu(v), rt=r * jnp.exp(g), at=mxu(-kk * jnp.exp(g - lw)), bt=mxu(beta * eng),
                                kt=mxu(kd * eng), bh=mxu(beta * dec), kh=mxu(kd * dec), gam=jnp.exp(g_tot))
                chains.append((d, c))
        a_ab, a_ak, a_rbk = {}, {}, {}
        for k in chains:
            e = ch[k]
            gm = _mm_nt(jnp.concatenate([e["at"], e["rt"].astype(BF16)], axis=0),
                        jnp.concatenate([bd(e["bt"]), bd(e["kt"])], axis=0))
            a_ab[k] = jnp.where(tri_s[k[0]], gm[:C, :LANES], 0.0).astype(BF16)
            a_ak[k] = jnp.where(tri_s[k[0]], gm[:C, LANES:], 0.0).astype(BF16)
            a_rbk[k] = jnp.concatenate([jnp.where(tri_i[k[0]], gm[C:, :LANES], 0.0),
                                        jnp.where(tri_i[k[0]], gm[C:, LANES:], 0.0)], axis=1).astype(BF16)
        tm = {k: eye_p + jnp.where((row_c >> 1) == (col_c >> 1), a_ab[k], 0.0) for k in chains}
        for lvl in range(1, 6):
            same = (row_c >> (lvl + 1)) == (col_c >> (lvl + 1))
            lower = (((row_c >> lvl) & 1) == 1) & (((col_c >> lvl) & 1) == 0)
            upper = (((row_c >> lvl) & 1) == 0) & (((col_c >> lvl) & 1) == 1)
            off = {d: same & (upper if reverse else lower) for d, _, _, reverse in ctxs}
            left = {k: _mm(tm[k], bd(jnp.where(off[k[0]], a_ab[k], zeros_cl))) for k in chains}
            tm = {k: tm[k] + _mm(left[k], bd(tm[k])) for k in chains}
        akv = {k: _mm(a_ak[k], bd(ch[k]["v"])) for k in chains}
        wu = {k: _mm(tm[k], jnp.concatenate([bd(ch[k]["at"]), bd(akv[k])], axis=1)).astype(BF16) for k in chains}

        def chunk_terms(k):
            e = ch[k]
            rhs = jnp.concatenate(
                [jnp.concatenate([bd(wu[k][:, :LANES]), bd(wu[k][:, LANES:])], axis=1),
                 jnp.concatenate([zeros_ll, bd(e["v"])], axis=1)], axis=0)
            qy = _mm(a_rbk[k], rhs)
            lhs = jnp.concatenate([wu[k], jnp.concatenate([zeros_cl, e["v"]], axis=1)], axis=0)
            pn = _mm_tn(lhs, jnp.concatenate([e["bh"], e["kh"]], axis=0))
            p_off = jnp.where(same_head, pn[:LANES], 0.0).astype(BF16)
            n0t = jnp.where(same_head, pn[LANES:], 0.0)
            return e["rt"] + qy[:, :LANES], qy[:, LANES:], e["gam"], p_off, n0t

        return chunk_terms

    def run_pair(tf, tb):
        qf = block_inputs(tf, False)
        qb = block_inputs(tb, True)
        ctxs = [(0, qf, list(range(nc)), False), (1, qb, list(range(nc - 1, -1, -1)), True)]
        chunk_terms = block_terms(ctxs)
        s = [s_scr[0], s_scr[1]]
        ys = [[None] * nc, [None] * nc]
        terms = {(d, order[0]): chunk_terms((d, order[0])) for d, _, order, _ in ctxs}
        for step in range(nc):
            if step + 1 < nc:
                for d, _, order, _ in ctxs:
                    terms[d, order[step + 1]] = chunk_terms((d, order[step + 1]))
            for d, _, order, _ in ctxs:
                qh, y0, gam, p_off, n0t = terms[d, order[step]]
                sb = s[d].astype(BF16)
                ys[d][order[step]] = _mm_nt(qh, sb) + y0
                s[d] = (s[d] * gam + n0t) + jnp.dot(sb, p_off, preferred_element_type=F32)
        s_scr[0] = s[0]
        s_scr[1] = s[1]
        return qf, jnp.concatenate(ys[0], axis=0), qb, jnp.concatenate(ys[1], axis=0)

    def finish(q, y, bonus, t0):
        mean = seg_sum(y) * (1.0 / RN)
        dy = y - mean
        var = seg_sum(dy * dy) * (1.0 / RN)
        gn = dy * lax.rsqrt(var + GN_EPS) * prow(PV_LNW) + prow(PV_LNB)
        gate = _mm(_sigmoid(q["lo"][:, 2 * LANES:]), wg_ref[0])
        out_ref[0, pl.ds(t0, blk), :] = ((gn + bonus) * gate).astype(out_ref.dtype)

    s_scr[...] = jnp.zeros(s_scr.shape, F32)

    def block_starts(i):
        return pl.multiple_of(i * blk, blk), pl.multiple_of((nblk - 1 - i) * blk, blk)

    def first_half(i, carry):
        tf, tb = block_starts(i)
        qf, yf, qb, yb = run_pair(tf, tb)
        y_scr[pl.ds(tf, blk), :] = yf
        bon_scr[pl.ds(tf, blk), :] = qf["bonus"]
        y_scr[pl.ds(tb, blk), :] = yb
        bon_scr[pl.ds(tb, blk), :] = qb["bonus"]
        return carry

    def second_half(i, carry):
        tf, tb = block_starts(i)
        qf, yf, qb, yb = run_pair(tf, tb)
        finish(qf, yf + y_scr[pl.ds(tf, blk), :], qf["bonus"] + bon_scr[pl.ds(tf, blk), :], tf)
        finish(qb, yb + y_scr[pl.ds(tb, blk), :], qb["bonus"] + bon_scr[pl.ds(tb, blk), :], tb)
        return carry

    lax.fori_loop(0, half, first_half, 0)
    lax.fori_loop(half, nblk, second_half, 0)


def _rwkv(rkv, lora, pvec, ww, wa, wg, blk):
    b, t, _ = rkv.shape
    assert t % (2 * blk) == 0 and blk % CUM_ROWS == 0
    nhp = RW // LANES
    col = lambda off: pl.BlockSpec((1, t, LANES), lambda i, j, off=off: (i, 0, off + j))
    body = functools.partial(_rwkv_body, seq_len=t, blk=blk)
    return pl.pallas_call(
        body,
        grid=(b, nhp),
        in_specs=[col(0), col(nhp), col(2 * nhp),
                  pl.BlockSpec((1, t, LORA_W), lambda i, j: (i, 0, 0)),
                  pl.BlockSpec((PV_ROWS, LANES), lambda i, j: (0, j)),
                  pl.BlockSpec((1, 2, LANES, LANES), lambda i, j: (j, 0, 0, 0)),
                  pl.BlockSpec((1, 2, LANES, LANES), lambda i, j: (j, 0, 0, 0)),
                  pl.BlockSpec((1, 2 * LANES, LANES), lambda i, j: (j, 0, 0))],
        out_specs=pl.BlockSpec((1, t, LANES), lambda i, j: (i, 0, j)),
        out_shape=jax.ShapeDtypeStruct((b, t, RW), BF16),
        scratch_shapes=[pltpu.VMEM((t, LANES), F32), pltpu.VMEM((t, LANES), F32),
                        pltpu.VMEM((2, LANES, LANES), F32)],
        compiler_params=_cparams(("parallel", "parallel")),
        name="rwkv",
    )(rkv, rkv, rkv, lora, pvec, ww, wa, wg)


def _gla_body(q_ref, k_ref, v_ref, gg_ref, gkd_ref, wgk_ref, gkb_ref, nw_ref, out_ref,
              o_scr, s_scr, *, seq_len, blk):
    C = CHUNK
    nc = blk // C
    nblk = seq_len // blk
    half = nblk // 2
    lane = _iota2((1, LANES), 1)
    head_mask = [lane < GK, lane >= GK]
    si = _iota2((2 * GV, LANES), 0)
    sj = _iota2((2 * GV, LANES), 1)
    same_head = (si >> GV_SHIFT) == (sj >> GK_SHIFT)
    row_c = _iota2((C, LANES), 0)
    col_c = _iota2((C, LANES), 1) & (C - 1)
    zeros_cv = jnp.zeros((C, GV), BF16)

    def run_pair(tf, tb):
        ctxs = [dict(d=0, rows=pl.ds(tf, blk), reverse=False, order=list(range(nc))),
                dict(d=1, rows=pl.ds(tb, blk), reverse=True, order=list(range(nc - 1, -1, -1)))]
        for cx in ctxs:
            cx["z"] = _mm(gkd_ref[0, cx["rows"], :], wgk_ref[0, cx["d"]]) + gkb_ref[0, cx["d"]]
        for cx in ctxs:
            z = cx["z"]
            lg = (jnp.minimum(z, 0.0) - jnp.log1p(jnp.exp(-jnp.abs(z)))) * (1.0 / GATE_NORM)
            cx["bcum"] = _chunk_cumsum(lg, cx["reverse"], 3)
        ch, chains, tri = {}, [], {}
        for cx in ctxs:
            d, reverse = cx["d"], cx["reverse"]
            tri[d] = (col_c >= row_c) if reverse else (col_c <= row_c)
            q = q_ref[0, cx["rows"], :] * (GK ** -0.5)
            k = k_ref[0, cx["rows"], :]
            v = v_ref[0, cx["rows"], :]
            for c in cx["order"]:
                sl = slice(c * C, (c + 1) * C)
                bc = cx["bcum"][sl]
                b_tot = bc[0:1, :] if reverse else bc[C - 1:C, :]
                mxu = lambda x: x.astype(BF16)
                ch[d, c] = dict(q_in=mxu(q[sl] * jnp.exp(bc)), k_in=mxu(k[sl] * jnp.exp(-bc)),
                                k_dec=mxu(k[sl] * jnp.exp(b_tot - bc)), v=mxu(v[sl]), gam=jnp.exp(b_tot))
                chains.append((d, c))
        att, intra, ds = {}, {}, {}
        for kk in chains:
            k_in = ch[kk]["k_in"]
            zero = jnp.zeros_like(k_in)
            k_bd = jnp.concatenate([jnp.where(head_mask[0], k_in, zero), jnp.where(head_mask[1], k_in, zero)], axis=0)
            att[kk] = jnp.where(tri[kk[0]], _mm_nt(ch[kk]["q_in"], k_bd), 0.0).astype(BF16)
        for kk in chains:
            v = ch[kk]["v"]
            v_bd = jnp.concatenate([jnp.concatenate([v[:, :GV], zeros_cv], axis=1),
                                    jnp.concatenate([zeros_cv, v[:, GV:]], axis=1)], axis=0)
            intra[kk] = _mm(att[kk], v_bd)
        for kk in chains:
            ds[kk] = jnp.where(same_head, _mm_tn(ch[kk]["v"], ch[kk]["k_dec"]), 0.0)
        s_in = {}
        for cx in ctxs:
            d = cx["d"]
            s = s_scr[d]
            for c in cx["order"]:
                s_in[d, c] = s
                s = s * ch[d, c]["gam"] + ds[d, c]
            s_scr[d] = s
        outs = {kk: _mm_nt(ch[kk]["q_in"], s_in[kk]) + intra[kk] for kk in chains}
        return (jnp.concatenate([outs[0, c] for c in range(nc)], axis=0),
                jnp.concatenate([outs[1, c] for c in range(nc)], axis=0))

    def finish(o, t0):
        gg = gg_ref[0, pl.ds(t0, blk), :]
        res = []
        for h in range(2):
            oh = o[:, h * GV:(h + 1) * GV]
            oh = oh * lax.rsqrt(jnp.mean(oh * oh, axis=-1, keepdims=True) + GLA_EPS) * nw_ref[...]
            gh = gg[:, h * GV:(h + 1) * GV]
            res.append(oh * (gh * _sigmoid(gh)))
        out_ref[0, pl.ds(t0, blk), :] = jnp.concatenate(res, axis=1).astype(out_ref.dtype)

    s_scr[...] = jnp.zeros(s_scr.shape, F32)

    def block_starts(i):
        return pl.multiple_of(i * blk, blk), pl.multiple_of((nblk - 1 - i) * blk, blk)

    def first_half(i, carry):
        tf, tb = block_starts(i)
        of, ob = run_pair(tf, tb)
        o_scr[pl.ds(tf, blk), :] = of
        o_scr[pl.ds(tb, blk), :] = ob
        return carry

    def second_half(i, carry):
        tf, tb = block_starts(i)
        of, ob = run_pair(tf, tb)
        finish(of + o_scr[pl.ds(tf, blk), :], tf)
        finish(ob + o_scr[pl.ds(tb, blk), :], tb)
        return carry

    lax.fori_loop(0, half, first_half, 0)
    lax.fori_loop(half, nblk, second_half, 0)


def _gla(qk, gv, gg, gkd, wgk, gkb, nw, blk):
    b, t, _ = qk.shape
    assert t % (2 * blk) == 0 and blk % CUM_ROWS == 0
    npair = GH // 2
    body = functools.partial(_gla_body, seq_len=t, blk=blk)
    return pl.pallas_call(
        body,
        grid=(b, npair),
        in_specs=[pl.BlockSpec((1, t, LANES), lambda i, j: (i, 0, j)),
                  pl.BlockSpec((1, t, LANES), lambda i, j: (i, 0, npair + j)),
                  pl.BlockSpec((1, t, 2 * GV), lambda i, j: (i, 0, j)),
                  pl.BlockSpec((1, t, 2 * GV), lambda i, j: (i, 0, j)),
                  pl.BlockSpec((1, t, GKD_W), lambda i, j: (i, 0, 0)),
                  pl.BlockSpec((1, 2, GKD_W, LANES), lambda i, j: (j, 0, 0, 0)),
                  pl.BlockSpec((1, 2, 1, LANES), lambda i, j: (j, 0, 0, 0)),
                  pl.BlockSpec((1, GV), lambda i, j: (0, 0))],
        out_specs=pl.BlockSpec((1, t, 2 * GV), lambda i, j: (i, 0, j)),
        out_shape=jax.ShapeDtypeStruct((b, t, GH * GV), BF16),
        scratch_shapes=[pltpu.VMEM((t, 2 * GV), F32), pltpu.VMEM((2, 2 * GV, LANES), F32)],
        compiler_params=_cparams(("parallel", "parallel")),
        name="gla",
    )(qk, qk, gv, gg, gkd, wgk, gkb, nw)


def _kvproj_body(m_ref, g_ref, w_ref, kv_ref):
    h = _rms(m_ref[...], g_ref[...], NORM_EPS).astype(BF16)
    kv_ref[...] = jnp.dot(h, w_ref[...], preferred_element_type=F32).astype(kv_ref.dtype)


def _kvproj(mem2d, g_mem, wkv, tm):
    n = mem2d.shape[0]
    return pl.pallas_call(
        _kvproj_body,
        grid=(n // tm,),
        in_specs=[pl.BlockSpec((tm, D_MODEL), lambda i: (i, 0)),
                  pl.BlockSpec((1, D_MODEL), lambda i: (0, 0)),
                  pl.BlockSpec((D_MODEL, 2 * D_MODEL), lambda i: (0, 0))],
        out_specs=pl.BlockSpec((tm, 2 * D_MODEL), lambda i: (i, 0)),
        out_shape=jax.ShapeDtypeStruct((n, 2 * D_MODEL), BF16),
        compiler_params=_cparams(("parallel",)),
        name="kvproj",
    )(mem2d, g_mem, wkv)


def _xattn_body(x_ref, rw_ref, gl_ref, kv_ref, wout_ref, gmix_ref, gpre_ref, wq_ref, wo_ref, gpost_ref, o_ref):
    tm = x_ref.shape[1]
    cols = lambda hd: slice(hd * X_HD, (hd + 1) * X_HD)
    heads = range(X_HEADS)

    def half_tile(rows):
        mixed = (jnp.dot(rw_ref[0, rows, :], wout_ref[0:RW, :], preferred_element_type=F32)
                 + jnp.dot(gl_ref[0, rows, :], wout_ref[RW:, :], preferred_element_type=F32))
        yield
        x = x_ref[0, rows, :] + _rms(mixed, gmix_ref[...], NORM_EPS)
        h = _rms(x, gpre_ref[...], NORM_EPS).astype(BF16)
        yield
        q = [jnp.dot(h, wq_ref[:, cols(hd)], preferred_element_type=F32).astype(BF16) for hd in heads]
        scores = [lax.dot_general(q[hd], kv_ref[0, :, cols(hd)], (((1,), (1,)), ((), ())),
                                  preferred_element_type=F32) * (X_HD ** -0.5) for hd in heads]
        yield
        probs = []
        for s in scores:
            e = jnp.exp(s - jnp.max(s, axis=-1, keepdims=True))
            probs.append((e * (1.0 / jnp.sum(e, axis=-1, keepdims=True))).astype(BF16))
        yield
        outs = [jnp.dot(probs[hd], kv_ref[0, :, D_MODEL + hd * X_HD:D_MODEL + (hd + 1) * X_HD],
                        preferred_element_type=F32).astype(BF16) for hd in heads]
        att = jnp.dot(jnp.concatenate(outs, axis=1), wo_ref[...], preferred_element_type=F32)
        yield
        o_ref[0, rows, :] = x + _rms(att, gpost_ref[...], NORM_EPS)

    first, second = half_tile(pl.ds(0, tm // 2)), half_tile(pl.ds(tm // 2, tm // 2))
    next(first)
    running = [second, first]
    while running:
        for gen in list(running):
            if next(gen, "done") == "done":
                running.remove(gen)


def _xattn(x3d, rw3d, gl3d, kv3d, w_out, g_mix_post, g_pre, wq, wo, g_post, tm):
    b, t, _ = x3d.shape
    row = lambda w: pl.BlockSpec((1, tm, w), lambda i, j: (i, j, 0))
    const = lambda r, c: pl.BlockSpec((r, c), lambda i, j: (0, 0))
    return pl.pallas_call(
        _xattn_body,
        grid=(b, t // tm),
        in_specs=[row(D_MODEL), row(RW), row(GH * GV),
                  pl.BlockSpec((1, N_MEM, 2 * D_MODEL), lambda i, j: (i, 0, 0)),
                  const(D_MODEL, D_MODEL), const(1, D_MODEL), const(1, D_MODEL),
                  const(D_MODEL, D_MODEL), const(D_MODEL, D_MODEL), const(1, D_MODEL)],
        out_specs=row(D_MODEL),
        out_shape=jax.ShapeDtypeStruct((b, t, D_MODEL), F32),
        compiler_params=_cparams(("parallel", "parallel")),
        name="xattn",
    )(x3d, rw3d, gl3d, kv3d, w_out, g_mix_post, g_pre, wq, wo, g_post)


def _ffn_body(x_ref, gpre_ref, w1_ref, w2_ref, gpost_ref, o_ref, *, tf):
    tm = x_ref.shape[0]

    def half_tile(rows):
        x = x_ref[rows, :]
        h = _rms(x, gpre_ref[...], NORM_EPS).astype(BF16)
        yield
        acc = None
        for lo in range(0, D_FF, tf):
            a = jnp.dot(h, w1_ref[:, lo:lo + tf], preferred_element_type=F32)
            yield
            a = jnp.square(jnp.maximum(a, 0.0)).astype(BF16)
            part = jnp.dot(a, w2_ref[lo:lo + tf, :], preferred_element_type=F32)
            acc = part if acc is None else acc + part
            yield
        o_ref[rows, :] = x + _rms(acc, gpost_ref[...], NORM_EPS)

    first, second = half_tile(pl.ds(0, tm // 2)), half_tile(pl.ds(tm // 2, tm // 2))
    next(first)
    running = [second, first]
    while running:
        for gen in list(running):
            if next(gen, "done") == "done":
                running.remove(gen)


def _ffn(x2d, g_pre, w1, w2, g_post, tm, tf):
    n = x2d.shape[0]
    const = lambda r, c: pl.BlockSpec((r, c), lambda i: (0, 0), pipeline_mode=pl.Buffered(1))
    return pl.pallas_call(
        functools.partial(_ffn_body, tf=tf),
        grid=(n // tm,),
        in_specs=[pl.BlockSpec((tm, D_MODEL), lambda i: (i, 0)), const(1, D_MODEL),
                  const(D_MODEL, D_FF), const(D_FF, D_MODEL), const(1, D_MODEL)],
        out_specs=pl.BlockSpec((tm, D_MODEL), lambda i: (i, 0)),
        out_shape=jax.ShapeDtypeStruct((n, D_MODEL), F32),
        compiler_params=_cparams(("parallel",)),
        name="ffn",
    )(x2d, g_pre, w1, w2, g_post)


def _pack_params(p):
    w_in = p["w_in"]
    g0 = R_COLS
    zeros = lambda n: jnp.zeros((D_MODEL, n), w_in.dtype)
    w_perm = jnp.concatenate([
        w_in[:, :R_COLS], zeros(LORA_W - (R_COLS - 3 * RW)),
        w_in[:, g0:g0 + 2 * GH * GK],
        w_in[:, g0 + 2 * GH * GK:g0 + 2 * GH * GK + GH * GV],
        w_in[:, g0 + 2 * GH * GK + GH * GV + GK_LORA:],
        w_in[:, g0 + 2 * GH * GK + GH * GV:g0 + 2 * GH * GK + GH * GV + GK_LORA], zeros(GKD_W - GK_LORA),
    ], axis=1).astype(BF16)
    row = lambda v: v.reshape(1, -1)
    mu_p, mu_n = p["mu_prev"], p["mu_next"]
    pvec = jnp.concatenate([
        row(p["w0_f"]), row(p["w0_b"]), row(p["a0_f"]), row(p["a0_b"]),
        row(p["k_k"]), row(p["k_a"]), row(p["r_k"]), row(p["lnx_w"]), row(p["lnx_b"]),
        jnp.zeros((PV_ROWS - 9, RW), F32)], axis=0)
    pad_mu = lambda v: jnp.pad(v, (0, C_QK - R_COLS))
    mu = jnp.stack([pad_mu(mu_p), pad_mu(mu_n)], axis=0)
    nhp = RW // LANES

    def lora_pair(wf, wb):
        wf = wf.reshape(-1, nhp, LANES).transpose(1, 0, 2)
        wb = wb.reshape(-1, nhp, LANES).transpose(1, 0, 2)
        z = jnp.zeros_like(wf)
        return jnp.stack([jnp.concatenate([wf, z], axis=1), jnp.concatenate([z, wb], axis=1)],
                         axis=1).astype(BF16)

    ww = lora_pair(p["w2_f"], p["w2_b"])
    wa = lora_pair(p["a2_f"], p["a2_b"])
    wg = jnp.pad(p["g2"], ((0, 2 * LANES - GATE_LORA), (0, 0)))
    wg = wg.reshape(2 * LANES, nhp, LANES).transpose(1, 0, 2).astype(BF16)
    npair = GH // 2

    def gk_pair(w):
        w = jnp.pad(w, ((0, GKD_W - GK_LORA), (0, 0)))
        return w.reshape(GKD_W, npair, LANES).transpose(1, 0, 2)

    wgk = jnp.stack([gk_pair(p["gk2_f"]), gk_pair(p["gk2_b"])], axis=1).astype(BF16)
    gkb = jnp.stack([p["gkb_f"].reshape(npair, 1, LANES), p["gkb_b"].reshape(npair, 1, LANES)], axis=1)
    return dict(
        w_perm=w_perm, pvec=pvec, mu=mu, ww=ww, wa=wa, wg=wg, wgk=wgk, gkb=gkb,
        nw=row(p["gla_norm_w"]),
        g_mix_pre=row(p["g_mix_pre"]), g_mix_post=row(p["g_mix_post"]),
        g_x_pre=row(p["g_x_pre"]), g_x_post=row(p["g_x_post"]), g_mem=row(p["g_mem"]),
        g_ffn_pre=row(p["g_ffn_pre"]), g_ffn_post=row(p["g_ffn_post"]),
        w_out=p["w_out"].astype(BF16), wq=p["wq_x"].astype(BF16), wkv=p["wkv_x"].astype(BF16),
        wo=p["wo_x"].astype(BF16), w1=p["w_ff1"].astype(BF16), w2=p["w_ff2"].astype(BF16))


def _pick(n, pref):
    t = pref
    while n % t:
        t //= 2
    return t


def _trunk(x, mem, pk):
    b, t, _ = x.shape
    n = b * t
    tm = _pick(n, 1024)
    x2d = x.reshape(n, D_MODEL)
    rkv, lora, qk, gv, gg, gkd = _inproj(x2d, pk["g_mix_pre"], pk["w_perm"], pk["mu"], _pick(t, 512), t)
    r3 = lambda a: a.reshape(b, t, a.shape[-1])
    blk = _pick(t // 2, 1024)
    rw = _rwkv(r3(rkv), r3(lora), pk["pvec"], pk["ww"], pk["wa"], pk["wg"], blk)
    gl = _gla(r3(qk), r3(gv), r3(gg), r3(gkd), pk["wgk"], pk["gkb"], pk["nw"], blk)
    nm = mem.shape[0] * mem.shape[1]
    kv = _kvproj(mem.reshape(nm, D_MODEL), pk["g_mem"], pk["wkv"], _pick(nm, 512))
    x2 = _xattn(x, rw, gl, kv.reshape(mem.shape[0], mem.shape[1], 2 * D_MODEL), pk["w_out"],
                pk["g_mix_post"], pk["g_x_pre"], pk["wq"], pk["wo"], pk["g_x_post"], _pick(t, 1024))
    y = _ffn(x2.reshape(n, D_MODEL), pk["g_ffn_pre"], pk["w1"], pk["w2"], pk["g_ffn_post"],
             _pick(n, 1024), 1024)
    return y.reshape(b, t, D_MODEL)


def kernel(x_prompt, x_sample, mem_prompt, mem_sample, g_mix_pre, w_in, mu_prev, mu_next, w0_f, w2_f, w0_b, w2_b, a0_f, a2_f, a0_b, a2_b, g2, k_k, k_a, r_k, lnx_w, lnx_b, gk2_f, gkb_f, gk2_b, gkb_b, gla_norm_w, w_out, g_mix_post, g_x_pre, g_mem, wq_x, wkv_x, wo_x, g_x_post, g_ffn_pre, w_ff1, w_ff2, g_ffn_post):
    params = dict(
        g_mix_pre=g_mix_pre, w_in=w_in, mu_prev=mu_prev, mu_next=mu_next, w0_f=w0_f, w2_f=w2_f,
        w0_b=w0_b, w2_b=w2_b, a0_f=a0_f, a2_f=a2_f, a0_b=a0_b, a2_b=a2_b, g2=g2, k_k=k_k, k_a=k_a,
        r_k=r_k, lnx_w=lnx_w, lnx_b=lnx_b, gk2_f=gk2_f, gkb_f=gkb_f, gk2_b=gk2_b, gkb_b=gkb_b,
        gla_norm_w=gla_norm_w, w_out=w_out, g_mix_post=g_mix_post, g_x_pre=g_x_pre, g_mem=g_mem,
        wq_x=wq_x, wkv_x=wkv_x, wo_x=wo_x, g_x_post=g_x_post, g_ffn_pre=g_ffn_pre, w_ff1=w_ff1,
        w_ff2=w_ff2, g_ffn_post=g_ffn_post)
    assert w_in.shape[0] == 1, "single-layer stack expected"
    pk = _pack_params({name: w[0] for name, w in params.items()})
    return (_trunk(x_prompt, mem_prompt, pk), _trunk(x_sample, mem_sample, pk))
```
